```python
import math
import jax, jax.numpy as jnp
from jax import lax
import numpy as np

D_MODEL = 1024
BATCH = 4
SEQ = 4096
DEPTH = 2
DEC_BATCH = 32
DEC_SEQ = 4
PAST_LEN = 8192
PAGE_SIZE = 128

N_MIXERS = 2
N_HGRN = (DEPTH + 1) // 2
N_DSA = DEPTH // 2
MEM_WIDTH = D_MODEL // 4
TOK_WIDTH = D_MODEL - MEM_WIDTH
MEM_HEADS = 4
MEM_HD = MEM_WIDTH // MEM_HEADS
N_MEM = 256
HGRN_DK = 128
HGRN_HEADS = TOK_WIDTH // HGRN_DK
HGRN_CHUNK = 32
HEAD_DIM = 128
N_Q_HEADS = TOK_WIDTH // HEAD_DIM
N_KV_HEADS = 2
IDX_HEADS = 8
IDX_DIM = 64
TOPK_MAX = 256
Q_BLOCK = 128
ROPE_THETA = 10000.0
D_FF = 2816
CONV_W = 3
EPS = 1e-6
HGRN_COLS = 4 * TOK_WIDTH + MEM_WIDTH
DSA_SIZES = (N_Q_HEADS * HEAD_DIM, N_KV_HEADS * HEAD_DIM, N_KV_HEADS * HEAD_DIM,
             IDX_HEADS * IDX_DIM, IDX_DIM, IDX_HEADS)
DSA_COLS = sum(DSA_SIZES) + MEM_WIDTH

kernel_name = 'hgrn2_dsa_memory_hybrid_step'


def rms_norm(x, gain):
    xf = x.astype(jnp.float32)
    y = xf * lax.rsqrt(jnp.mean(xf * xf, axis=-1, keepdims=True) + EPS)
    return (y * gain.astype(jnp.float32)).astype(x.dtype)


def rope(x, pos):
    half = x.shape[-1] // 2
    inv_freq = ROPE_THETA ** (-jnp.arange(half, dtype=jnp.float32) / half)
    ang = pos.astype(jnp.float32)[:, None] * inv_freq[None, :]
    cos = jnp.cos(ang)[:, None, :]
    sin = jnp.sin(ang)[:, None, :]
    xf = x.astype(jnp.float32)
    x1, x2 = xf[..., :half], xf[..., half:]
    return jnp.concatenate([x1 * cos - x2 * sin, x2 * cos + x1 * sin], axis=-1).astype(x.dtype)


def hgrn2_recurrence(q, k, v, log_f, s0):
    B, T, H, DK = q.shape
    DV = v.shape[-1]
    c = math.gcd(T, HGRN_CHUNK)
    n = T // c

    def chunks(a):
        return a.astype(jnp.float32).reshape(B, n, c, H, a.shape[-1]).transpose(1, 0, 3, 2, 4)

    causal = jnp.tril(jnp.ones((c, c), dtype=bool))

    def step(S, inp):
        qc, kc, vc, gc = inp
        b = jnp.cumsum(gc, axis=2)
        b_last = b[:, :, -1:, :]
        q_dec = qc * jnp.exp(b)
        k_dec = kc * jnp.exp(-b)
        scores = jnp.where(causal, jnp.einsum('bhtk,bhsk->bhts', q_dec, k_dec), 0.0)
        o = jnp.einsum('bhtk,bhkv->bhtv', q_dec, S) + jnp.einsum('bhts,bhsv->bhtv', scores, vc)
        S_new = (jnp.exp(b_last[:, :, 0, :])[..., None] * S
                 + jnp.einsum('bhsk,bhsv->bhkv', kc * jnp.exp(b_last - b), vc))
        return S_new, o

    S, o = lax.scan(step, s0.astype(jnp.float32), tuple(chunks(a) for a in (q, k, v, log_f)))
    return o.transpose(1, 0, 3, 2, 4).reshape(B, T, H, DV), S


def hgrn2_mixer(z, s0, lower_bound, out_norm):
    B, T, _ = z.shape
    heads = lambda a: a.reshape(B, T, HGRN_HEADS, HGRN_DK)
    q, f_pre, i, g = jnp.split(z, 4, axis=-1)
    lb = lower_bound.reshape(HGRN_HEADS, HGRN_DK)
    f = lb + (1.0 - lb) * jax.nn.sigmoid(heads(f_pre).astype(jnp.float32))
    o, s = hgrn2_recurrence(heads(jax.nn.silu(q)), 1.0 - f, heads(i), jnp.log(f), s0)
    o = rms_norm(o, out_norm) * jax.nn.silu(heads(g).astype(jnp.float32))
    return o.reshape(B, T, TOK_WIDTH).astype(z.dtype), s


def dsa_project(z, pos, q_norm, k_norm, ik_norm):
    B, T, _ = z.shape
    q, k, v, iq, ik, iw = jnp.split(z, np.cumsum(DSA_SIZES)[:-1].tolist(), axis=-1)
    q = rope(rms_norm(q.reshape(B, T, N_Q_HEADS, HEAD_DIM), q_norm), pos)
    k = rope(rms_norm(k.reshape(B, T, N_KV_HEADS, HEAD_DIM), k_norm), pos)
    v = v.reshape(B, T, N_KV_HEADS, HEAD_DIM)
    iq = rope(iq.reshape(B, T, IDX_HEADS, IDX_DIM), pos)
    ik = rope(rms_norm(ik, ik_norm)[:, :, None, :], pos)[:, :, 0, :]
    return q, k, v, iq, ik, iw


def indexer_select(iq, iw, ik, q_pos, n_sel):
    dots = jnp.einsum('bthd,bsd->bths', iq.astype(jnp.float32), ik.astype(jnp.float32)) * (IDX_DIM ** -0.5)
    score = jnp.einsum('bth,bths->bts', iw.astype(jnp.float32) * (IDX_HEADS ** -0.5), jax.nn.relu(dots))
    allowed = jnp.arange(ik.shape[1])[None, :] <= q_pos[:, None]
    score = jnp.where(allowed[None], score, -jnp.inf)
    vals, idx = lax.top_k(score, n_sel)
    return idx, jnp.isfinite(vals)


def gathered_attention(q, kg, vg, valid):
    B, T, H, D = q.shape
    qg = q.reshape(B, T, N_KV_HEADS, H // N_KV_HEADS, D)
    s = jnp.einsum('btgrd,btkgd->btgrk', qg, kg).astype(jnp.float32) * (D ** -0.5)
    s = jnp.where(valid[:, :, None, None, :], s, -jnp.inf)
    p = jax.nn.softmax(s, axis=-1).astype(vg.dtype)
    return jnp.einsum('btgrk,btkgd->btgrd', p, vg).reshape(B, T, H, D)


def dsa_prompt(q, k, v, iq, ik, iw):
    B, T = q.shape[:2]
    n_sel = min(TOPK_MAX, T // 4)
    bidx = jnp.arange(B)[:, None, None]

    def block(start):
        qb = lax.dynamic_slice_in_dim(q, start, Q_BLOCK, axis=1)
        iqb = lax.dynamic_slice_in_dim(iq, start, Q_BLOCK, axis=1)
        iwb = lax.dynamic_slice_in_dim(iw, start, Q_BLOCK, axis=1)
        idx, valid = indexer_select(iqb, iwb, ik, start + jnp.arange(Q_BLOCK), n_sel)
        return gathered_attention(qb, k[bidx, idx], v[bidx, idx], valid)

    o = lax.map(block, jnp.arange(0, T, Q_BLOCK))
    return o.transpose(1, 0, 2, 3, 4).reshape(B, T, N_Q_HEADS * HEAD_DIM)


def dsa_sample(q, k_new, v_new, iq, ik_new, iw, pool_k, pool_v, pool_ik, page_table):
    B, T = q.shape[:2]
    n_sel = min(TOPK_MAX, (PAST_LEN + T) // 4)
    ik_past = pool_ik[page_table].reshape(B, PAST_LEN, IDX_DIM)
    ik_all = jnp.concatenate([ik_past, ik_new.astype(ik_past.dtype)], axis=1)
    idx, valid = indexer_select(iq, iw, ik_all, PAST_LEN + jnp.arange(T), n_sel)
    bidx = jnp.arange(B)[:, None, None]
    is_past = (idx < PAST_LEN)[..., None, None]
    pi = jnp.minimum(idx, PAST_LEN - 1)
    phys = page_table[bidx, pi // PAGE_SIZE]
    row = pi % PAGE_SIZE
    ni = jnp.clip(idx - PAST_LEN, 0, T - 1)
    kg = jnp.where(is_past, pool_k[phys, row], k_new[bidx, ni])
    vg = jnp.where(is_past, pool_v[phys, row], v_new[bidx, ni])
    return gathered_attention(q, kg, vg, valid).reshape(B, T, N_Q_HEADS * HEAD_DIM)


def memory_keys_values(mem, w_kv, k_norm):
    B, M, _ = mem.shape
    k, v = jnp.split(mem @ w_kv, 2, axis=-1)
    return (rms_norm(k.reshape(B, M, MEM_HEADS, MEM_HD), k_norm),
            v.reshape(B, M, MEM_HEADS, MEM_HD))


def memory_attention(zq, q_norm, mk, mv):
    B, T, _ = zq.shape
    q = rms_norm(zq.reshape(B, T, MEM_HEADS, MEM_HD), q_norm)
    s = jnp.einsum('bthd,bmhd->bhtm', q, mk).astype(jnp.float32) * (MEM_HD ** -0.5)
    p = jax.nn.softmax(s, axis=-1).astype(mv.dtype)
    return jnp.einsum('bhtm,bmhd->bthd', p, mv).reshape(B, T, MEM_WIDTH)


def conv_ffn(h, conv_state, w_gate, w_up, conv_w, conv_b, w_down):
    T = h.shape[1]
    u = h @ w_gate
    ext = jnp.concatenate([conv_state.astype(u.dtype), u], axis=1)
    c = conv_b + sum(ext[:, j:j + T] * conv_w[j] for j in range(CONV_W))
    y = (jax.nn.silu(c) * (h @ w_up)) @ w_down
    return y, ext[:, T:]


def setup_inputs(seed: int = 0) -> dict:
    key = jax.random.key(seed)
    ks = list(jax.random.split(key, 40))
    nrm = lambda shape, scale=1.0: jax.random.normal(ks.pop(), shape, jnp.float32) * scale
    gain = lambda shape: 1.0 + 0.1 * jax.random.normal(ks.pop(), shape, jnp.float32)
    n_pages = PAST_LEN // PAGE_SIZE
    n_used = DEC_BATCH * n_pages
    n_pool = n_used + max(1, n_used // 4)
    page_table = jax.random.permutation(ks.pop(), n_pool)[:n_used].reshape(DEC_BATCH, n_pages).astype(jnp.int32)
    d = {}
    d['x_prompt'] = nrm((BATCH, SEQ, D_MODEL))
    d['x_sample'] = nrm((DEC_BATCH, DEC_SEQ, D_MODEL))
    d['cache_k'] = nrm((N_DSA, n_pool, PAGE_SIZE, N_KV_HEADS, HEAD_DIM))
    d['cache_v'] = nrm((N_DSA, n_pool, PAGE_SIZE, N_KV_HEADS, HEAD_DIM))
    d['cache_idx_k'] = nrm((N_DSA, n_pool, PAGE_SIZE, IDX_DIM))
    d['cache_mem_k'] = nrm((DEPTH, DEC_BATCH, N_MEM, MEM_HEADS, MEM_HD))
    d['cache_mem_v'] = nrm((DEPTH, DEC_BATCH, N_MEM, MEM_HEADS, MEM_HD))
    d['state_hgrn'] = nrm((N_HGRN, DEC_BATCH, HGRN_HEADS, HGRN_DK, HGRN_DK), 0.5)
    d['state_ffn_conv'] = nrm((DEPTH, DEC_BATCH, CONV_W - 1, D_FF))
    d['page_table'] = page_table
    d['mem_prompt'] = nrm((BATCH, N_MEM, D_MODEL))
    d['norm_mix'] = gain((DEPTH, D_MODEL))
    d['norm_ffn'] = gain((DEPTH, D_MODEL))
    d['w_in_hgrn'] = nrm((N_HGRN, D_MODEL, HGRN_COLS), D_MODEL ** -0.5)
    d['hgrn_lb_logits'] = nrm((N_HGRN + 1, TOK_WIDTH), 0.5)
    d['hgrn_out_norm'] = gain((N_HGRN, HGRN_DK))
    d['w_in_dsa'] = nrm((N_DSA, D_MODEL, DSA_COLS), D_MODEL ** -0.5)
    d['dsa_q_norm'] = gain((N_DSA, HEAD_DIM))
    d['dsa_k_norm'] = gain((N_DSA, HEAD_DIM))
    d['idx_k_norm'] = gain((N_DSA, IDX_DIM))
    d['w_mem_kv'] = nrm((DEPTH, D_MODEL, 2 * MEM_WIDTH), D_MODEL ** -0.5)
    d['mem_q_norm'] = gain((DEPTH, MEM_HD))
    d['mem_k_norm'] = gain((DEPTH, MEM_HD))
    d['w_out'] = nrm((DEPTH, D_MODEL, D_MODEL), D_MODEL ** -0.5)
    d['w_ffn_gate'] = nrm((DEPTH, D_MODEL, D_FF), D_MODEL ** -0.5)
    d['w_ffn_up'] = nrm((DEPTH, D_MODEL, D_FF), D_MODEL ** -0.5)
    d['ffn_conv_w'] = nrm((DEPTH, CONV_W, D_FF), CONV_W ** -0.5)
    d['ffn_conv_b'] = nrm((DEPTH, D_FF), 0.02)
    d['w_ffn_down'] = nrm((DEPTH, D_FF, D_MODEL), D_FF ** -0.5)
    return d


def reference(x_prompt, x_sample, cache_k, cache_v, cache_idx_k, cache_mem_k, cache_mem_v,
              state_hgrn, state_ffn_conv, page_table, mem_prompt,
              norm_mix, norm_ffn, w_in_hgrn, hgrn_lb_logits, hgrn_out_norm,
              w_in_dsa, dsa_q_norm, dsa_k_norm, idx_k_norm,
              w_mem_kv, mem_q_norm, mem_k_norm, w_out,
              w_ffn_gate, w_ffn_up, ffn_conv_w, ffn_conv_b, w_ffn_down):
    pos_p = jnp.arange(SEQ)
    pos_s = PAST_LEN + jnp.arange(DEC_SEQ)
    lower_bounds = jnp.cumsum(jax.nn.softmax(hgrn_lb_logits.astype(jnp.float32), axis=0), axis=0)
    xp, xs = x_prompt, x_sample
    bp = xp.shape[0]
    hgrn_p, hgrn_s = [], []
    kp_l, vp_l, ikp_l, ks_l, vs_l, iks_l = [], [], [], [], [], []
    mk_l, mv_l, cvp_l, cvs_l = [], [], [], []
    for layer in range(DEPTH):
        j = layer // N_MIXERS
        hp = rms_norm(xp, norm_mix[layer])
        hs = rms_norm(xs, norm_mix[layer])
        if layer % N_MIXERS == 0:
            zp = hp @ w_in_hgrn[j]
            zs = hs @ w_in_hgrn[j]
            s0 = jnp.zeros((bp, HGRN_HEADS, HGRN_DK, HGRN_DK), jnp.float32)
            tp, sp = hgrn2_mixer(zp[..., :-MEM_WIDTH], s0, lower_bounds[j], hgrn_out_norm[j])
            ts, ss = hgrn2_mixer(zs[..., :-MEM_WIDTH], state_hgrn[j], lower_bounds[j], hgrn_out_norm[j])
            hgrn_p.append(sp.astype(state_hgrn.dtype))
            hgrn_s.append(ss.astype(state_hgrn.dtype))
        else:
            zp = hp @ w_in_dsa[j]
            zs = hs @ w_in_dsa[j]
            qp, kp, vp, iqp, ikp, iwp = dsa_project(zp[..., :-MEM_WIDTH], pos_p, dsa_q_norm[j], dsa_k_norm[j], idx_k_norm[j])
            tp = dsa_prompt(qp, kp, vp, iqp, ikp, iwp)
            qs, kn, vn, iqs, ikn, iws = dsa_project(zs[..., :-MEM_WIDTH], pos_s, dsa_q_norm[j], dsa_k_norm[j], idx_k_norm[j])
            ts = dsa_sample(qs, kn, vn, iqs, ikn, iws, cache_k[j], cache_v[j], cache_idx_k[j], page_table)
            kp_l.append(kp); vp_l.append(vp); ikp_l.append(ikp)
            ks_l.append(kn); vs_l.append(vn); iks_l.append(ikn)
        mkp, mvp = memory_keys_values(mem_prompt, w_mem_kv[layer], mem_k_norm[layer])
        mk_l.append(mkp); mv_l.append(mvp)
        cp = memory_attention(zp[..., -MEM_WIDTH:], mem_q_norm[layer], mkp, mvp)
        cs = memory_attention(zs[..., -MEM_WIDTH:], mem_q_norm[layer], cache_mem_k[layer], cache_mem_v[layer])
        xp = xp + jnp.concatenate([tp, cp], axis=-1) @ w_out[layer]
        xs = xs + jnp.concatenate([ts, cs], axis=-1) @ w_out[layer]
        fp, cvp = conv_ffn(rms_norm(xp, norm_ffn[layer]), jnp.zeros((bp, CONV_W - 1, D_FF), xp.dtype),
                           w_ffn_gate[layer], w_ffn_up[layer], ffn_conv_w[layer], ffn_conv_b[layer], w_ffn_down[layer])
        fs, cvs = conv_ffn(rms_norm(xs, norm_ffn[layer]), state_ffn_conv[layer],
                           w_ffn_gate[layer], w_ffn_up[layer], ffn_conv_w[layer], ffn_conv_b[layer], w_ffn_down[layer])
        xp = xp + fp
        xs = xs + fs
        cvp_l.append(cvp); cvs_l.append(cvs)
    return (xp, xs,
            jnp.stack(hgrn_p), jnp.stack(hgrn_s),
            jnp.stack(kp_l), jnp.stack(vp_l), jnp.stack(ikp_l),
            jnp.stack(ks_l), jnp.stack(vs_l), jnp.stack(iks_l),
            jnp.stack(mk_l), jnp.stack(mv_l),
            jnp.stack(cvp_l), jnp.stack(cvs_l))
```

```python
import functools
import math

import jax
import jax.numpy as jnp
from jax import lax
from jax.experimental import pallas as pl
from jax.experimental.pallas import tpu as pltpu

D_MODEL = 1024
DEPTH = 2
PAST_LEN = 8192
PAGE_SIZE = 128
MEM_WIDTH = 256
TOK_WIDTH = 768
MEM_HEADS = 4
MEM_HD = 64
N_MEM = 256
HGRN_DK = 128
HGRN_HEADS = 6
HGRN_CHUNK = 32
HEAD_DIM = 128
N_Q_HEADS = 6
N_KV_HEADS = 2
Q_PER_KV = N_Q_HEADS // N_KV_HEADS
IDX_HEADS = 8
IDX_DIM = 64
TOPK = 256
ROPE_THETA = 10000.0
D_FF = 2816
CONV_W = 3
EPS = 1e-6

LANES = 128
SUBLANES = 8
VMEM_LIMIT_BYTES = 56 * 1024 * 1024
FF_CHUNK = 256
KEY_BLOCK = 512
Q_TILE = 128
SAMPLE_T_PAD = 8
PAGES_PER_STEP = 8
NEG_BIG = -1e30
F32 = jnp.float32
BF16 = jnp.bfloat16
I32 = jnp.int32
INT_MIN = -2 ** 31

DSA_Q0, DSA_K0, DSA_V0, DSA_IQ0, DSA_IK0, DSA_IW0, DSA_MEM0, DSA_COLS_PAD = 0, 768, 1024, 1280, 1792, 1920, 2048, 2304
HGRN_MEM0 = 4 * TOK_WIDTH


def _params(*sem):
    return pltpu.CompilerParams(dimension_semantics=sem, vmem_limit_bytes=VMEM_LIMIT_BYTES)


def _rms(x, gain):
    ms = jnp.mean(x * x, axis=-1, keepdims=True)
    return x * lax.rsqrt(ms + EPS) * gain


def _split_bf16(x):
    hi = x.astype(BF16)
    lo = (x - hi.astype(F32)).astype(BF16)
    return hi, lo


def _group_mean_sq(x, group):
    n = x.shape[-1]
    r = lax.broadcasted_iota(I32, (n, n), 0) // group
    c = lax.broadcasted_iota(I32, (n, n), 1) // group
    ones = jnp.where(r == c, 1.0, 0.0).astype(BF16)
    hi, lo = _split_bf16(x * x)
    s = jnp.dot(hi, ones, preferred_element_type=F32) + jnp.dot(lo, ones, preferred_element_type=F32)
    return s * (1.0 / group)


def _dot_nt(a, b):
    return lax.dot_general(a, b, (((1,), (1,)), ((), ())), preferred_element_type=F32)


def _dot_tn(a, b):
    return lax.dot_general(a, b, (((0,), (0,)), ((), ())), preferred_element_type=F32)


def _silu(x):
    return x * jax.nn.sigmoid(x)


def _norm_proj_kernel(x_ref, g_ref, w_ref, o_ref):
    h = _rms(x_ref[...], g_ref[...]).astype(BF16)
    o_ref[...] = jnp.dot(h, w_ref[...], preferred_element_type=F32)


def _norm_proj(x2d, gain, w_bf16, tm):
    n, d = x2d.shape
    c = w_bf16.shape[1]
    return pl.pallas_call(
        _norm_proj_kernel,
        out_shape=jax.ShapeDtypeStruct((n, c), F32),
        grid=(n // tm,),
        in_specs=[pl.BlockSpec((tm, d), lambda i: (i, 0)),
                  pl.BlockSpec((1, d), lambda i: (0, 0)),
                  pl.BlockSpec((d, c), lambda i: (0, 0))],
        out_specs=pl.BlockSpec((tm, c), lambda i: (i, 0)),
        compiler_params=_params("parallel"),
        name="norm_proj",
    )(x2d, gain.reshape(1, d), w_bf16)


def _hgrn_kernel(zq_ref, zf_ref, zi_ref, zg_ref, lb_ref, on_ref, s0_ref, o_ref, so_ref, st_ref,
                 *, chunk, n_chunks, valid_rows):
    step = pl.program_id(1)

    @pl.when(step == 0)
    def _():
        for h in range(HGRN_HEADS):
            st_ref[h] = s0_ref[0, h].T

    row = lax.broadcasted_iota(I32, (chunk, chunk), 0)
    col = lax.broadcasted_iota(I32, (chunk, chunk), 1)
    causal = row >= col
    tri = jnp.where(causal, 1.0, 0.0).astype(BF16)
    row_ok = lax.broadcasted_iota(I32, (chunk, HGRN_DK), 0) < valid_rows

    def body(ci, carry):
        r0 = pl.multiple_of(ci * chunk, chunk)
        for h in range(HGRN_HEADS):
            sl = slice(h * HGRN_DK, (h + 1) * HGRN_DK)
            rows = pl.ds(r0, chunk)
            lb = lb_ref[:, sl]
            q = _silu(zq_ref[0, rows, sl])
            f = lb + (1.0 - lb) * jax.nn.sigmoid(zf_ref[0, rows, sl])
            if valid_rows < chunk:
                f = jnp.where(row_ok, f, 1.0)
            k = 1.0 - f
            v = zi_ref[0, rows, sl]
            g_hi, g_lo = _split_bf16(jnp.log(f))
            b = jnp.dot(tri, g_hi, preferred_element_type=F32) + jnp.dot(tri, g_lo, preferred_element_type=F32)
            b_last = b[chunk - 1:chunk, :]
            q_dec = (q * jnp.exp(b)).astype(BF16)
            k_dec = (k * jnp.exp(-b)).astype(BF16)
            k_rem = (k * jnp.exp(b_last - b)).astype(BF16)
            vb = v.astype(BF16)
            scores = jnp.where(causal, _dot_nt(q_dec, k_dec), 0.0)
            st = st_ref[h]
            o = _dot_nt(q_dec, st.astype(BF16)) + jnp.dot(scores.astype(BF16), vb, preferred_element_type=F32)
            st_ref[h] = st * jnp.exp(b_last) + _dot_tn(vb, k_rem)
            o_ref[0, rows, sl] = _rms(o, on_ref[...]) * _silu(zg_ref[0, rows, sl])
        return carry

    lax.fori_loop(0, n_chunks, body, 0)

    @pl.when(step == pl.num_programs(1) - 1)
    def _():
        for h in range(HGRN_HEADS):
            so_ref[0, h] = st_ref[h].T


def _hgrn(z3d, lb, out_norm, s0, chunk, t_step, valid_rows):
    bsz, t, _ = z3d.shape
    col = lambda j: pl.BlockSpec((1, t_step, TOK_WIDTH), lambda b, i, j=j: (b, i, j))
    state_spec = pl.BlockSpec((1, HGRN_HEADS, HGRN_DK, HGRN_DK), lambda b, i: (b, 0, 0, 0))
    kern = functools.partial(_hgrn_kernel, chunk=chunk, n_chunks=t_step // chunk, valid_rows=valid_rows)
    return pl.pallas_call(
        kern,
        out_shape=(jax.ShapeDtypeStruct((bsz, t, TOK_WIDTH), F32),
                   jax.ShapeDtypeStruct((bsz, HGRN_HEADS, HGRN_DK, HGRN_DK), F32)),
        grid=(bsz, t // t_step),
        in_specs=[col(0), col(1), col(2), col(3),
                  pl.BlockSpec((1, TOK_WIDTH), lambda b, i: (0, 0)),
                  pl.BlockSpec((1, HGRN_DK), lambda b, i: (0, 0)),
                  state_spec],
        out_specs=(pl.BlockSpec((1, t_step, TOK_WIDTH), lambda b, i: (b, i, 0)), state_spec),
        scratch_shapes=[pltpu.VMEM((HGRN_HEADS, HGRN_DK, HGRN_DK), F32)],
        compiler_params=_params("parallel", "arbitrary"),
        name="hgrn2",
    )(z3d, z3d, z3d, z3d, lb.reshape(1, TOK_WIDTH), out_norm.reshape(1, HGRN_DK), s0)


def _mem_kv_kernel(m_ref, w_ref, g_ref, k_ref, v_ref):
    kv = jnp.dot(m_ref[...].astype(BF16), w_ref[...], preferred_element_type=F32)
    k = kv[:, :MEM_WIDTH]
    k_ref[...] = k * lax.rsqrt(_group_mean_sq(k, MEM_HD) + EPS) * g_ref[...]
    v_ref[...] = kv[:, MEM_WIDTH:]


def _mem_kv(mem2d, w_bf16, k_gain):
    n = mem2d.shape[0]
    out = jax.ShapeDtypeStruct((n, MEM_WIDTH), F32)
    return pl.pallas_call(
        _mem_kv_kernel,
        out_shape=(out, out),
        grid=(n // N_MEM,),
        in_specs=[pl.BlockSpec((N_MEM, D_MODEL), lambda i: (i, 0)),
                  pl.BlockSpec((D_MODEL, 2 * MEM_WIDTH), lambda i: (0, 0)),
                  pl.BlockSpec((1, MEM_WIDTH), lambda i: (0, 0))],
        out_specs=(pl.BlockSpec((N_MEM, MEM_WIDTH), lambda i: (i, 0)),
                   pl.BlockSpec((N_MEM, MEM_WIDTH), lambda i: (i, 0))),
        compiler_params=_params("parallel"),
        name="mem_kv",
    )(mem2d, w_bf16, jnp.tile(k_gain, MEM_HEADS).reshape(1, MEM_WIDTH))


def _mem_attn_kernel(zq_ref, g_ref, mk_ref, mv_ref, o_ref):
    q = zq_ref[0]
    qn = q * lax.rsqrt(_group_mean_sq(q, MEM_HD) + EPS) * g_ref[...]
    mk = mk_ref[0].astype(BF16)
    mv = mv_ref[0].astype(BF16)
    head_of_lane = lax.broadcasted_iota(I32, qn.shape, 1) // MEM_HD
    out = jnp.zeros(qn.shape, F32)
    for h in range(MEM_HEADS):
        mine = head_of_lane == h
        s = _dot_nt(jnp.where(mine, qn, 0.0).astype(BF16), mk) * (MEM_HD ** -0.5)
        p = jnp.exp(s - jnp.max(s, axis=-1, keepdims=True))
        oh = jnp.dot(p.astype(BF16), mv, preferred_element_type=F32) / jnp.sum(p, axis=-1, keepdims=True)
        out = jnp.where(mine, oh, out)
    o_ref[0] = out


def _mem_attn(z3d, q_col_block, q_gain, mk, mv, tm):
    bsz, t, _ = z3d.shape
    return pl.pallas_call(
        _mem_attn_kernel,
        out_shape=jax.ShapeDtypeStruct((bsz, t, MEM_WIDTH), F32),
        grid=(bsz, t // tm),
        in_specs=[pl.BlockSpec((1, tm, MEM_WIDTH), lambda b, i: (b, i, q_col_block)),
                  pl.BlockSpec((1, MEM_WIDTH), lambda b, i: (0, 0)),
                  pl.BlockSpec((1, N_MEM, MEM_WIDTH), lambda b, i: (b, 0, 0)),
                  pl.BlockSpec((1, N_MEM, MEM_WIDTH), lambda b, i: (b, 0, 0))],
        out_specs=pl.BlockSpec((1, tm, MEM_WIDTH), lambda b, i: (b, i, 0)),
        compiler_params=_params("parallel", "parallel"),
        name="mem_attn",
    )(z3d, jnp.tile(q_gain, MEM_HEADS).reshape(1, MEM_WIDTH), mk, mv)


def _out_proj_kernel(x_ref, t_ref, c_ref, wt_ref, wc_ref, o_ref):
    o_ref[...] = (x_ref[...]
                  + jnp.dot(t_ref[...].astype(BF16), wt_ref[...], preferred_element_type=F32)
                  + jnp.dot(c_ref[...].astype(BF16), wc_ref[...], preferred_element_type=F32))


def _out_proj(x2d, t2d, c2d, w_bf16, tm):
    n = x2d.shape[0]
    row = lambda w: pl.BlockSpec((tm, w), lambda i: (i, 0))
    return pl.pallas_call(
        _out_proj_kernel,
        out_shape=jax.ShapeDtypeStruct((n, D_MODEL), F32),
        grid=(n // tm,),
        in_specs=[row(D_MODEL), row(TOK_WIDTH), row(MEM_WIDTH),
                  pl.BlockSpec((TOK_WIDTH, D_MODEL), lambda i: (0, 0)),
                  pl.BlockSpec((MEM_WIDTH, D_MODEL), lambda i: (0, 0))],
        out_specs=row(D_MODEL),
        compiler_params=_params("parallel"),
        name="out_proj",
    )(x2d, t2d, c2d, w_bf16[:TOK_WIDTH], w_bf16[TOK_WIDTH:])


def _ffn_kernel(x_ref, g_ref, wg_ref, wu_ref, cw_ref, cb_ref, wd_ref, p1_ref, p2_ref, y_ref, u_ref, carry_ref,
                *, seq_rows, carry):
    tm = x_ref.shape[0]
    x = x_ref[...]
    h = _rms(x, g_ref[...]).astype(BF16)
    t_in_seq = lax.broadcasted_iota(I32, (tm, FF_CHUNK), 0) % seq_rows
    first_block = pl.program_id(1) == 0 if carry else None
    acc = x
    for c in range(D_FF // FF_CHUNK):
        cs = slice(c * FF_CHUNK, (c + 1) * FF_CHUNK)
        u = jnp.dot(h, wg_ref[:, cs], preferred_element_type=F32)
        up = jnp.dot(h, wu_ref[:, cs], preferred_element_type=F32)
        u1 = pltpu.roll(u, 1, 0)
        u2 = pltpu.roll(u, 2, 0)
        if carry:
            prev = carry_ref[:, cs]
            start1 = jnp.where(first_block, p1_ref[0:1, cs], prev[7:8, :])
            start2a = jnp.where(first_block, p2_ref[0:1, cs], prev[6:7, :])
            start2b = jnp.where(first_block, p2_ref[1:2, cs], prev[7:8, :])
            u1 = jnp.where(t_in_seq == 0, start1, u1)
            u2 = jnp.where(t_in_seq == 0, start2a, jnp.where(t_in_seq == 1, start2b, u2))
            carry_ref[:, cs] = u[tm - SUBLANES:, :]
        else:
            u1 = jnp.where(t_in_seq == 0, p1_ref[:, cs], u1)
            u2 = jnp.where(t_in_seq < 2, p2_ref[:, cs], u2)
        conv = cb_ref[:, cs] + u2 * cw_ref[0:1, cs] + u1 * cw_ref[1:2, cs] + u * cw_ref[2:3, cs]
        a = (_silu(conv) * up).astype(BF16)
        acc = acc + jnp.dot(a, wd_ref[cs, :], preferred_element_type=F32)
        if carry:
            u_ref[0, :, cs] = u[tm - SUBLANES:, :]
        else:
            u_ref[:, cs] = u
    y_ref[...] = acc


def _ffn_weights_specs(idx):
    once = pl.Buffered(1)
    return [pl.BlockSpec((1, D_MODEL), idx),
            pl.BlockSpec((D_MODEL, D_FF), idx, pipeline_mode=once),
            pl.BlockSpec((D_MODEL, D_FF), idx, pipeline_mode=once),
            pl.BlockSpec((CONV_W, D_FF), idx),
            pl.BlockSpec((1, D_FF), idx),
            pl.BlockSpec((D_FF, D_MODEL), idx, pipeline_mode=once)]


def _ffn_long(x3d, gain, wg, wu, cw, cb, wd, state, tm):
    bsz, t, _ = x3d.shape
    idx = lambda b, i: (0, 0)
    kern = functools.partial(_ffn_kernel, seq_rows=t, carry=True)
    x2d = x3d.reshape(bsz * t, D_MODEL)
    nb = t // tm
    p1 = state[:, 1:2, :]
    y, u_last = pl.pallas_call(
        kern,
        out_shape=(jax.ShapeDtypeStruct((bsz * t, D_MODEL), F32),
                   jax.ShapeDtypeStruct((bsz, SUBLANES, D_FF), F32)),
        grid=(bsz, nb),
        in_specs=[pl.BlockSpec((tm, D_MODEL), lambda b, i: (b * nb + i, 0))] + _ffn_weights_specs(idx)
                 + [pl.BlockSpec((None, 1, D_FF), lambda b, i: (b, 0, 0)),
                    pl.BlockSpec((None, 2, D_FF), lambda b, i: (b, 0, 0))],
        out_specs=(pl.BlockSpec((tm, D_MODEL), lambda b, i: (b * nb + i, 0)),
                   pl.BlockSpec((1, SUBLANES, D_FF), lambda b, i: (b, 0, 0))),
        scratch_shapes=[pltpu.VMEM((SUBLANES, D_FF), F32)],
        compiler_params=_params("parallel", "arbitrary"),
        name="conv_ffn_long",
    )(x2d, gain.reshape(1, D_MODEL), wg, wu, cw, cb.reshape(1, D_FF), wd, p1, state)
    return y.reshape(bsz, t, D_MODEL), u_last


def _ffn_short(x2d, gain, wg, wu, cw, cb, wd, p1, p2, seq_rows):
    n = x2d.shape[0]
    idx = lambda i: (0, 0)
    kern = functools.partial(_ffn_kernel, seq_rows=seq_rows, carry=False)
    row = lambda w: pl.BlockSpec((n, w), lambda i: (0, 0))
    return pl.pallas_call(
        kern,
        out_shape=(jax.ShapeDtypeStruct((n, D_MODEL), F32), jax.ShapeDtypeStruct((n, D_FF), F32)),
        grid=(1,),
        in_specs=[row(D_MODEL)] + _ffn_weights_specs(idx) + [row(D_FF), row(D_FF)],
        out_specs=(row(D_MODEL), row(D_FF)),
        scratch_shapes=[pltpu.VMEM((SUBLANES, D_FF), F32)],
        compiler_params=_params("arbitrary"),
        name="conv_ffn_short",
    )(x2d, gain.reshape(1, D_MODEL), wg, wu, cw, cb.reshape(1, D_FF), wd, p1, p2)


def _rope128(x, cos, sin_signed):
    return x * cos + pltpu.roll(x, HEAD_DIM // 2, 1) * sin_signed


def _rope64_pairs(x, cos, sin_lo, sin_hi):
    return (x * cos + pltpu.roll(x, LANES - IDX_DIM // 2, 1) * sin_lo
            + pltpu.roll(x, IDX_DIM // 2, 1) * sin_hi)


def _dsa_prep_kernel(z_ref, c128_ref, s128_ref, c64_ref, s64lo_ref, s64hi_ref, qg_ref, kg_ref, ikg_ref,
                     q_ref, k_ref, kb_ref, v_ref, vt_ref, iq_ref, ik_ref, ikb_ref, iwt_ref):
    c128, s128 = c128_ref[...], s128_ref[...]
    c64, s64lo, s64hi = c64_ref[...], s64lo_ref[...], s64hi_ref[...]
    for h in range(N_Q_HEADS):
        sl = slice(DSA_Q0 + h * HEAD_DIM, DSA_Q0 + (h + 1) * HEAD_DIM)
        q_ref[0, :, h * HEAD_DIM:(h + 1) * HEAD_DIM] = _rope128(_rms(z_ref[0, :, sl], qg_ref[...]), c128, s128).astype(BF16)
    for h in range(N_KV_HEADS):
        sl = slice(DSA_K0 + h * HEAD_DIM, DSA_K0 + (h + 1) * HEAD_DIM)
        k = _rope128(_rms(z_ref[0, :, sl], kg_ref[...]), c128, s128)
        k_ref[0, :, h * HEAD_DIM:(h + 1) * HEAD_DIM] = k
        kb_ref[0, :, h * HEAD_DIM:(h + 1) * HEAD_DIM] = k.astype(BF16)
    v = z_ref[0, :, DSA_V0:DSA_V0 + N_KV_HEADS * HEAD_DIM]
    v_ref[0] = v
    vt_ref[0, 0] = v.T.astype(BF16)
    for p in range(IDX_HEADS // 2):
        sl = slice(DSA_IQ0 + p * LANES, DSA_IQ0 + (p + 1) * LANES)
        iq_ref[0, :, p * LANES:(p + 1) * LANES] = _rope64_pairs(z_ref[0, :, sl], c64, s64lo, s64hi)
    ik = _rope64_pairs(_rms(z_ref[0, :, DSA_IK0:DSA_IK0 + LANES], ikg_ref[...]), c64, s64lo, s64hi)
    ik_ref[0] = ik
    ik_hi, ik_lo = _split_bf16(ik)
    ikb_ref[0, :, :LANES] = ik_hi
    ikb_ref[0, :, LANES:] = ik_lo
    iw = z_ref[0, :, DSA_IW0:DSA_IW0 + LANES] * (IDX_HEADS ** -0.5 * IDX_DIM ** -0.5)
    iwt_ref[0] = iw.T[:IDX_HEADS, :]


def _dsa_prep(z3d, tabs, q_gain, k_gain, ik_gain, tm):
    bsz, t, _ = z3d.shape
    nb = t // tm
    tab = pl.BlockSpec((tm, LANES), lambda b, i: (i, 0))
    gain = pl.BlockSpec((1, LANES), lambda b, i: (0, 0))
    rows = lambda w: pl.BlockSpec((1, tm, w), lambda b, i: (b, i, 0))
    kvw = N_KV_HEADS * HEAD_DIM
    sds = jax.ShapeDtypeStruct
    return pl.pallas_call(
        _dsa_prep_kernel,
        out_shape=(sds((bsz, t, TOK_WIDTH), BF16),
                   sds((bsz, t, kvw), F32), sds((bsz, t, kvw), BF16),
                   sds((bsz, t, kvw), F32), sds((bsz, nb, kvw, tm), BF16),
                   sds((bsz, t, IDX_HEADS * IDX_DIM), F32),
                   sds((bsz, t, LANES), F32),
                   sds((bsz, t, 2 * LANES), BF16),
                   sds((bsz, IDX_HEADS, t), F32)),
        grid=(bsz, nb),
        in_specs=[rows(DSA_COLS_PAD), tab, tab, tab, tab, tab, gain, gain, gain],
        out_specs=(rows(TOK_WIDTH), rows(kvw), rows(kvw), rows(kvw),
                   pl.BlockSpec((1, 1, kvw, tm), lambda b, i: (b, i, 0, 0)),
                   rows(IDX_HEADS * IDX_DIM), rows(LANES), rows(2 * LANES),
                   pl.BlockSpec((1, IDX_HEADS, tm), lambda b, i: (b, 0, i))),
        compiler_params=_params("parallel", "parallel"),
        name="dsa_prep",
    )(z3d, *tabs, q_gain.reshape(1, LANES), k_gain.reshape(1, LANES), jnp.tile(ik_gain, 2).reshape(1, LANES))


def _rope_tables(pos):
    pos = pos.astype(F32)[:, None]

    def cs(half):
        inv = ROPE_THETA ** (-jnp.arange(half, dtype=F32) / half)
        ang = pos * inv[None, :]
        return jnp.cos(ang), jnp.sin(ang)

    c, s = cs(HEAD_DIM // 2)
    c128 = jnp.concatenate([c, c], axis=1)
    s128 = jnp.concatenate([-s, s], axis=1)
    c, s = cs(IDX_DIM // 2)
    z = jnp.zeros_like(s)
    c64 = jnp.concatenate([c, c, c, c], axis=1)
    s64lo = jnp.concatenate([-s, z, -s, z], axis=1)
    s64hi = jnp.concatenate([z, s, z, s], axis=1)
    return c128, s128, c64, s64lo, s64hi


def _ordered_bits_to_float(u):
    key = u ^ INT_MIN
    bits = key ^ ((key >> 31) & 0x7FFFFFFF)
    return lax.bitcast_convert_type(bits, F32)


def _topk_threshold(count_ge, count_gt, count_eq_before, any_true, shape, idx_bits):
    def value_step(i, prefix):
        cand = prefix | jnp.left_shift(jnp.int32(1), 31 - i)
        below_neg_inf = jnp.logical_and(cand >= 0, cand < 0x00800000)
        ok = jnp.logical_or(count_ge(_ordered_bits_to_float(cand)) >= TOPK, below_neg_inf)
        return jnp.where(ok, cand, prefix)

    thr = _ordered_bits_to_float(lax.fori_loop(0, 32, value_step, jnp.zeros(shape, I32)))
    n_gt = count_gt(thr)
    need = TOPK - n_gt
    n_eq = count_ge(thr) - n_gt
    finite_thr = thr > -jnp.inf
    excess = jnp.logical_and(n_eq > need, finite_thr)

    def index_search():
        def index_step(i, cut):
            cand = cut | jnp.left_shift(jnp.int32(1), idx_bits - 1 - i)
            return jnp.where(count_eq_before(thr, cand) < need, cand, cut)
        return lax.fori_loop(0, idx_bits, index_step, jnp.zeros(shape, I32))

    everything = jnp.full(shape, 2 ** idx_bits - 1, I32)
    cut = lax.cond(any_true(excess), index_search, lambda: everything)
    return thr, jnp.where(finite_thr, cut, -1)


def _dsa_prompt_kernel(q_ref, iq_ref, iwt_ref, ik_ref, k_ref, vt_ref, o_ref,
                       score_ref, iqm_ref, m_ref, l_ref, acc_ref):
    qi = pl.program_id(1)
    q0 = qi * Q_TILE
    n_kb = (q0 + Q_TILE + KEY_BLOCK - 1) // KEY_BLOCK
    lane = lax.broadcasted_iota(I32, (Q_TILE, LANES), 1)
    for p in range(IDX_HEADS // 2):
        pair = iq_ref[0, :, p * LANES:(p + 1) * LANES]
        for half in range(2):
            mine = (lane < IDX_DIM) if half == 0 else (lane >= IDX_DIM)
            parts = _split_bf16(jnp.where(mine, pair, 0.0))
            rows = slice(half * Q_TILE, (half + 1) * Q_TILE)
            for part in range(2):
                iqm_ref[p, part, rows, :LANES] = parts[part]
                iqm_ref[p, part, rows, LANES:] = parts[part]
    q_pos = q0 + lax.broadcasted_iota(I32, (1, Q_TILE), 1)
    key_row = lax.broadcasted_iota(I32, (KEY_BLOCK, 1), 0)

    def score_block(kb, carry):
        s0 = pl.multiple_of(kb * KEY_BLOCK, KEY_BLOCK)
        ik = ik_ref[0, pl.ds(s0, KEY_BLOCK), :]
        acc = jnp.zeros((KEY_BLOCK, Q_TILE), F32)
        for p in range(IDX_HEADS // 2):
            d = _dot_nt(ik, iqm_ref[p, 0]) + _dot_nt(ik, iqm_ref[p, 1])
            for half in range(2):
                h = 2 * p + half
                acc = acc + iwt_ref[0, h:h + 1, :] * jnp.maximum(d[:, half * Q_TILE:(half + 1) * Q_TILE], 0.0)
        score_ref[pl.ds(s0, KEY_BLOCK), :] = jnp.where(s0 + key_row <= q_pos, acc, -jnp.inf)
        return carry

    lax.fori_loop(0, n_kb, score_block, 0)

    def count(pred):
        def body(kb, cnt):
            s0 = pl.multiple_of(kb * KEY_BLOCK, KEY_BLOCK)
            hit = pred(score_ref[pl.ds(s0, KEY_BLOCK), :], s0 + key_row).astype(I32)
            return cnt + jnp.sum(hit.reshape(KEY_BLOCK // SUBLANES, SUBLANES, Q_TILE), axis=0)
        part = lax.fori_loop(0, n_kb, body, jnp.zeros((SUBLANES, Q_TILE), I32))
        return jnp.sum(part, axis=0, keepdims=True)

    thr, cut = _topk_threshold(
        count_ge=lambda c: count(lambda s, idx: s >= c),
        count_gt=lambda c: count(lambda s, idx: s > c),
        count_eq_before=lambda t, j: count(lambda s, idx: jnp.logical_and(s == t, idx < j)),
        any_true=lambda m: jnp.max(m.astype(I32)) > 0,
        shape=(1, Q_TILE), idx_bits=13)

    m_ref[...] = jnp.full(m_ref.shape, NEG_BIG, F32)
    l_ref[...] = jnp.zeros(l_ref.shape, F32)
    acc_ref[...] = jnp.zeros(acc_ref.shape, F32)

    def attend_block(kb, carry):
        s0 = pl.multiple_of(kb * KEY_BLOCK, KEY_BLOCK)
        sc = score_ref[pl.ds(s0, KEY_BLOCK), :]
        sel = jnp.logical_or(sc > thr, jnp.logical_and(sc == thr, s0 + key_row <= cut))
        for g in range(N_KV_HEADS):
            kg = k_ref[0, pl.ds(s0, KEY_BLOCK), g * HEAD_DIM:(g + 1) * HEAD_DIM]
            vtg = vt_ref[0, kb, g * HEAD_DIM:(g + 1) * HEAD_DIM, :]
            for r in range(Q_PER_KV):
                h = g * Q_PER_KV + r
                s = _dot_nt(kg, q_ref[0, :, h * HEAD_DIM:(h + 1) * HEAD_DIM]) * (HEAD_DIM ** -0.5)
                s = jnp.where(sel, s, NEG_BIG)
                m_old = m_ref[h:h + 1, :]
                m_new = jnp.maximum(m_old, jnp.max(s, axis=0, keepdims=True))
                alpha = jnp.exp(m_old - m_new)
                p = jnp.exp(s - m_new)
                l_ref[h:h + 1, :] = alpha * l_ref[h:h + 1, :] + jnp.sum(p, axis=0, keepdims=True)
                acc_ref[h] = alpha * acc_ref[h] + jnp.dot(vtg, p.astype(BF16), preferred_element_type=F32)
                m_ref[h:h + 1, :] = m_new
        return carry

    lax.fori_loop(0, n_kb, attend_block, 0)
    for h in range(N_Q_HEADS):
        o_ref[0, :, h * HEAD_DIM:(h + 1) * HEAD_DIM] = (acc_ref[h] / l_ref[h:h + 1, :]).T


def _dsa_prompt(q, iq, iwt, ikb, kb, vt):
    bsz, t, _ = q.shape
    kvw = N_KV_HEADS * HEAD_DIM
    full = lambda w: pl.BlockSpec((1, t, w), lambda b, i: (b, 0, 0))
    return pl.pallas_call(
        _dsa_prompt_kernel,
        out_shape=jax.ShapeDtypeStruct((bsz, t, TOK_WIDTH), F32),
        grid=(bsz, t // Q_TILE),
        in_specs=[pl.BlockSpec((1, Q_TILE, TOK_WIDTH), lambda b, i: (b, i, 0)),
                  pl.BlockSpec((1, Q_TILE, IDX_HEADS * IDX_DIM), lambda b, i: (b, i, 0)),
                  pl.BlockSpec((1, IDX_HEADS, Q_TILE), lambda b, i: (b, 0, i)),
                  full(2 * LANES), full(kvw),
                  pl.BlockSpec((1, t // KEY_BLOCK, kvw, KEY_BLOCK), lambda b, i: (b, 0, 0, 0))],
        out_specs=pl.BlockSpec((1, Q_TILE, TOK_WIDTH), lambda b, i: (b, i, 0)),
        scratch_shapes=[pltpu.VMEM((t, Q_TILE), F32),
                        pltpu.VMEM((IDX_HEADS // 2, 2, 2 * Q_TILE, 2 * LANES), BF16),
                        pltpu.VMEM((SUBLANES, Q_TILE), F32),
                        pltpu.VMEM((SUBLANES, Q_TILE), F32),
                        pltpu.VMEM((N_Q_HEADS, HEAD_DIM, Q_TILE), F32)],
        compiler_params=_params("parallel", "arbitrary"),
        name="dsa_prompt",
    )(q, iq, iwt, ikb, kb, vt)


SAMPLE_ROWS = 16
SAMPLE_T = 4
N_PAGES = PAST_LEN // PAGE_SIZE
N_KEY_PAGES = N_PAGES + 1


def _sample_score_kernel(pt_ref, iqm_ref, iw_ref, ikn_ref, *rest):
    page_refs = rest[:PAGES_PER_STEP]
    score_out, thr_out, cut_out, score_ref = rest[PAGES_PER_STEP:]
    step = pl.program_id(1)

    def page_scores(ik_f32, page, allowed):
        ik_hi, ik_lo = _split_bf16(ik_f32)
        for t in range(SAMPLE_T):
            iq_hi, iq_lo = _split_bf16(iqm_ref[0, t])
            d = _dot_nt(iq_hi, ik_hi) + _dot_nt(iq_hi, ik_lo) + _dot_nt(iq_lo, ik_hi)
            sc = jnp.sum(jnp.maximum(d, 0.0) * iw_ref[0, t], axis=0, keepdims=True)
            if allowed is not None:
                sc = jnp.where(allowed[t], sc, -jnp.inf)
            for r in range(SAMPLE_ROWS // SAMPLE_T):
                score_ref[page, r * SAMPLE_T + t:r * SAMPLE_T + t + 1, :] = sc

    for i in range(PAGES_PER_STEP):
        page_scores(page_refs[i][0], step * PAGES_PER_STEP + i, None)

    @pl.when(step == pl.num_programs(1) - 1)
    def _():
        new_idx = lax.broadcasted_iota(I32, (1, PAGE_SIZE), 1)
        page_scores(ikn_ref[0], N_PAGES, [new_idx <= t for t in range(SAMPLE_T)])
        key_idx = (lax.broadcasted_iota(I32, (N_KEY_PAGES, 1, PAGE_SIZE), 0) * PAGE_SIZE
                   + lax.broadcasted_iota(I32, (N_KEY_PAGES, 1, PAGE_SIZE), 2))

        def count(pred):
            hit = pred(score_ref[...], key_idx).astype(I32)
            return jnp.sum(jnp.sum(hit, axis=0), axis=-1, keepdims=True)

        thr, cut = _topk_threshold(
            count_ge=lambda c: count(lambda s, idx: s >= c[None]),
            count_gt=lambda c: count(lambda s, idx: s > c[None]),
            count_eq_before=lambda t, j: count(lambda s, idx: jnp.logical_and(s == t[None], idx < j[None])),
            any_true=lambda m: jnp.max(m.astype(I32)) > 0,
            shape=(SAMPLE_ROWS, 1), idx_bits=14)
        score_out[0] = score_ref[...]
        thr_out[0] = jnp.broadcast_to(thr, (SAMPLE_ROWS, LANES))
        cut_out[0] = jnp.broadcast_to(cut, (SAMPLE_ROWS, LANES))


def _sample_scores(page_table, iqm, iw_col, ik_new_pad, pool_ik):
    bsz = iqm.shape[0]
    page_spec = lambda i: pl.BlockSpec((1, PAGE_SIZE, IDX_DIM),
                                       lambda b, s, pt, i=i: (pt[b, s * PAGES_PER_STEP + i], 0, 0))
    per_b = lambda shape: pl.BlockSpec((1,) + shape, lambda b, s, pt: (b,) + (0,) * len(shape))
    sds = jax.ShapeDtypeStruct
    grid_spec = pltpu.PrefetchScalarGridSpec(
        num_scalar_prefetch=1,
        grid=(bsz, N_PAGES // PAGES_PER_STEP),
        in_specs=[per_b((SAMPLE_T, IDX_HEADS, IDX_DIM)), per_b((SAMPLE_T, IDX_HEADS, 1)),
                  per_b((PAGE_SIZE, IDX_DIM))] + [page_spec(i) for i in range(PAGES_PER_STEP)],
        out_specs=(per_b((N_KEY_PAGES, SAMPLE_ROWS, PAGE_SIZE)), per_b((SAMPLE_ROWS, LANES)),
                   per_b((SAMPLE_ROWS, LANES))),
        scratch_shapes=[pltpu.VMEM((N_KEY_PAGES, SAMPLE_ROWS, PAGE_SIZE), F32)])
    return pl.pallas_call(
        _sample_score_kernel,
        out_shape=(sds((bsz, N_KEY_PAGES, SAMPLE_ROWS, PAGE_SIZE), F32),
                   sds((bsz, SAMPLE_ROWS, LANES), F32), sds((bsz, SAMPLE_ROWS, LANES), I32)),
        grid_spec=grid_spec,
        compiler_params=_params("parallel", "arbitrary"),
        name="dsa_sample_scores",
    )(page_table, iqm, iw_col, ik_new_pad, *([pool_ik] * PAGES_PER_STEP))


def _sample_attn_kernel(pt_ref, q_ref, sc_ref, thr_ref, cut_ref, kn_ref, vn_ref, *rest):
    k_pages = rest[:PAGES_PER_STEP]
    v_pages = rest[PAGES_PER_STEP:2 * PAGES_PER_STEP]
    o_ref, m_ref, l_ref, acc_ref = rest[2 * PAGES_PER_STEP:]
    step = pl.program_id(1)

    @pl.when(step == 0)
    def _():
        m_ref[...] = jnp.full(m_ref.shape, NEG_BIG, F32)
        l_ref[...] = jnp.zeros(l_ref.shape, F32)
        acc_ref[...] = jnp.zeros(acc_ref.shape, F32)

    thr = thr_ref[0]
    cut = cut_ref[0]
    lane = lax.broadcasted_iota(I32, (SAMPLE_ROWS, PAGE_SIZE), 1)

    def attend(pages):
        sels = []
        for page, _, _ in pages:
            sc = sc_ref[0, page]
            sels.append(jnp.logical_or(sc > thr, jnp.logical_and(sc == thr, page * PAGE_SIZE + lane <= cut)))
        for g in range(N_KV_HEADS):
            gs = slice(g * HEAD_DIM, (g + 1) * HEAD_DIM)
            q = q_ref[0, g]
            ss = [jnp.where(sel, _dot_nt(q, k[:, gs].astype(BF16)) * (HEAD_DIM ** -0.5), NEG_BIG)
                  for sel, (_, k, _) in zip(sels, pages)]
            blk_max = functools.reduce(jnp.maximum, ss)
            m_old = m_ref[g]
            m_new = jnp.maximum(m_old, jnp.max(blk_max, axis=-1, keepdims=True))
            alpha = jnp.exp(m_old - m_new)
            ps = [jnp.exp(s - m_new) for s in ss]
            l_ref[g] = alpha * l_ref[g] + jnp.sum(functools.reduce(jnp.add, ps), axis=-1, keepdims=True)
            pv = [jnp.dot(p.astype(BF16), v[:, gs].astype(BF16), preferred_element_type=F32)
                  for p, (_, _, v) in zip(ps, pages)]
            acc_ref[g] = alpha * acc_ref[g] + functools.reduce(jnp.add, pv)
            m_ref[g] = m_new

    attend([(step * PAGES_PER_STEP + i, k_pages[i][0], v_pages[i][0]) for i in range(PAGES_PER_STEP)])

    @pl.when(step == pl.num_programs(1) - 1)
    def _():
        attend([(N_PAGES, kn_ref[0], vn_ref[0])])
        for g in range(N_KV_HEADS):
            o_ref[0, g] = acc_ref[g] / l_ref[g]


def _sample_attn(page_table, q16, scores, thr, cut, k_new_pad, v_new_pad, pool_k, pool_v):
    bsz = q16.shape[0]
    kvw = N_KV_HEADS * HEAD_DIM
    page_spec = lambda i: pl.BlockSpec((1, PAGE_SIZE, kvw),
                                       lambda b, s, pt, i=i: (pt[b, s * PAGES_PER_STEP + i], 0, 0))
    per_b = lambda shape: pl.BlockSpec((1,) + shape, lambda b, s, pt: (b,) + (0,) * len(shape))
    pages = [page_spec(i) for i in range(PAGES_PER_STEP)]
    grid_spec = pltpu.PrefetchScalarGridSpec(
        num_scalar_prefetch=1,
        grid=(bsz, N_PAGES // PAGES_PER_STEP),
        in_specs=[per_b((N_KV_HEADS, SAMPLE_ROWS, HEAD_DIM)), per_b((N_KEY_PAGES, SAMPLE_ROWS, PAGE_SIZE)),
                  per_b((SAMPLE_ROWS, LANES)), per_b((SAMPLE_ROWS, LANES)),
                  per_b((PAGE_SIZE, kvw)), per_b((PAGE_SIZE, kvw))] + pages + pages,
        out_specs=per_b((N_KV_HEADS, SAMPLE_ROWS, HEAD_DIM)),
        scratch_shapes=[pltpu.VMEM((N_KV_HEADS, SAMPLE_ROWS, 1), F32),
                        pltpu.VMEM((N_KV_HEADS, SAMPLE_ROWS, 1), F32),
                        pltpu.VMEM((N_KV_HEADS, SAMPLE_ROWS, HEAD_DIM), F32)])
    return pl.pallas_call(
        _sample_attn_kernel,
        out_shape=jax.ShapeDtypeStruct((bsz, N_KV_HEADS, SAMPLE_ROWS, HEAD_DIM), F32),
        grid_spec=grid_spec,
        compiler_params=_params("parallel", "arbitrary"),
        name="dsa_sample_attn",
    )(page_table, q16, scores, thr, cut, k_new_pad, v_new_pad,
      *([pool_k] * PAGES_PER_STEP), *([pool_v] * PAGES_PER_STEP))


def _pad_cols(w, width):
    return jnp.pad(w, ((0, 0), (0, width - w.shape[1])))


def _dsa_weight(w):
    sizes = (TOK_WIDTH, N_KV_HEADS * HEAD_DIM, N_KV_HEADS * HEAD_DIM, IDX_HEADS * IDX_DIM, IDX_DIM, IDX_HEADS, MEM_WIDTH)
    offs = [0]
    for s in sizes:
        offs.append(offs[-1] + s)
    q, k, v, iq, ik, iw, mq = [w[:, offs[i]:offs[i + 1]] for i in range(len(sizes))]
    return jnp.concatenate([q, k, v, iq, ik, ik, _pad_cols(iw, LANES), mq], axis=1).astype(BF16)


def kernel(x_prompt, x_sample, cache_k, cache_v, cache_idx_k, cache_mem_k, cache_mem_v, state_hgrn, state_ffn_conv, page_table, mem_prompt, norm_mix, norm_ffn, w_in_hgrn, hgrn_lb_logits, hgrn_out_norm, w_in_dsa, dsa_q_norm, dsa_k_norm, idx_k_norm, w_mem_kv, mem_q_norm, mem_k_norm, w_out, w_ffn_gate, w_ffn_up, ffn_conv_w, ffn_conv_b, w_ffn_down):
    bp, seq, _ = x_prompt.shape
    bs, t_s, _ = x_sample.shape
    assert t_s == SAMPLE_T and seq % KEY_BLOCK == 0
    n_p = bp * seq
    n_s = bs * SAMPLE_T_PAD
    kvw = N_KV_HEADS * HEAD_DIM
    lower_bounds = jnp.cumsum(jax.nn.softmax(hgrn_lb_logits.astype(F32), axis=0), axis=0)

    xp = x_prompt
    xs = jnp.pad(x_sample, ((0, 0), (0, SAMPLE_T_PAD - t_s), (0, 0)))
    mem2d = mem_prompt.reshape(bp * N_MEM, D_MODEL)
    tabs_p = _rope_tables(jnp.arange(seq))
    pos_s = PAST_LEN + jnp.arange(SAMPLE_T_PAD)
    tabs_s = tuple(jnp.tile(tb, (bs, 1)) for tb in _rope_tables(pos_s))
    row_t = jnp.arange(n_s) % SAMPLE_T_PAD

    hgrn_p, hgrn_s = [], []
    kp_l, vp_l, ikp_l, ks_l, vs_l, iks_l = [], [], [], [], [], []
    mk_l, mv_l, cvp_l, cvs_l = [], [], [], []
    for layer in range(DEPTH):
        j = layer // 2
        if layer % 2 == 0:
            w_in = w_in_hgrn[j].astype(BF16)
            mem_col = HGRN_MEM0 // MEM_WIDTH
        else:
            w_in = _dsa_weight(w_in_dsa[j])
            mem_col = DSA_MEM0 // MEM_WIDTH
        cols = w_in.shape[1]
        zp = _norm_proj(xp.reshape(n_p, D_MODEL), norm_mix[layer], w_in, 256).reshape(bp, seq, cols)
        zs = _norm_proj(xs.reshape(n_s, D_MODEL), norm_mix[layer], w_in, n_s).reshape(bs, SAMPLE_T_PAD, cols)
        if layer % 2 == 0:
            s0 = jnp.zeros((bp, HGRN_HEADS, HGRN_DK, HGRN_DK), F32)
            tp, sp = _hgrn(zp, lower_bounds[j], hgrn_out_norm[j], s0, HGRN_CHUNK, 512, HGRN_CHUNK)
            ts, ss = _hgrn(zs, lower_bounds[j], hgrn_out_norm[j], state_hgrn[j], SAMPLE_T_PAD, SAMPLE_T_PAD,
                           math.gcd(t_s, HGRN_CHUNK))
            hgrn_p.append(sp.astype(state_hgrn.dtype))
            hgrn_s.append(ss.astype(state_hgrn.dtype))
        else:
            qp, kp, kpb, vp, vtp, iqp, ikp, ikpb, iwtp = _dsa_prep(
                zp, tabs_p, dsa_q_norm[j], dsa_k_norm[j], idx_k_norm[j], KEY_BLOCK)
            tp = _dsa_prompt(qp, iqp, iwtp, ikpb, kpb, vtp)
            kp_l.append(kp.reshape(bp, seq, N_KV_HEADS, HEAD_DIM))
            vp_l.append(vp.reshape(bp, seq, N_KV_HEADS, HEAD_DIM))
            ikp_l.append(ikp[..., :IDX_DIM])

            qs, kn, _, vn, _, iqs, ikn, _, iwts = _dsa_prep(
                zs.reshape(1, n_s, cols), tabs_s, dsa_q_norm[j], dsa_k_norm[j], idx_k_norm[j], n_s)
            sel_t = lambda a: a.reshape(bs, SAMPLE_T_PAD, -1)[:, :t_s]
            kn, vn, ikn = sel_t(kn), sel_t(vn), sel_t(ikn)[..., :IDX_DIM]
            ks_l.append(kn.reshape(bs, t_s, N_KV_HEADS, HEAD_DIM))
            vs_l.append(vn.reshape(bs, t_s, N_KV_HEADS, HEAD_DIM))
            iks_l.append(ikn)
            iqm = sel_t(iqs).reshape(bs, t_s, IDX_HEADS, IDX_DIM)
            iw_col = sel_t(iwts[0].T).reshape(bs, t_s, IDX_HEADS, 1)
            pad_rows = lambda a: jnp.pad(a, ((0, 0), (0, PAGE_SIZE - t_s), (0, 0)))
            scores, thr, cut = _sample_scores(page_table, iqm, iw_col, pad_rows(ikn), cache_idx_k[j])
            q16 = sel_t(qs).reshape(bs, t_s, N_KV_HEADS, Q_PER_KV, HEAD_DIM).transpose(0, 2, 3, 1, 4)
            q16 = jnp.pad(q16.reshape(bs, N_KV_HEADS, Q_PER_KV * t_s, HEAD_DIM),
                          ((0, 0), (0, 0), (0, SAMPLE_ROWS - Q_PER_KV * t_s), (0, 0)))
            pool_k = cache_k[j].reshape(-1, PAGE_SIZE, kvw)
            pool_v = cache_v[j].reshape(-1, PAGE_SIZE, kvw)
            o16 = _sample_attn(page_table, q16, scores, thr, cut, pad_rows(kn), pad_rows(vn), pool_k, pool_v)
            o = o16[:, :, :Q_PER_KV * t_s].reshape(bs, N_KV_HEADS, Q_PER_KV, t_s, HEAD_DIM)
            ts = jnp.pad(o.transpose(0, 3, 1, 2, 4).reshape(bs, t_s, TOK_WIDTH),
                         ((0, 0), (0, SAMPLE_T_PAD - t_s), (0, 0)))

        mkp, mvp = _mem_kv(mem2d, w_mem_kv[layer].astype(BF16), mem_k_norm[layer])
        mk_l.append(mkp.reshape(bp, N_MEM, MEM_HEADS, MEM_HD))
        mv_l.append(mvp.reshape(bp, N_MEM, MEM_HEADS, MEM_HD))
        cp = _mem_attn(zp, mem_col, mem_q_norm[layer], mkp.reshape(bp, N_MEM, MEM_WIDTH),
                       mvp.reshape(bp, N_MEM, MEM_WIDTH), 512)
        cs = _mem_attn(zs, mem_col, mem_q_norm[layer], cache_mem_k[layer].reshape(bs, N_MEM, MEM_WIDTH),
                       cache_mem_v[layer].reshape(bs, N_MEM, MEM_WIDTH), SAMPLE_T_PAD)
        w_o = w_out[layer].astype(BF16)
        xp = _out_proj(xp.reshape(n_p, D_MODEL), tp.reshape(n_p, TOK_WIDTH), cp.reshape(n_p, MEM_WIDTH),
                       w_o, 512).reshape(bp, seq, D_MODEL)
        xs2 = _out_proj(xs.reshape(n_s, D_MODEL), ts.reshape(n_s, TOK_WIDTH), cs.reshape(n_s, MEM_WIDTH), w_o, n_s)

        wg, wu, wd = (w.astype(BF16) for w in (w_ffn_gate[layer], w_ffn_up[layer], w_ffn_down[layer]))
        ffn_w = (norm_ffn[layer], wg, wu, ffn_conv_w[layer], ffn_conv_b[layer], wd)
        xp, u_last = _ffn_long(xp, *ffn_w, jnp.zeros((bp, CONV_W - 1, D_FF), F32), 512)
        cvp_l.append(u_last[:, SUBLANES - (CONV_W - 1):, :])
        st = state_ffn_conv[layer]
        zrow = jnp.zeros((bs, SAMPLE_T_PAD - 1, D_FF), F32)
        p1 = jnp.concatenate([st[:, 1:2], zrow], axis=1).reshape(n_s, D_FF)
        p2 = jnp.concatenate([st, zrow[:, 1:]], axis=1).reshape(n_s, D_FF)
        xs2, u_s = _ffn_short(xs2, *ffn_w, p1, p2, SAMPLE_T_PAD)
        cvs_l.append(u_s.reshape(bs, SAMPLE_T_PAD, D_FF)[:, t_s - (CONV_W - 1):t_s])
        xs = jnp.where((row_t < t_s)[:, None], xs2, 0.0).reshape(bs, SAMPLE_T_PAD, D_MODEL)

    return (xp, xs[:, :t_s],
            jnp.stack(hgrn_p), jnp.stack(hgrn_s),
            jnp.stack(kp_l), jnp.stack(vp_l), jnp.stack(ikp_l),
            jnp.stack(ks_l), jnp.stack(vs_l), jnp.stack(iks_l),
            jnp.stack(mk_l), jnp.stack(mv_l),
            jnp.stack(cvp_l), jnp.stack(cvs_l))
```

```python
import functools
import math

import jax
import jax.numpy as jnp
from jax import lax
from jax.experimental import pallas as pl
from jax.experimental.pallas import tpu as pltpu

D_MODEL = 1024
DEPTH = 2
PAST_LEN = 8192
PAGE_SIZE = 128
MEM_WIDTH = 256
TOK_WIDTH = 768
MEM_HEADS = 4
MEM_HD = 64
N_MEM = 256
HGRN_DK = 128
HGRN_HEADS = 6
HGRN_CHUNK = 32
HEAD_DIM = 128
N_Q_HEADS = 6
N_KV_HEADS = 2
Q_PER_KV = N_Q_HEADS // N_KV_HEADS
IDX_HEADS = 8
IDX_DIM = 64
TOPK = 256
ROPE_THETA = 10000.0
D_FF = 2816
CONV_W = 3
EPS = 1e-6

LANES = 128
SUBLANES = 8
VMEM_LIMIT_BYTES = 56 * 1024 * 1024
FF_CHUNK = 256
KEY_BLOCK = 512
Q_TILE = 128
SAMPLE_T_PAD = 8
PAGES_PER_STEP = 8
NEG_BIG = -1e30
QK_SCALE_LOG2E = HEAD_DIM ** -0.5 * math.log2(math.e)
F32 = jnp.float32
BF16 = jnp.bfloat16
I32 = jnp.int32
INT_MIN = -2 ** 31

DSA_Q0, DSA_K0, DSA_V0, DSA_IQ0, DSA_IK0, DSA_IW0, DSA_MEM0, DSA_COLS_PAD = 0, 768, 1024, 1280, 1792, 1920, 2048, 2304
HGRN_MEM0 = 4 * TOK_WIDTH


def _params(*sem):
    return pltpu.CompilerParams(dimension_semantics=sem, vmem_limit_bytes=VMEM_LIMIT_BYTES)


def _rms(x, gain):
    ms = jnp.mean(x * x, axis=-1, keepdims=True)
    return x * lax.rsqrt(ms + EPS) * gain


def _split_bf16(x):
    hi = x.astype(BF16)
    lo = (x - hi.astype(F32)).astype(BF16)
    return hi, lo


def _group_mean_sq(x, group):
    n = x.shape[-1]
    r = lax.broadcasted_iota(I32, (n, n), 0) // group
    c = lax.broadcasted_iota(I32, (n, n), 1) // group
    ones = jnp.where(r == c, 1.0, 0.0).astype(BF16)
    hi, lo = _split_bf16(x * x)
    s = jnp.dot(hi, ones, preferred_element_type=F32) + jnp.dot(lo, ones, preferred_element_type=F32)
    return s * (1.0 / group)


def _dot_nt(a, b):
    return lax.dot_general(a, b, (((1,), (1,)), ((), ())), preferred_element_type=F32)


def _dot_tn(a, b):
    return lax.dot_general(a, b, (((0,), (0,)), ((), ())), preferred_element_type=F32)


def _silu(x):
    return x * jax.nn.sigmoid(x)


def _norm_proj_kernel(x_ref, g_ref, w_ref, o_ref):
    h = _rms(x_ref[...], g_ref[...]).astype(BF16)
    o_ref[...] = jnp.dot(h, w_ref[...], preferred_element_type=F32)


def _norm_proj(x2d, gain, w_bf16, tm):
    n, d = x2d.shape
    c = w_bf16.shape[1]
    return pl.pallas_call(
        _norm_proj_kernel,
        out_shape=jax.ShapeDtypeStruct((n, c), F32),
        grid=(n // tm,),
        in_specs=[pl.BlockSpec((tm, d), lambda i: (i, 0)),
                  pl.BlockSpec((1, d), lambda i: (0, 0)),
                  pl.BlockSpec((d, c), lambda i: (0, 0))],
        out_specs=pl.BlockSpec((tm, c), lambda i: (i, 0)),
        compiler_params=_params("parallel"),
        name="norm_proj",
    )(x2d, gain.reshape(1, d), w_bf16)


def _hgrn_kernel(zq_ref, zf_ref, zi_ref, zg_ref, lb_ref, on_ref, s0_ref, o_ref, so_ref, st_ref,
                 *, chunk, n_chunks, valid_rows):
    step = pl.program_id(1)

    @pl.when(step == 0)
    def _():
        for h in range(HGRN_HEADS):
            st_ref[h] = s0_ref[0, h].T

    row = lax.broadcasted_iota(I32, (chunk, chunk), 0)
    col = lax.broadcasted_iota(I32, (chunk, chunk), 1)
    causal = row >= col
    tri = jnp.where(causal, 1.0, 0.0).astype(BF16)
    row_ok = lax.broadcasted_iota(I32, (chunk, HGRN_DK), 0) < valid_rows

    def body(ci, carry):
        r0 = pl.multiple_of(ci * chunk, chunk)
        for h in range(HGRN_HEADS):
            sl = slice(h * HGRN_DK, (h + 1) * HGRN_DK)
            rows = pl.ds(r0, chunk)
            lb = lb_ref[:, sl]
            q = _silu(zq_ref[0, rows, sl])
            f = lb + (1.0 - lb) * jax.nn.sigmoid(zf_ref[0, rows, sl])
            if valid_rows < chunk:
                f = jnp.where(row_ok, f, 1.0)
            k = 1.0 - f
            v = zi_ref[0, rows, sl]
            g_hi, g_lo = _split_bf16(jnp.log(f))
            b = jnp.dot(tri, g_hi, preferred_element_type=F32) + jnp.dot(tri, g_lo, preferred_element_type=F32)
            b_last = b[chunk - 1:chunk, :]
            q_dec = (q * jnp.exp(b)).astype(BF16)
            k_dec = (k * jnp.exp(-b)).astype(BF16)
            k_rem = (k * jnp.exp(b_last - b)).astype(BF16)
            vb = v.astype(BF16)
            scores = jnp.where(causal, _dot_nt(q_dec, k_dec), 0.0)
            st = st_ref[h]
            o = _dot_nt(q_dec, st.astype(BF16)) + jnp.dot(scores.astype(BF16), vb, preferred_element_type=F32)
            st_ref[h] = st * jnp.exp(b_last) + _dot_tn(vb, k_rem)
            o_ref[0, rows, sl] = _rms(o, on_ref[...]) * _silu(zg_ref[0, rows, sl])
        return carry

    lax.fori_loop(0, n_chunks, body, 0)

    @pl.when(step == pl.num_programs(1) - 1)
    def _():
        for h in range(HGRN_HEADS):
            so_ref[0, h] = st_ref[h].T


def _hgrn(z3d, lb, out_norm, s0, chunk, t_step, valid_rows):
    bsz, t, _ = z3d.shape
    col = lambda j: pl.BlockSpec((1, t_step, TOK_WIDTH), lambda b, i, j=j: (b, i, j))
    state_spec = pl.BlockSpec((1, HGRN_HEADS, HGRN_DK, HGRN_DK), lambda b, i: (b, 0, 0, 0))
    kern = functools.partial(_hgrn_kernel, chunk=chunk, n_chunks=t_step // chunk, valid_rows=valid_rows)
    return pl.pallas_call(
        kern,
        out_shape=(jax.ShapeDtypeStruct((bsz, t, TOK_WIDTH), F32),
                   jax.ShapeDtypeStruct((bsz, HGRN_HEADS, HGRN_DK, HGRN_DK), F32)),
        grid=(bsz, t // t_step),
        in_specs=[col(0), col(1), col(2), col(3),
                  pl.BlockSpec((1, TOK_WIDTH), lambda b, i: (0, 0)),
                  pl.BlockSpec((1, HGRN_DK), lambda b, i: (0, 0)),
                  state_spec],
        out_specs=(pl.BlockSpec((1, t_step, TOK_WIDTH), lambda b, i: (b, i, 0)), state_spec),
        scratch_shapes=[pltpu.VMEM((HGRN_HEADS, HGRN_DK, HGRN_DK), F32)],
        compiler_params=_params("parallel", "arbitrary"),
        name="hgrn2",
    )(z3d, z3d, z3d, z3d, lb.reshape(1, TOK_WIDTH), out_norm.reshape(1, HGRN_DK), s0)


def _mem_kv_kernel(m_ref, w_ref, g_ref, k_ref, v_ref):
    kv = jnp.dot(m_ref[...].astype(BF16), w_ref[...], preferred_element_type=F32)
    k = kv[:, :MEM_WIDTH]
    k_ref[...] = k * lax.rsqrt(_group_mean_sq(k, MEM_HD) + EPS) * g_ref[...]
    v_ref[...] = kv[:, MEM_WIDTH:]


def _mem_kv(mem2d, w_bf16, k_gain):
    n = mem2d.shape[0]
    out = jax.ShapeDtypeStruct((n, MEM_WIDTH), F32)
    return pl.pallas_call(
        _mem_kv_kernel,
        out_shape=(out, out),
        grid=(n // N_MEM,),
        in_specs=[pl.BlockSpec((N_MEM, D_MODEL), lambda i: (i, 0)),
                  pl.BlockSpec((D_MODEL, 2 * MEM_WIDTH), lambda i: (0, 0)),
                  pl.BlockSpec((1, MEM_WIDTH), lambda i: (0, 0))],
        out_specs=(pl.BlockSpec((N_MEM, MEM_WIDTH), lambda i: (i, 0)),
                   pl.BlockSpec((N_MEM, MEM_WIDTH), lambda i: (i, 0))),
        compiler_params=_params("parallel"),
        name="mem_kv",
    )(mem2d, w_bf16, jnp.tile(k_gain, MEM_HEADS).reshape(1, MEM_WIDTH))


def _mem_attn_kernel(zq_ref, g_ref, mk_ref, mv_ref, o_ref):
    q = zq_ref[0]
    qn = q * lax.rsqrt(_group_mean_sq(q, MEM_HD) + EPS) * g_ref[...]
    mk = mk_ref[0].astype(BF16)
    mv = mv_ref[0].astype(BF16)
    head_of_lane = lax.broadcasted_iota(I32, qn.shape, 1) // MEM_HD
    out = jnp.zeros(qn.shape, F32)
    for h in range(MEM_HEADS):
        mine = head_of_lane == h
        s = _dot_nt(jnp.where(mine, qn, 0.0).astype(BF16), mk) * (MEM_HD ** -0.5)
        p = jnp.exp(s - jnp.max(s, axis=-1, keepdims=True))
        oh = jnp.dot(p.astype(BF16), mv, preferred_element_type=F32) / jnp.sum(p, axis=-1, keepdims=True)
        out = jnp.where(mine, oh, out)
    o_ref[0] = out


def _mem_attn(z3d, q_col_block, q_gain, mk, mv, tm):
    bsz, t, _ = z3d.shape
    return pl.pallas_call(
        _mem_attn_kernel,
        out_shape=jax.ShapeDtypeStruct((bsz, t, MEM_WIDTH), F32),
        grid=(bsz, t // tm),
        in_specs=[pl.BlockSpec((1, tm, MEM_WIDTH), lambda b, i: (b, i, q_col_block)),
                  pl.BlockSpec((1, MEM_WIDTH), lambda b, i: (0, 0)),
                  pl.BlockSpec((1, N_MEM, MEM_WIDTH), lambda b, i: (b, 0, 0)),
                  pl.BlockSpec((1, N_MEM, MEM_WIDTH), lambda b, i: (b, 0, 0))],
        out_specs=pl.BlockSpec((1, tm, MEM_WIDTH), lambda b, i: (b, i, 0)),
        compiler_params=_params("parallel", "parallel"),
        name="mem_attn",
    )(z3d, jnp.tile(q_gain, MEM_HEADS).reshape(1, MEM_WIDTH), mk, mv)


def _out_proj_kernel(x_ref, t_ref, c_ref, wt_ref, wc_ref, o_ref):
    o_ref[...] = (x_ref[...]
                  + jnp.dot(t_ref[...].astype(BF16), wt_ref[...], preferred_element_type=F32)
                  + jnp.dot(c_ref[...].astype(BF16), wc_ref[...], preferred_element_type=F32))


def _out_proj(x2d, t2d, c2d, w_bf16, tm):
    n = x2d.shape[0]
    row = lambda w: pl.BlockSpec((tm, w), lambda i: (i, 0))
    return pl.pallas_call(
        _out_proj_kernel,
        out_shape=jax.ShapeDtypeStruct((n, D_MODEL), F32),
        grid=(n // tm,),
        in_specs=[row(D_MODEL), row(TOK_WIDTH), row(MEM_WIDTH),
                  pl.BlockSpec((TOK_WIDTH, D_MODEL), lambda i: (0, 0)),
                  pl.BlockSpec((MEM_WIDTH, D_MODEL), lambda i: (0, 0))],
        out_specs=row(D_MODEL),
        compiler_params=_params("parallel"),
        name="out_proj",
    )(x2d, t2d, c2d, w_bf16[:TOK_WIDTH], w_bf16[TOK_WIDTH:])


def _ffn_kernel(x_ref, g_ref, wg_ref, wu_ref, cw_ref, cb_ref, wd_ref, p1_ref, p2_ref, y_ref, u_ref, carry_ref,
                *, seq_rows, carry):
    tm = x_ref.shape[0]
    x = x_ref[...]
    h = _rms(x, g_ref[...]).astype(BF16)
    t_in_seq = lax.broadcasted_iota(I32, (tm, FF_CHUNK), 0) % seq_rows
    first_block = pl.program_id(1) == 0 if carry else None
    acc = x
    for c in range(D_FF // FF_CHUNK):
        cs = slice(c * FF_CHUNK, (c + 1) * FF_CHUNK)
        u = jnp.dot(h, wg_ref[:, cs], preferred_element_type=F32)
        up = jnp.dot(h, wu_ref[:, cs], preferred_element_type=F32)
        u1 = pltpu.roll(u, 1, 0)
        u2 = pltpu.roll(u, 2, 0)
        if carry:
            prev = carry_ref[:, cs]
            start1 = jnp.where(first_block, p1_ref[0:1, cs], prev[7:8, :])
            start2a = jnp.where(first_block, p2_ref[0:1, cs], prev[6:7, :])
            start2b = jnp.where(first_block, p2_ref[1:2, cs], prev[7:8, :])
            u1 = jnp.where(t_in_seq == 0, start1, u1)
            u2 = jnp.where(t_in_seq == 0, start2a, jnp.where(t_in_seq == 1, start2b, u2))
            carry_ref[:, cs] = u[tm - SUBLANES:, :]
        else:
            u1 = jnp.where(t_in_seq == 0, p1_ref[:, cs], u1)
            u2 = jnp.where(t_in_seq < 2, p2_ref[:, cs], u2)
        conv = cb_ref[:, cs] + u2 * cw_ref[0:1, cs] + u1 * cw_ref[1:2, cs] + u * cw_ref[2:3, cs]
        a = (_silu(conv) * up).astype(BF16)
        acc = acc + jnp.dot(a, wd_ref[cs, :], preferred_element_type=F32)
        if carry:
            u_ref[0, :, cs] = u[tm - SUBLANES:, :]
        else:
            u_ref[:, cs] = u
    y_ref[...] = acc


def _ffn_weights_specs(idx):
    once = pl.Buffered(1)
    return [pl.BlockSpec((1, D_MODEL), idx),
            pl.BlockSpec((D_MODEL, D_FF), idx, pipeline_mode=once),
            pl.BlockSpec((D_MODEL, D_FF), idx, pipeline_mode=once),
            pl.BlockSpec((CONV_W, D_FF), idx),
            pl.BlockSpec((1, D_FF), idx),
            pl.BlockSpec((D_FF, D_MODEL), idx, pipeline_mode=once)]


def _ffn_long(x3d, gain, wg, wu, cw, cb, wd, state, tm):
    bsz, t, _ = x3d.shape
    idx = lambda b, i: (0, 0)
    kern = functools.partial(_ffn_kernel, seq_rows=t, carry=True)
    x2d = x3d.reshape(bsz * t, D_MODEL)
    nb = t // tm
    p1 = state[:, 1:2, :]
    y, u_last = pl.pallas_call(
        kern,
        out_shape=(jax.ShapeDtypeStruct((bsz * t, D_MODEL), F32),
                   jax.ShapeDtypeStruct((bsz, SUBLANES, D_FF), F32)),
        grid=(bsz, nb),
        in_specs=[pl.BlockSpec((tm, D_MODEL), lambda b, i: (b * nb + i, 0))] + _ffn_weights_specs(idx)
                 + [pl.BlockSpec((None, 1, D_FF), lambda b, i: (b, 0, 0)),
                    pl.BlockSpec((None, 2, D_FF), lambda b, i: (b, 0, 0))],
        out_specs=(pl.BlockSpec((tm, D_MODEL), lambda b, i: (b * nb + i, 0)),
                   pl.BlockSpec((1, SUBLANES, D_FF), lambda b, i: (b, 0, 0))),
        scratch_shapes=[pltpu.VMEM((SUBLANES, D_FF), F32)],
        compiler_params=_params("parallel", "arbitrary"),
        name="conv_ffn_long",
    )(x2d, gain.reshape(1, D_MODEL), wg, wu, cw, cb.reshape(1, D_FF), wd, p1, state)
    return y.reshape(bsz, t, D_MODEL), u_last


def _ffn_short(x2d, gain, wg, wu, cw, cb, wd, p1, p2, seq_rows):
    n = x2d.shape[0]
    idx = lambda i: (0, 0)
    kern = functools.partial(_ffn_kernel, seq_rows=seq_rows, carry=False)
    row = lambda w: pl.BlockSpec((n, w), lambda i: (0, 0))
    return pl.pallas_call(
        kern,
        out_shape=(jax.ShapeDtypeStruct((n, D_MODEL), F32), jax.ShapeDtypeStruct((n, D_FF), F32)),
        grid=(1,),
        in_specs=[row(D_MODEL)] + _ffn_weights_specs(idx) + [row(D_FF), row(D_FF)],
        out_specs=(row(D_MODEL), row(D_FF)),
        scratch_shapes=[pltpu.VMEM((SUBLANES, D_FF), F32)],
        compiler_params=_params("arbitrary"),
        name="conv_ffn_short",
    )(x2d, gain.reshape(1, D_MODEL), wg, wu, cw, cb.reshape(1, D_FF), wd, p1, p2)


def _rope128(x, cos, sin_signed):
    return x * cos + pltpu.roll(x, HEAD_DIM // 2, 1) * sin_signed


def _rope64_pairs(x, cos, sin_lo, sin_hi):
    return (x * cos + pltpu.roll(x, LANES - IDX_DIM // 2, 1) * sin_lo
            + pltpu.roll(x, IDX_DIM // 2, 1) * sin_hi)


def _dsa_prep_kernel(z_ref, c128_ref, s128_ref, c64_ref, s64lo_ref, s64hi_ref, qg_ref, kg_ref, ikg_ref,
                     q_ref, k_ref, kb_ref, v_ref, vt_ref, iq_ref, ik_ref, ikb_ref, iwt_ref):
    c128, s128 = c128_ref[...], s128_ref[...]
    c64, s64lo, s64hi = c64_ref[...], s64lo_ref[...], s64hi_ref[...]
    for h in range(N_Q_HEADS):
        sl = slice(DSA_Q0 + h * HEAD_DIM, DSA_Q0 + (h + 1) * HEAD_DIM)
        q = _rope128(_rms(z_ref[0, :, sl], qg_ref[...]), c128, s128)
        q_ref[0, :, h * HEAD_DIM:(h + 1) * HEAD_DIM] = (q * QK_SCALE_LOG2E).astype(BF16)
    for h in range(N_KV_HEADS):
        sl = slice(DSA_K0 + h * HEAD_DIM, DSA_K0 + (h + 1) * HEAD_DIM)
        k = _rope128(_rms(z_ref[0, :, sl], kg_ref[...]), c128, s128)
        k_ref[0, :, h * HEAD_DIM:(h + 1) * HEAD_DIM] = k
        kb_ref[0, :, h * HEAD_DIM:(h + 1) * HEAD_DIM] = k.astype(BF16)
    v = z_ref[0, :, DSA_V0:DSA_V0 + N_KV_HEADS * HEAD_DIM]
    v_ref[0] = v
    vt_ref[0, 0] = v.T.astype(BF16)
    for p in range(IDX_HEADS // 2):
        sl = slice(DSA_IQ0 + p * LANES, DSA_IQ0 + (p + 1) * LANES)
        iq_ref[0, :, p * LANES:(p + 1) * LANES] = _rope64_pairs(z_ref[0, :, sl], c64, s64lo, s64hi)
    ik = _rope64_pairs(_rms(z_ref[0, :, DSA_IK0:DSA_IK0 + LANES], ikg_ref[...]), c64, s64lo, s64hi)
    ik_ref[0] = ik
    ik_hi, ik_lo = _split_bf16(ik)
    hi_lo = jnp.where(lax.broadcasted_iota(I32, ik.shape, 1) < IDX_DIM, ik_hi, ik_lo)
    ikb_ref[0, :, :LANES] = hi_lo
    ikb_ref[0, :, LANES:] = hi_lo
    iw = z_ref[0, :, DSA_IW0:DSA_IW0 + LANES] * (IDX_HEADS ** -0.5 * IDX_DIM ** -0.5)
    iwt_ref[0] = iw.T[:IDX_HEADS, :]


def _dsa_prep(z3d, tabs, q_gain, k_gain, ik_gain, tm):
    bsz, t, _ = z3d.shape
    nb = t // tm
    tab = pl.BlockSpec((tm, LANES), lambda b, i: (i, 0))
    gain = pl.BlockSpec((1, LANES), lambda b, i: (0, 0))
    rows = lambda w: pl.BlockSpec((1, tm, w), lambda b, i: (b, i, 0))
    kvw = N_KV_HEADS * HEAD_DIM
    sds = jax.ShapeDtypeStruct
    return pl.pallas_call(
        _dsa_prep_kernel,
        out_shape=(sds((bsz, t, TOK_WIDTH), BF16),
                   sds((bsz, t, kvw), F32), sds((bsz, t, kvw), BF16),
                   sds((bsz, t, kvw), F32), sds((bsz, nb, kvw, tm), BF16),
                   sds((bsz, t, IDX_HEADS * IDX_DIM), F32),
                   sds((bsz, t, LANES), F32),
                   sds((bsz, t, 2 * LANES), BF16),
                   sds((bsz, IDX_HEADS, t), F32)),
        grid=(bsz, nb),
        in_specs=[rows(DSA_COLS_PAD), tab, tab, tab, tab, tab, gain, gain, gain],
        out_specs=(rows(TOK_WIDTH), rows(kvw), rows(kvw), rows(kvw),
                   pl.BlockSpec((1, 1, kvw, tm), lambda b, i: (b, i, 0, 0)),
                   rows(IDX_HEADS * IDX_DIM), rows(LANES), rows(2 * LANES),
                   pl.BlockSpec((1, IDX_HEADS, tm), lambda b, i: (b, 0, i))),
        compiler_params=_params("parallel", "parallel"),
        name="dsa_prep",
    )(z3d, *tabs, q_gain.reshape(1, LANES), k_gain.reshape(1, LANES), jnp.tile(ik_gain, 2).reshape(1, LANES))


def _rope_tables(pos):
    pos = pos.astype(F32)[:, None]

    def cs(half):
        inv = ROPE_THETA ** (-jnp.arange(half, dtype=F32) / half)
        ang = pos * inv[None, :]
        return jnp.cos(ang), jnp.sin(ang)

    c, s = cs(HEAD_DIM // 2)
    c128 = jnp.concatenate([c, c], axis=1)
    s128 = jnp.concatenate([-s, s], axis=1)
    c, s = cs(IDX_DIM // 2)
    z = jnp.zeros_like(s)
    c64 = jnp.concatenate([c, c, c, c], axis=1)
    s64lo = jnp.concatenate([-s, z, -s, z], axis=1)
    s64hi = jnp.concatenate([z, s, z, s], axis=1)
    return c128, s128, c64, s64lo, s64hi


def _ordered_bits_to_float(u):
    key = u ^ INT_MIN
    bits = key ^ ((key >> 31) & 0x7FFFFFFF)
    return lax.bitcast_convert_type(bits, F32)


def _topk_threshold(count_ge, count_gt, count_eq_before, any_true, shape, idx_bits):
    def value_step(i, prefix):
        cand = prefix | jnp.left_shift(jnp.int32(1), 31 - i)
        below_neg_inf = jnp.logical_and(cand >= 0, cand < 0x00800000)
        ok = jnp.logical_or(count_ge(_ordered_bits_to_float(cand)) >= TOPK, below_neg_inf)
        return jnp.where(ok, cand, prefix)

    thr = _ordered_bits_to_float(lax.fori_loop(0, 32, value_step, jnp.zeros(shape, I32)))
    n_gt = count_gt(thr)
    need = TOPK - n_gt
    n_eq = count_ge(thr) - n_gt
    finite_thr = thr > -jnp.inf
    excess = jnp.logical_and(n_eq > need, finite_thr)

    def index_search():
        def index_step(i, cut):
            cand = cut | jnp.left_shift(jnp.int32(1), idx_bits - 1 - i)
            return jnp.where(count_eq_before(thr, cand) < need, cand, cut)
        return lax.fori_loop(0, idx_bits, index_step, jnp.zeros(shape, I32))

    everything = jnp.full(shape, 2 ** idx_bits - 1, I32)
    cut = lax.cond(any_true(excess), index_search, lambda: everything)
    return thr, jnp.where(finite_thr, cut, -1)


def _dsa_prompt_kernel(q_ref, iq_ref, iwt_ref, ik_ref, k_ref, vt_ref, o_ref,
                       score_ref, iqm_ref, qm_ref, m_ref, l_ref, acc_ref):
    qi = pl.program_id(1)
    q0 = qi * Q_TILE
    n_kb = (q0 + Q_TILE + KEY_BLOCK - 1) // KEY_BLOCK
    lane = lax.broadcasted_iota(I32, (Q_TILE, LANES), 1)
    first_half = lane < IDX_DIM
    for p in range(IDX_HEADS // 2):
        pair = iq_ref[0, :, p * LANES:(p + 1) * LANES]
        swapped = pltpu.roll(pair, IDX_DIM, 1)
        for half in range(2):
            dup = jnp.where(first_half, pair, swapped) if half == 0 else jnp.where(first_half, swapped, pair)
            hi, lo = _split_bf16(dup)
            rows = slice(half * Q_TILE, (half + 1) * Q_TILE)
            iqm_ref[p, rows, :LANES] = hi
            iqm_ref[p, rows, LANES:] = lo
    for h in range(N_Q_HEADS):
        g = h // Q_PER_KV
        qh = q_ref[0, :, h * HEAD_DIM:(h + 1) * HEAD_DIM]
        rows = slice(h * Q_TILE, (h + 1) * Q_TILE)
        qm_ref[rows, g * HEAD_DIM:(g + 1) * HEAD_DIM] = qh
        qm_ref[rows, (1 - g) * HEAD_DIM:(2 - g) * HEAD_DIM] = jnp.zeros_like(qh)
    q_pos = q0 + lax.broadcasted_iota(I32, (1, Q_TILE), 1)
    key_row = lax.broadcasted_iota(I32, (KEY_BLOCK, 1), 0)

    def score_block(kb, carry):
        s0 = pl.multiple_of(kb * KEY_BLOCK, KEY_BLOCK)
        ik = ik_ref[0, pl.ds(s0, KEY_BLOCK), :]
        acc = jnp.zeros((KEY_BLOCK, Q_TILE), F32)
        for p in range(IDX_HEADS // 2):
            d = _dot_nt(ik, iqm_ref[p])
            for half in range(2):
                h = 2 * p + half
                acc = acc + iwt_ref[0, h:h + 1, :] * jnp.maximum(d[:, half * Q_TILE:(half + 1) * Q_TILE], 0.0)
        score_ref[pl.ds(s0, KEY_BLOCK), :] = jnp.where(s0 + key_row <= q_pos, acc, -jnp.inf)
        return carry

    lax.fori_loop(0, n_kb, score_block, 0)

    def count(pred):
        def body(kb, cnt):
            s0 = pl.multiple_of(kb * KEY_BLOCK, KEY_BLOCK)
            hit = pred(score_ref[pl.ds(s0, KEY_BLOCK), :], s0 + key_row).astype(I32)
            return cnt + jnp.sum(hit.reshape(KEY_BLOCK // SUBLANES, SUBLANES, Q_TILE), axis=0)
        part = lax.fori_loop(0, n_kb, body, jnp.zeros((SUBLANES, Q_TILE), I32))
        return jnp.sum(part, axis=0, keepdims=True)

    thr, cut = _topk_threshold(
        count_ge=lambda c: count(lambda s, idx: s >= c),
        count_gt=lambda c: count(lambda s, idx: s > c),
        count_eq_before=lambda t, j: count(lambda s, idx: jnp.logical_and(s == t, idx < j)),
        any_true=lambda m: jnp.max(m.astype(I32)) > 0,
        shape=(1, Q_TILE), idx_bits=13)

    m_ref[...] = jnp.full(m_ref.shape, NEG_BIG, F32)
    l_ref[...] = jnp.zeros(l_ref.shape, F32)
    acc_ref[...] = jnp.zeros(acc_ref.shape, F32)

    def attend_block(kb, carry):
        s0 = pl.multiple_of(kb * KEY_BLOCK, KEY_BLOCK)
        sc = score_ref[pl.ds(s0, KEY_BLOCK), :]
        sel = jnp.logical_or(sc > thr, jnp.logical_and(sc == thr, s0 + key_row <= cut))
        k_blk = k_ref[0, pl.ds(s0, KEY_BLOCK), :]
        alphas, ps = [], []
        for j in range(N_Q_HEADS // 2):
            s2 = _dot_nt(k_blk, qm_ref[2 * j * Q_TILE:2 * (j + 1) * Q_TILE, :])
            for half in range(2):
                h = 2 * j + half
                s = jnp.where(sel, s2[:, half * Q_TILE:(half + 1) * Q_TILE], NEG_BIG)
                m_old = m_ref[h:h + 1, :]
                m_new = jnp.maximum(m_old, jnp.max(s, axis=0, keepdims=True))
                alpha = jnp.exp2(m_old - m_new)
                p = jnp.exp2(s - m_new)
                l_ref[h:h + 1, :] = alpha * l_ref[h:h + 1, :] + jnp.sum(p, axis=0, keepdims=True)
                m_ref[h:h + 1, :] = m_new
                alphas.append(alpha)
                ps.append(p.astype(BF16))
        for g in range(N_KV_HEADS):
            vtg = vt_ref[0, kb, g * HEAD_DIM:(g + 1) * HEAD_DIM, :]
            pv = jnp.dot(vtg, jnp.concatenate(ps[g * Q_PER_KV:(g + 1) * Q_PER_KV], axis=1),
                         preferred_element_type=F32)
            for r in range(Q_PER_KV):
                h = g * Q_PER_KV + r
                acc_ref[h] = alphas[h] * acc_ref[h] + pv[:, r * Q_TILE:(r + 1) * Q_TILE]
        return carry

    lax.fori_loop(0, n_kb, attend_block, 0)
    for h in range(N_Q_HEADS):
        o_ref[0, :, h * HEAD_DIM:(h + 1) * HEAD_DIM] = (acc_ref[h] / l_ref[h:h + 1, :]).T


def _dsa_prompt(q, iq, iwt, ikb, kb, vt):
    bsz, t, _ = q.shape
    kvw = N_KV_HEADS * HEAD_DIM
    full = lambda w: pl.BlockSpec((1, t, w), lambda b, i: (b, 0, 0))
    return pl.pallas_call(
        _dsa_prompt_kernel,
        out_shape=jax.ShapeDtypeStruct((bsz, t, TOK_WIDTH), F32),
        grid=(bsz, t // Q_TILE),
        in_specs=[pl.BlockSpec((1, Q_TILE, TOK_WIDTH), lambda b, i: (b, i, 0)),
                  pl.BlockSpec((1, Q_TILE, IDX_HEADS * IDX_DIM), lambda b, i: (b, i, 0)),
                  pl.BlockSpec((1, IDX_HEADS, Q_TILE), lambda b, i: (b, 0, i)),
                  full(2 * LANES), full(kvw),
                  pl.BlockSpec((1, t // KEY_BLOCK, kvw, KEY_BLOCK), lambda b, i: (b, 0, 0, 0))],
        out_specs=pl.BlockSpec((1, Q_TILE, TOK_WIDTH), lambda b, i: (b, i, 0)),
        scratch_shapes=[pltpu.VMEM((t, Q_TILE), F32),
                        pltpu.VMEM((IDX_HEADS // 2, 2 * Q_TILE, 2 * LANES), BF16),
                        pltpu.VMEM((N_Q_HEADS * Q_TILE, N_KV_HEADS * HEAD_DIM), BF16),
                        pltpu.VMEM((SUBLANES, Q_TILE), F32),
                        pltpu.VMEM((SUBLANES, Q_TILE), F32),
                        pltpu.VMEM((N_Q_HEADS, HEAD_DIM, Q_TILE), F32)],
        compiler_params=_params("parallel", "arbitrary"),
        name="dsa_prompt",
    )(q, iq, iwt, ikb, kb, vt)


SAMPLE_ROWS = 16
SAMPLE_T = 4
N_PAGES = PAST_LEN // PAGE_SIZE
N_KEY_PAGES = N_PAGES + 1


def _sample_score_kernel(pt_ref, iq4_ref, iw_ref, ikn_ref, *rest):
    page_refs = rest[:PAGES_PER_STEP]
    score_out, thr_out, cut_out, score_ref = rest[PAGES_PER_STEP:]
    step = pl.program_id(1)
    iq_hi, iq_lo = _split_bf16(iq4_ref[0])
    iq_parts = jnp.where(lax.broadcasted_iota(I32, iq_hi.shape, 1) < 2 * IDX_DIM, iq_hi, iq_lo)
    w = iw_ref[0]

    def page_scores(ik_t, page, allowed):
        k_hi, k_lo = _split_bf16(ik_t)
        d = jnp.dot(iq_parts, jnp.concatenate([k_hi, k_lo, k_hi, k_lo], axis=0), preferred_element_type=F32)
        sc = jnp.sum((jnp.maximum(d, 0.0) * w).reshape(SAMPLE_T, IDX_HEADS, PAGE_SIZE), axis=1)
        if allowed is not None:
            sc = jnp.where(allowed, sc, -jnp.inf)
        for r in range(SAMPLE_ROWS // SAMPLE_T):
            score_ref[page, r * SAMPLE_T:(r + 1) * SAMPLE_T, :] = sc

    for i in range(PAGES_PER_STEP):
        page_scores(page_refs[i][0], step * PAGES_PER_STEP + i, None)

    @pl.when(step == pl.num_programs(1) - 1)
    def _():
        new_idx = lax.broadcasted_iota(I32, (SAMPLE_T, PAGE_SIZE), 1)
        page_scores(ikn_ref[0], N_PAGES, new_idx <= lax.broadcasted_iota(I32, (SAMPLE_T, PAGE_SIZE), 0))
        key_idx = (lax.broadcasted_iota(I32, (N_KEY_PAGES, 1, PAGE_SIZE), 0) * PAGE_SIZE
                   + lax.broadcasted_iota(I32, (N_KEY_PAGES, 1, PAGE_SIZE), 2))

        def count(pred):
            hit = pred(score_ref[...], key_idx).astype(I32)
            return jnp.sum(jnp.sum(hit, axis=0), axis=-1, keepdims=True)

        thr, cut = _topk_threshold(
            count_ge=lambda c: count(lambda s, idx: s >= c[None]),
            count_gt=lambda c: count(lambda s, idx: s > c[None]),
            count_eq_before=lambda t, j: count(lambda s, idx: jnp.logical_and(s == t[None], idx < j[None])),
            any_true=lambda m: jnp.max(m.astype(I32)) > 0,
            shape=(SAMPLE_ROWS, 1), idx_bits=14)
        score_out[0] = score_ref[...]
        thr_out[0] = jnp.broadcast_to(thr, (SAMPLE_ROWS, LANES))
        cut_out[0] = jnp.broadcast_to(cut, (SAMPLE_ROWS, LANES))


def _sample_scores(page_table, iq4, iw_col, ik_new_t, pool_ik_t):
    bsz = iq4.shape[0]
    page_spec = lambda i: pl.BlockSpec((1, IDX_DIM, PAGE_SIZE),
                                       lambda b, s, pt, i=i: (pt[b, s * PAGES_PER_STEP + i], 0, 0))
    per_b = lambda shape: pl.BlockSpec((1,) + shape, lambda b, s, pt: (b,) + (0,) * len(shape))
    sds = jax.ShapeDtypeStruct
    grid_spec = pltpu.PrefetchScalarGridSpec(
        num_scalar_prefetch=1,
        grid=(bsz, N_PAGES // PAGES_PER_STEP),
        in_specs=[per_b((SAMPLE_T * IDX_HEADS, 4 * IDX_DIM)), per_b((SAMPLE_T * IDX_HEADS, 1)),
                  per_b((IDX_DIM, PAGE_SIZE))] + [page_spec(i) for i in range(PAGES_PER_STEP)],
        out_specs=(per_b((N_KEY_PAGES, SAMPLE_ROWS, PAGE_SIZE)), per_b((SAMPLE_ROWS, LANES)),
                   per_b((SAMPLE_ROWS, LANES))),
        scratch_shapes=[pltpu.VMEM((N_KEY_PAGES, SAMPLE_ROWS, PAGE_SIZE), F32)])
    return pl.pallas_call(
        _sample_score_kernel,
        out_shape=(sds((bsz, N_KEY_PAGES, SAMPLE_ROWS, PAGE_SIZE), F32),
                   sds((bsz, SAMPLE_ROWS, LANES), F32), sds((bsz, SAMPLE_ROWS, LANES), I32)),
        grid_spec=grid_spec,
        compiler_params=_params("parallel", "arbitrary"),
        name="dsa_sample_scores",
    )(page_table, iq4, iw_col, ik_new_t, *([pool_ik_t] * PAGES_PER_STEP))


def _sample_attn_kernel(pt_ref, q_ref, sc_ref, thr_ref, cut_ref, kn_ref, vn_ref, *rest):
    k_pages = rest[:PAGES_PER_STEP]
    v_pages = rest[PAGES_PER_STEP:2 * PAGES_PER_STEP]
    o_ref, m_ref, l_ref, acc_ref = rest[2 * PAGES_PER_STEP:]
    step = pl.program_id(1)

    @pl.when(step == 0)
    def _():
        m_ref[...] = jnp.full(m_ref.shape, NEG_BIG, F32)
        l_ref[...] = jnp.zeros(l_ref.shape, F32)
        acc_ref[...] = jnp.zeros(acc_ref.shape, F32)

    thr = thr_ref[0]
    cut = cut_ref[0]
    lane = lax.broadcasted_iota(I32, (SAMPLE_ROWS, PAGE_SIZE), 1)

    def attend(pages):
        sels = []
        for page, _, _ in pages:
            sc = sc_ref[0, page]
            sels.append(jnp.logical_or(sc > thr, jnp.logical_and(sc == thr, page * PAGE_SIZE + lane <= cut)))
        for g in range(N_KV_HEADS):
            rows_g = pl.ds(g, PAGE_SIZE, stride=N_KV_HEADS)
            q = q_ref[0, g]
            ss = [jnp.where(sel, _dot_nt(q, k[0, rows_g, :].astype(BF16)), NEG_BIG)
                  for sel, (_, k, _) in zip(sels, pages)]
            blk_max = functools.reduce(jnp.maximum, ss)
            m_old = m_ref[g]
            m_new = jnp.maximum(m_old, jnp.max(blk_max, axis=-1, keepdims=True))
            alpha = jnp.exp2(m_old - m_new)
            ps = [jnp.exp2(s - m_new) for s in ss]
            l_ref[g] = alpha * l_ref[g] + jnp.sum(functools.reduce(jnp.add, ps), axis=-1, keepdims=True)
            pv = [jnp.dot(p.astype(BF16), v[0, rows_g, :].astype(BF16), preferred_element_type=F32)
                  for p, (_, _, v) in zip(ps, pages)]
            acc_ref[g] = alpha * acc_ref[g] + functools.reduce(jnp.add, pv)
            m_ref[g] = m_new

    attend([(step * PAGES_PER_STEP + i, k_pages[i], v_pages[i]) for i in range(PAGES_PER_STEP)])

    @pl.when(step == pl.num_programs(1) - 1)
    def _():
        attend([(N_PAGES, kn_ref, vn_ref)])
        for g in range(N_KV_HEADS):
            o_ref[0, g] = acc_ref[g] / l_ref[g]


def _sample_attn(page_table, q16, scores, thr, cut, k_new_pad, v_new_pad, pool_k, pool_v):
    bsz = q16.shape[0]
    page_rows = PAGE_SIZE * N_KV_HEADS
    page_spec = lambda i: pl.BlockSpec((1, page_rows, HEAD_DIM),
                                       lambda b, s, pt, i=i: (pt[b, s * PAGES_PER_STEP + i], 0, 0))
    per_b = lambda shape: pl.BlockSpec((1,) + shape, lambda b, s, pt: (b,) + (0,) * len(shape))
    pages = [page_spec(i) for i in range(PAGES_PER_STEP)]
    grid_spec = pltpu.PrefetchScalarGridSpec(
        num_scalar_prefetch=1,
        grid=(bsz, N_PAGES // PAGES_PER_STEP),
        in_specs=[per_b((N_KV_HEADS, SAMPLE_ROWS, HEAD_DIM)), per_b((N_KEY_PAGES, SAMPLE_ROWS, PAGE_SIZE)),
                  per_b((SAMPLE_ROWS, LANES)), per_b((SAMPLE_ROWS, LANES)),
                  per_b((page_rows, HEAD_DIM)), per_b((page_rows, HEAD_DIM))] + pages + pages,
        out_specs=per_b((N_KV_HEADS, SAMPLE_ROWS, HEAD_DIM)),
        scratch_shapes=[pltpu.VMEM((N_KV_HEADS, SAMPLE_ROWS, 1), F32),
                        pltpu.VMEM((N_KV_HEADS, SAMPLE_ROWS, 1), F32),
                        pltpu.VMEM((N_KV_HEADS, SAMPLE_ROWS, HEAD_DIM), F32)])
    return pl.pallas_call(
        _sample_attn_kernel,
        out_shape=jax.ShapeDtypeStruct((bsz, N_KV_HEADS, SAMPLE_ROWS, HEAD_DIM), F32),
        grid_spec=grid_spec,
        compiler_params=_params("parallel", "arbitrary"),
        name="dsa_sample_attn",
    )(page_table, q16, scores, thr, cut, k_new_pad, v_new_pad,
      *([pool_k] * PAGES_PER_STEP), *([pool_v] * PAGES_PER_STEP))


def _pad_cols(w, width):
    return jnp.pad(w, ((0, 0), (0, width - w.shape[1])))


def _dsa_weight(w):
    sizes = (TOK_WIDTH, N_KV_HEADS * HEAD_DIM, N_KV_HEADS * HEAD_DIM, IDX_HEADS * IDX_DIM, IDX_DIM, IDX_HEADS, MEM_WIDTH)
    offs = [0]
    for s in sizes:
        offs.append(offs[-1] + s)
    q, k, v, iq, ik, iw, mq = [w[:, offs[i]:offs[i + 1]] for i in range(len(sizes))]
    return jnp.concatenate([q, k, v, iq, ik, ik, _pad_cols(iw, LANES), mq], axis=1).astype(BF16)


def kernel(x_prompt, x_sample, cache_k, cache_v, cache_idx_k, cache_mem_k, cache_mem_v, state_hgrn, state_ffn_conv, page_table, mem_prompt, norm_mix, norm_ffn, w_in_hgrn, hgrn_lb_logits, hgrn_out_norm, w_in_dsa, dsa_q_norm, dsa_k_norm, idx_k_norm, w_mem_kv, mem_q_norm, mem_k_norm, w_out, w_ffn_gate, w_ffn_up, ffn_conv_w, ffn_conv_b, w_ffn_down):
    bp, seq, _ = x_prompt.shape
    bs, t_s, _ = x_sample.shape
    assert t_s == SAMPLE_T and seq % KEY_BLOCK == 0
    n_p = bp * seq
    n_s = bs * SAMPLE_T_PAD
    kvw = N_KV_HEADS * HEAD_DIM
    lower_bounds = jnp.cumsum(jax.nn.softmax(hgrn_lb_logits.astype(F32), axis=0), axis=0)

    xp = x_prompt
    xs = jnp.pad(x_sample, ((0, 0), (0, SAMPLE_T_PAD - t_s), (0, 0)))
    mem2d = mem_prompt.reshape(bp * N_MEM, D_MODEL)
    tabs_p = _rope_tables(jnp.arange(seq))
    pos_s = PAST_LEN + jnp.arange(SAMPLE_T_PAD)
    tabs_s = tuple(jnp.tile(tb, (bs, 1)) for tb in _rope_tables(pos_s))
    row_t = jnp.arange(n_s) % SAMPLE_T_PAD

    hgrn_p, hgrn_s = [], []
    kp_l, vp_l, ikp_l, ks_l, vs_l, iks_l = [], [], [], [], [], []
    mk_l, mv_l, cvp_l, cvs_l = [], [], [], []
    for layer in range(DEPTH):
        j = layer // 2
        if layer % 2 == 0:
            w_in = w_in_hgrn[j].astype(BF16)
            mem_col = HGRN_MEM0 // MEM_WIDTH
        else:
            w_in = _dsa_weight(w_in_dsa[j])
            mem_col = DSA_MEM0 // MEM_WIDTH
        cols = w_in.shape[1]
        zp = _norm_proj(xp.reshape(n_p, D_MODEL), norm_mix[layer], w_in, 256).reshape(bp, seq, cols)
        zs = _norm_proj(xs.reshape(n_s, D_MODEL), norm_mix[layer], w_in, n_s).reshape(bs, SAMPLE_T_PAD, cols)
        if layer % 2 == 0:
            s0 = jnp.zeros((bp, HGRN_HEADS, HGRN_DK, HGRN_DK), F32)
            tp, sp = _hgrn(zp, lower_bounds[j], hgrn_out_norm[j], s0, HGRN_CHUNK, 512, HGRN_CHUNK)
            ts, ss = _hgrn(zs, lower_bounds[j], hgrn_out_norm[j], state_hgrn[j], SAMPLE_T_PAD, SAMPLE_T_PAD,
                           math.gcd(t_s, HGRN_CHUNK))
            hgrn_p.append(sp.astype(state_hgrn.dtype))
            hgrn_s.append(ss.astype(state_hgrn.dtype))
        else:
            qp, kp, kpb, vp, vtp, iqp, ikp, ikpb, iwtp = _dsa_prep(
                zp, tabs_p, dsa_q_norm[j], dsa_k_norm[j], idx_k_norm[j], KEY_BLOCK)
            tp = _dsa_prompt(qp, iqp, iwtp, ikpb, kpb, vtp)
            kp_l.append(kp.reshape(bp, seq, N_KV_HEADS, HEAD_DIM))
            vp_l.append(vp.reshape(bp, seq, N_KV_HEADS, HEAD_DIM))
            ikp_l.append(ikp[..., :IDX_DIM])

            qs, kn, _, vn, _, iqs, ikn, _, iwts = _dsa_prep(
                zs.reshape(1, n_s, cols), tabs_s, dsa_q_norm[j], dsa_k_norm[j], idx_k_norm[j], n_s)
            sel_t = lambda a: a.reshape(bs, SAMPLE_T_PAD, -1)[:, :t_s]
            kn, vn, ikn = sel_t(kn), sel_t(vn), sel_t(ikn)[..., :IDX_DIM]
            ks_l.append(kn.reshape(bs, t_s, N_KV_HEADS, HEAD_DIM))
            vs_l.append(vn.reshape(bs, t_s, N_KV_HEADS, HEAD_DIM))
            iks_l.append(ikn)
            iq4 = jnp.tile(sel_t(iqs).reshape(bs, t_s * IDX_HEADS, IDX_DIM), (1, 1, 4))
            iw_col = sel_t(iwts[0].T).reshape(bs, t_s * IDX_HEADS, 1)
            ikn_t = jnp.pad(ikn.transpose(0, 2, 1), ((0, 0), (0, 0), (0, PAGE_SIZE - t_s)))
            scores, thr, cut = _sample_scores(page_table, iq4, iw_col, ikn_t, cache_idx_k[j].transpose(0, 2, 1))
            q16 = sel_t(qs).reshape(bs, t_s, N_KV_HEADS, Q_PER_KV, HEAD_DIM).transpose(0, 2, 3, 1, 4)
            q16 = jnp.pad(q16.reshape(bs, N_KV_HEADS, Q_PER_KV * t_s, HEAD_DIM),
                          ((0, 0), (0, 0), (0, SAMPLE_ROWS - Q_PER_KV * t_s), (0, 0)))
            page_rows = lambda a: a.reshape(a.shape[0], -1, HEAD_DIM)
            pad_page = lambda a: jnp.pad(page_rows(a), ((0, 0), (0, (PAGE_SIZE - t_s) * N_KV_HEADS), (0, 0)))
            o16 = _sample_attn(page_table, q16, scores, thr, cut, pad_page(kn), pad_page(vn),
                               page_rows(cache_k[j]), page_rows(cache_v[j]))
            o = o16[:, :, :Q_PER_KV * t_s].reshape(bs, N_KV_HEADS, Q_PER_KV, t_s, HEAD_DIM)
            ts = jnp.pad(o.transpose(0, 3, 1, 2, 4).reshape(bs, t_s, TOK_WIDTH),
                         ((0, 0), (0, SAMPLE_T_PAD - t_s), (0, 0)))

        mkp, mvp = _mem_kv(mem2d, w_mem_kv[layer].astype(BF16), mem_k_norm[layer])
        mk_l.append(mkp.reshape(bp, N_MEM, MEM_HEADS, MEM_HD))
        mv_l.append(mvp.reshape(bp, N_MEM, MEM_HEADS, MEM_HD))
        cp = _mem_attn(zp, mem_col, mem_q_norm[layer], mkp.reshape(bp, N_MEM, MEM_WIDTH),
                       mvp.reshape(bp, N_MEM, MEM_WIDTH), 512)
        cs = _mem_attn(zs, mem_col, mem_q_norm[layer], cache_mem_k[layer].reshape(bs, N_MEM, MEM_WIDTH),
                       cache_mem_v[layer].reshape(bs, N_MEM, MEM_WIDTH), SAMPLE_T_PAD)
        w_o = w_out[layer].astype(BF16)
        xp = _out_proj(xp.reshape(n_p, D_MODEL), tp.reshape(n_p, TOK_WIDTH), cp.reshape(n_p, MEM_WIDTH),
                       w_o, 512).reshape(bp, seq, D_MODEL)
        xs2 = _out_proj(xs.reshape(n_s, D_MODEL), ts.reshape(n_s, TOK_WIDTH), cs.reshape(n_s, MEM_WIDTH), w_o, n_s)

        wg, wu, wd = (w.astype(BF16) for w in (w_ffn_gate[layer], w_ffn_up[layer], w_ffn_down[layer]))
        ffn_w = (norm_ffn[layer], wg, wu, ffn_conv_w[layer], ffn_conv_b[layer], wd)
        xp, u_last = _ffn_long(xp, *ffn_w, jnp.zeros((bp, CONV_W - 1, D_FF), F32), 512)
        cvp_l.append(u_last[:, SUBLANES - (CONV_W - 1):, :])
        st = state_ffn_conv[layer]
        zrow = jnp.zeros((bs, SAMPLE_T_PAD - 1, D_FF), F32)
        p1 = jnp.concatenate([st[:, 1:2], zrow], axis=1).reshape(n_s, D_FF)
        p2 = jnp.concatenate([st, zrow[:, 1:]], axis=1).reshape(n_s, D_FF)
        xs2, u_s = _ffn_short(xs2, *ffn_w, p1, p2, SAMPLE_T_PAD)
        cvs_l.append(u_s.reshape(bs, SAMPLE_T_PAD, D_FF)[:, t_s - (CONV_W - 1):t_s])
        xs = jnp.where((row_t < t_s)[:, None], xs2, 0.0).reshape(bs, SAMPLE_T_PAD, D_MODEL)

    return (xp, xs[:, :t_s],
            jnp.stack(hgrn_p), jnp.stack(hgrn_s),
            jnp.stack(kp_l), jnp.stack(vp_l), jnp.stack(ikp_l),
            jnp.stack(ks_l), jnp.stack(vs_l), jnp.stack(iks_l),
            jnp.stack(mk_l), jnp.stack(mv_l),
            jnp.stack(cvp_l), jnp.stack(cvs_l))
```

```python
import functools
import math

import jax
import jax.numpy as jnp
from jax import lax
from jax.experimental import pallas as pl
from jax.experimental.pallas import tpu as pltpu

D_MODEL = 1024
DEPTH = 2
PAST_LEN = 8192
PAGE_SIZE = 128
MEM_WIDTH = 256
TOK_WIDTH = 768
MEM_HEADS = 4
MEM_HD = 64
N_MEM = 256
HGRN_DK = 128
HGRN_HEADS = 6
HGRN_CHUNK = 32
HEAD_DIM = 128
N_Q_HEADS = 6
N_KV_HEADS = 2
Q_PER_KV = N_Q_HEADS // N_KV_HEADS
IDX_HEADS = 8
IDX_DIM = 64
TOPK = 256
ROPE_THETA = 10000.0
D_FF = 2816
CONV_W = 3
EPS = 1e-6

LANES = 128
SUBLANES = 8
VMEM_LIMIT_BYTES = 56 * 1024 * 1024
FF_CHUNK = 256
HGRN_SUB = 256
KEY_BLOCK = 512
Q_TILE = 128
SAMPLE_T_PAD = 8
SCORE_PAGES_PER_STEP = 32
ATTN_PAGES_PER_STEP = 16
NEG_BIG = -1e30
QK_SCALE_LOG2E = HEAD_DIM ** -0.5 * math.log2(math.e)
F32 = jnp.float32
BF16 = jnp.bfloat16
I32 = jnp.int32
INT_MIN = -2 ** 31

DSA_Q0, DSA_K0, DSA_V0, DSA_IQ0, DSA_IK0, DSA_IW0, DSA_MEM0, DSA_COLS_PAD = 0, 768, 1024, 1280, 1792, 1920, 2048, 2304
HGRN_MEM0 = 4 * TOK_WIDTH


def _params(*sem):
    return pltpu.CompilerParams(dimension_semantics=sem, vmem_limit_bytes=VMEM_LIMIT_BYTES)


def _rms(x, gain):
    ms = jnp.mean(x * x, axis=-1, keepdims=True)
    return x * lax.rsqrt(ms + EPS) * gain


def _split_bf16(x):
    hi = x.astype(BF16)
    lo = (x - hi.astype(F32)).astype(BF16)
    return hi, lo


def _group_mean_sq(x, group):
    n = x.shape[-1]
    r = lax.broadcasted_iota(I32, (n, n), 0) // group
    c = lax.broadcasted_iota(I32, (n, n), 1) // group
    ones = jnp.where(r == c, 1.0, 0.0).astype(BF16)
    hi, lo = _split_bf16(x * x)
    s = jnp.dot(hi, ones, preferred_element_type=F32) + jnp.dot(lo, ones, preferred_element_type=F32)
    return s * (1.0 / group)


def _dot_nt(a, b):
    return lax.dot_general(a, b, (((1,), (1,)), ((), ())), preferred_element_type=F32)


def _dot_tn(a, b):
    return lax.dot_general(a, b, (((0,), (0,)), ((), ())), preferred_element_type=F32)


def _silu(x):
    return x * jax.nn.sigmoid(x)


def _norm_proj_kernel(x_ref, g_ref, w_ref, o_ref):
    h = _rms(x_ref[...], g_ref[...]).astype(BF16)
    o_ref[...] = jnp.dot(h, w_ref[...], preferred_element_type=F32)


def _norm_proj(x2d, gain, w_bf16, tm):
    n, d = x2d.shape
    c = w_bf16.shape[1]
    return pl.pallas_call(
        _norm_proj_kernel,
        out_shape=jax.ShapeDtypeStruct((n, c), F32),
        grid=(n // tm,),
        in_specs=[pl.BlockSpec((tm, d), lambda i: (i, 0)),
                  pl.BlockSpec((1, d), lambda i: (0, 0)),
                  pl.BlockSpec((d, c), lambda i: (0, 0))],
        out_specs=pl.BlockSpec((tm, c), lambda i: (i, 0)),
        compiler_params=_params("parallel"),
        name="norm_proj",
    )(x2d, gain.reshape(1, d), w_bf16)


def _hgrn_kernel(zq_ref, zf_ref, zi_ref, zg_ref, lb_ref, on_ref, s0_ref, o_ref, so_ref, st_ref,
                 *, chunk, sub, n_sub, valid_rows):
    step = pl.program_id(1)

    @pl.when(step == 0)
    def _():
        for h in range(HGRN_HEADS):
            st_ref[h] = s0_ref[0, h].T

    row = lax.broadcasted_iota(I32, (sub, sub), 0)
    col = lax.broadcasted_iota(I32, (sub, sub), 1)
    same_chunk = (row // chunk) == (col // chunk)
    causal = jnp.logical_and(same_chunk, row >= col)
    sum_lhs = jnp.concatenate([jnp.where(causal, 1.0, 0.0), jnp.where(same_chunk, 1.0, 0.0)], axis=0).astype(BF16)
    row_id = lax.broadcasted_iota(I32, (sub, HGRN_DK), 0)
    row_ok = row_id % chunk < valid_rows
    n_c = sub // chunk
    in_chunk = [row_id // chunk == c for c in range(n_c)]

    def body(si, carry):
        rows = pl.ds(pl.multiple_of(si * sub, sub), sub)
        for h in range(HGRN_HEADS):
            sl = slice(h * HGRN_DK, (h + 1) * HGRN_DK)
            lb = lb_ref[:, sl]
            q = _silu(zq_ref[0, rows, sl])
            f = lb + (1.0 - lb) * jax.nn.sigmoid(zf_ref[0, rows, sl])
            if valid_rows < chunk:
                f = jnp.where(row_ok, f, 1.0)
            k = 1.0 - f
            vb = zi_ref[0, rows, sl].astype(BF16)
            g_hi, g_lo = _split_bf16(jnp.log(f))
            sums = jnp.dot(sum_lhs, jnp.concatenate([g_hi, g_lo], axis=1), preferred_element_type=F32)
            b = sums[:sub, :HGRN_DK] + sums[:sub, HGRN_DK:]
            b_chunk = sums[sub:, :HGRN_DK] + sums[sub:, HGRN_DK:]
            q_dec = q * jnp.exp(b)
            k_dec = (k * jnp.exp(-b)).astype(BF16)
            k_rem = k * jnp.exp(b_chunk - b)
            chunk_decay = jnp.exp(b_chunk)
            scores = jnp.where(causal, _dot_nt(q_dec.astype(BF16), k_dec), 0.0)
            o = jnp.dot(scores.astype(BF16), vb, preferred_element_type=F32)
            spread = lambda a: jnp.concatenate([jnp.where(in_chunk[c], a, 0.0) for c in range(n_c)],
                                               axis=1).astype(BF16)
            incr = _dot_tn(vb, spread(k_rem))
            st = st_ref[h]
            states = []
            for c in range(n_c):
                states.append(st.astype(BF16))
                st = st * chunk_decay[c * chunk:c * chunk + 1] + incr[:, c * HGRN_DK:(c + 1) * HGRN_DK]
            st_ref[h] = st
            o = o + _dot_nt(spread(q_dec), jnp.concatenate(states, axis=1))
            o_ref[0, rows, sl] = _rms(o, on_ref[...]) * _silu(zg_ref[0, rows, sl])
        return carry

    lax.fori_loop(0, n_sub, body, 0)

    @pl.when(step == pl.num_programs(1) - 1)
    def _():
        for h in range(HGRN_HEADS):
            so_ref[0, h] = st_ref[h].T


def _hgrn(z3d, lb, out_norm, s0, chunk, sub, t_step, valid_rows):
    bsz, t, _ = z3d.shape
    col = lambda j: pl.BlockSpec((1, t_step, TOK_WIDTH), lambda b, i, j=j: (b, i, j))
    state_spec = pl.BlockSpec((1, HGRN_HEADS, HGRN_DK, HGRN_DK), lambda b, i: (b, 0, 0, 0))
    kern = functools.partial(_hgrn_kernel, chunk=chunk, sub=sub, n_sub=t_step // sub, valid_rows=valid_rows)
    return pl.pallas_call(
        kern,
        out_shape=(jax.ShapeDtypeStruct((bsz, t, TOK_WIDTH), F32),
                   jax.ShapeDtypeStruct((bsz, HGRN_HEADS, HGRN_DK, HGRN_DK), F32)),
        grid=(bsz, t // t_step),
        in_specs=[col(0), col(1), col(2), col(3),
                  pl.BlockSpec((1, TOK_WIDTH), lambda b, i: (0, 0)),
                  pl.BlockSpec((1, HGRN_DK), lambda b, i: (0, 0)),
                  state_spec],
        out_specs=(pl.BlockSpec((1, t_step, TOK_WIDTH), lambda b, i: (b, i, 0)), state_spec),
        scratch_shapes=[pltpu.VMEM((HGRN_HEADS, HGRN_DK, HGRN_DK), F32)],
        compiler_params=_params("parallel", "arbitrary"),
        name="hgrn2",
    )(z3d, z3d, z3d, z3d, lb.reshape(1, TOK_WIDTH), out_norm.reshape(1, HGRN_DK), s0)


def _mem_kv_kernel(m_ref, w_ref, g_ref, k_ref, v_ref):
    kv = jnp.dot(m_ref[...].astype(BF16), w_ref[...], preferred_element_type=F32)
    k = kv[:, :MEM_WIDTH]
    k_ref[...] = k * lax.rsqrt(_group_mean_sq(k, MEM_HD) + EPS) * g_ref[...]
    v_ref[...] = kv[:, MEM_WIDTH:]


def _mem_kv(mem2d, w_bf16, k_gain):
    n = mem2d.shape[0]
    out = jax.ShapeDtypeStruct((n, MEM_WIDTH), F32)
    return pl.pallas_call(
        _mem_kv_kernel,
        out_shape=(out, out),
        grid=(n // N_MEM,),
        in_specs=[pl.BlockSpec((N_MEM, D_MODEL), lambda i: (i, 0)),
                  pl.BlockSpec((D_MODEL, 2 * MEM_WIDTH), lambda i: (0, 0)),
                  pl.BlockSpec((1, MEM_WIDTH), lambda i: (0, 0))],
        out_specs=(pl.BlockSpec((N_MEM, MEM_WIDTH), lambda i: (i, 0)),
                   pl.BlockSpec((N_MEM, MEM_WIDTH), lambda i: (i, 0))),
        compiler_params=_params("parallel"),
        name="mem_kv",
    )(mem2d, w_bf16, jnp.tile(k_gain, MEM_HEADS).reshape(1, MEM_WIDTH))


def _mem_attn_kernel(zq_ref, g_ref, mk_ref, mv_ref, o_ref):
    q = zq_ref[0]
    qn = q * lax.rsqrt(_group_mean_sq(q, MEM_HD) + EPS) * g_ref[...]
    mk = mk_ref[0].astype(BF16)
    mv = mv_ref[0].astype(BF16)
    head_of_lane = lax.broadcasted_iota(I32, qn.shape, 1) // MEM_HD
    out = jnp.zeros(qn.shape, F32)
    for h in range(MEM_HEADS):
        mine = head_of_lane == h
        s = _dot_nt(jnp.where(mine, qn, 0.0).astype(BF16), mk) * (MEM_HD ** -0.5)
        p = jnp.exp(s - jnp.max(s, axis=-1, keepdims=True))
        oh = jnp.dot(p.astype(BF16), mv, preferred_element_type=F32) / jnp.sum(p, axis=-1, keepdims=True)
        out = jnp.where(mine, oh, out)
    o_ref[0] = out


def _mem_attn(z3d, q_col_block, q_gain, mk, mv, tm):
    bsz, t, _ = z3d.shape
    return pl.pallas_call(
        _mem_attn_kernel,
        out_shape=jax.ShapeDtypeStruct((bsz, t, MEM_WIDTH), F32),
        grid=(bsz, t // tm),
        in_specs=[pl.BlockSpec((1, tm, MEM_WIDTH), lambda b, i: (b, i, q_col_block)),
                  pl.BlockSpec((1, MEM_WIDTH), lambda b, i: (0, 0)),
                  pl.BlockSpec((1, N_MEM, MEM_WIDTH), lambda b, i: (b, 0, 0)),
                  pl.BlockSpec((1, N_MEM, MEM_WIDTH), lambda b, i: (b, 0, 0))],
        out_specs=pl.BlockSpec((1, tm, MEM_WIDTH), lambda b, i: (b, i, 0)),
        compiler_params=_params("parallel", "parallel"),
        name="mem_attn",
    )(z3d, jnp.tile(q_gain, MEM_HEADS).reshape(1, MEM_WIDTH), mk, mv)


def _out_proj_kernel(x_ref, t_ref, c_ref, wt_ref, wc_ref, o_ref):
    o_ref[...] = (x_ref[...]
                  + jnp.dot(t_ref[...].astype(BF16), wt_ref[...], preferred_element_type=F32)
                  + jnp.dot(c_ref[...].astype(BF16), wc_ref[...], preferred_element_type=F32))


def _out_proj(x2d, t2d, c2d, w_bf16, tm):
    n = x2d.shape[0]
    row = lambda w: pl.BlockSpec((tm, w), lambda i: (i, 0))
    return pl.pallas_call(
        _out_proj_kernel,
        out_shape=jax.ShapeDtypeStruct((n, D_MODEL), F32),
        grid=(n // tm,),
        in_specs=[row(D_MODEL), row(TOK_WIDTH), row(MEM_WIDTH),
                  pl.BlockSpec((TOK_WIDTH, D_MODEL), lambda i: (0, 0)),
                  pl.BlockSpec((MEM_WIDTH, D_MODEL), lambda i: (0, 0))],
        out_specs=row(D_MODEL),
        compiler_params=_params("parallel"),
        name="out_proj",
    )(x2d, t2d, c2d, w_bf16[:TOK_WIDTH], w_bf16[TOK_WIDTH:])


def _ffn_kernel(x_ref, g_ref, wg_ref, wu_ref, cw_ref, cb_ref, wd_ref, p1_ref, p2_ref, y_ref, u_ref, carry_ref,
                *, seq_rows, carry):
    tm = x_ref.shape[0]
    x = x_ref[...]
    h = _rms(x, g_ref[...]).astype(BF16)
    t_in_seq = lax.broadcasted_iota(I32, (tm, FF_CHUNK), 0) % seq_rows
    first_block = pl.program_id(1) == 0 if carry else None
    n_chunks = D_FF // FF_CHUNK

    def project(c):
        cs = slice(c * FF_CHUNK, (c + 1) * FF_CHUNK)
        return (jnp.dot(h, wg_ref[:, cs], preferred_element_type=F32),
                jnp.dot(h, wu_ref[:, cs], preferred_element_type=F32))

    acc = x
    ahead = project(0)
    for c in range(n_chunks):
        cs = slice(c * FF_CHUNK, (c + 1) * FF_CHUNK)
        u, up = ahead
        if c + 1 < n_chunks:
            ahead = project(c + 1)
        u1 = pltpu.roll(u, 1, 0)
        u2 = pltpu.roll(u, 2, 0)
        if carry:
            prev = carry_ref[:, cs]
            start1 = jnp.where(first_block, p1_ref[0:1, cs], prev[7:8, :])
            start2a = jnp.where(first_block, p2_ref[0:1, cs], prev[6:7, :])
            start2b = jnp.where(first_block, p2_ref[1:2, cs], prev[7:8, :])
            u1 = jnp.where(t_in_seq == 0, start1, u1)
            u2 = jnp.where(t_in_seq == 0, start2a, jnp.where(t_in_seq == 1, start2b, u2))
            carry_ref[:, cs] = u[tm - SUBLANES:, :]
        else:
            u1 = jnp.where(t_in_seq == 0, p1_ref[:, cs], u1)
            u2 = jnp.where(t_in_seq < 2, p2_ref[:, cs], u2)
        conv = cb_ref[:, cs] + u2 * cw_ref[0:1, cs] + u1 * cw_ref[1:2, cs] + u * cw_ref[2:3, cs]
        a = (_silu(conv) * up).astype(BF16)
        acc = acc + jnp.dot(a, wd_ref[cs, :], preferred_element_type=F32)
        if carry:
            u_ref[0, :, cs] = u[tm - SUBLANES:, :]
        else:
            u_ref[:, cs] = u
    y_ref[...] = acc


def _ffn_weights_specs(idx):
    once = pl.Buffered(1)
    return [pl.BlockSpec((1, D_MODEL), idx),
            pl.BlockSpec((D_MODEL, D_FF), idx, pipeline_mode=once),
            pl.BlockSpec((D_MODEL, D_FF), idx, pipeline_mode=once),
            pl.BlockSpec((CONV_W, D_FF), idx),
            pl.BlockSpec((1, D_FF), idx),
            pl.BlockSpec((D_FF, D_MODEL), idx, pipeline_mode=once)]


def _ffn_long(x3d, gain, wg, wu, cw, cb, wd, state, tm):
    bsz, t, _ = x3d.shape
    idx = lambda b, i: (0, 0)
    kern = functools.partial(_ffn_kernel, seq_rows=t, carry=True)
    x2d = x3d.reshape(bsz * t, D_MODEL)
    nb = t // tm
    p1 = state[:, 1:2, :]
    y, u_last = pl.pallas_call(
        kern,
        out_shape=(jax.ShapeDtypeStruct((bsz * t, D_MODEL), F32),
                   jax.ShapeDtypeStruct((bsz, SUBLANES, D_FF), F32)),
        grid=(bsz, nb),
        in_specs=[pl.BlockSpec((tm, D_MODEL), lambda b, i: (b * nb + i, 0))] + _ffn_weights_specs(idx)
                 + [pl.BlockSpec((None, 1, D_FF), lambda b, i: (b, 0, 0)),
                    pl.BlockSpec((None, 2, D_FF), lambda b, i: (b, 0, 0))],
        out_specs=(pl.BlockSpec((tm, D_MODEL), lambda b, i: (b * nb + i, 0)),
                   pl.BlockSpec((1, SUBLANES, D_FF), lambda b, i: (b, 0, 0))),
        scratch_shapes=[pltpu.VMEM((SUBLANES, D_FF), F32)],
        compiler_params=_params("parallel", "arbitrary"),
        name="conv_ffn_long",
    )(x2d, gain.reshape(1, D_MODEL), wg, wu, cw, cb.reshape(1, D_FF), wd, p1, state)
    return y.reshape(bsz, t, D_MODEL), u_last


def _ffn_short(x2d, gain, wg, wu, cw, cb, wd, p1, p2, seq_rows):
    n = x2d.shape[0]
    idx = lambda i: (0, 0)
    kern = functools.partial(_ffn_kernel, seq_rows=seq_rows, carry=False)
    row = lambda w: pl.BlockSpec((n, w), lambda i: (0, 0))
    return pl.pallas_call(
        kern,
        out_shape=(jax.ShapeDtypeStruct((n, D_MODEL), F32), jax.ShapeDtypeStruct((n, D_FF), F32)),
        grid=(1,),
        in_specs=[row(D_MODEL)] + _ffn_weights_specs(idx) + [row(D_FF), row(D_FF)],
        out_specs=(row(D_MODEL), row(D_FF)),
        scratch_shapes=[pltpu.VMEM((SUBLANES, D_FF), F32)],
        compiler_params=_params("arbitrary"),
        name="conv_ffn_short",
    )(x2d, gain.reshape(1, D_MODEL), wg, wu, cw, cb.reshape(1, D_FF), wd, p1, p2)


def _rope128(x, cos, sin_signed):
    return x * cos + pltpu.roll(x, HEAD_DIM // 2, 1) * sin_signed


def _rope64_pairs(x, cos, sin_lo, sin_hi):
    return (x * cos + pltpu.roll(x, LANES - IDX_DIM // 2, 1) * sin_lo
            + pltpu.roll(x, IDX_DIM // 2, 1) * sin_hi)


def _dsa_prep_kernel(z_ref, c128_ref, s128_ref, c64_ref, s64lo_ref, s64hi_ref, qg_ref, kg_ref, ikg_ref,
                     q_ref, k_ref, kb_ref, v_ref, vt_ref, iq_ref, ik_ref, ikb_ref, iwt_ref):
    c128, s128 = c128_ref[...], s128_ref[...]
    c64, s64lo, s64hi = c64_ref[...], s64lo_ref[...], s64hi_ref[...]
    for h in range(N_Q_HEADS):
        sl = slice(DSA_Q0 + h * HEAD_DIM, DSA_Q0 + (h + 1) * HEAD_DIM)
        q = _rope128(_rms(z_ref[0, :, sl], qg_ref[...]), c128, s128)
        q_ref[0, :, h * HEAD_DIM:(h + 1) * HEAD_DIM] = (q * QK_SCALE_LOG2E).astype(BF16)
    for h in range(N_KV_HEADS):
        sl = slice(DSA_K0 + h * HEAD_DIM, DSA_K0 + (h + 1) * HEAD_DIM)
        k = _rope128(_rms(z_ref[0, :, sl], kg_ref[...]), c128, s128)
        k_ref[0, :, h * HEAD_DIM:(h + 1) * HEAD_DIM] = k
        kb_ref[0, :, h * HEAD_DIM:(h + 1) * HEAD_DIM] = k.astype(BF16)
    v = z_ref[0, :, DSA_V0:DSA_V0 + N_KV_HEADS * HEAD_DIM]
    v_ref[0] = v
    vt_ref[0, 0] = v.T.astype(BF16)
    for p in range(IDX_HEADS // 2):
        sl = slice(DSA_IQ0 + p * LANES, DSA_IQ0 + (p + 1) * LANES)
        iq_ref[0, :, p * LANES:(p + 1) * LANES] = _rope64_pairs(z_ref[0, :, sl], c64, s64lo, s64hi)
    ik = _rope64_pairs(_rms(z_ref[0, :, DSA_IK0:DSA_IK0 + LANES], ikg_ref[...]), c64, s64lo, s64hi)
    ik_ref[0] = ik
    ik_hi, ik_lo = _split_bf16(ik)
    hi_lo = jnp.where(lax.broadcasted_iota(I32, ik.shape, 1) < IDX_DIM, ik_hi, ik_lo)
    ikb_ref[0, :, :LANES] = hi_lo
    ikb_ref[0, :, LANES:] = hi_lo
    iw = z_ref[0, :, DSA_IW0:DSA_IW0 + LANES] * (IDX_HEADS ** -0.5 * IDX_DIM ** -0.5)
    iwt_ref[0] = iw.T[:IDX_HEADS, :]


def _dsa_prep(z3d, tabs, q_gain, k_gain, ik_gain, tm):
    bsz, t, _ = z3d.shape
    nb = t // tm
    tab = pl.BlockSpec((tm, LANES), lambda b, i: (i, 0))
    gain = pl.BlockSpec((1, LANES), lambda b, i: (0, 0))
    rows = lambda w: pl.BlockSpec((1, tm, w), lambda b, i: (b, i, 0))
    kvw = N_KV_HEADS * HEAD_DIM
    sds = jax.ShapeDtypeStruct
    return pl.pallas_call(
        _dsa_prep_kernel,
        out_shape=(sds((bsz, t, TOK_WIDTH), BF16),
                   sds((bsz, t, kvw), F32), sds((bsz, t, kvw), BF16),
                   sds((bsz, t, kvw), F32), sds((bsz, nb, kvw, tm), BF16),
                   sds((bsz, t, IDX_HEADS * IDX_DIM), F32),
                   sds((bsz, t, LANES), F32),
                   sds((bsz, t, 2 * LANES), BF16),
                   sds((bsz, IDX_HEADS, t), F32)),
        grid=(bsz, nb),
        in_specs=[rows(DSA_COLS_PAD), tab, tab, tab, tab, tab, gain, gain, gain],
        out_specs=(rows(TOK_WIDTH), rows(kvw), rows(kvw), rows(kvw),
                   pl.BlockSpec((1, 1, kvw, tm), lambda b, i: (b, i, 0, 0)),
                   rows(IDX_HEADS * IDX_DIM), rows(LANES), rows(2 * LANES),
                   pl.BlockSpec((1, IDX_HEADS, tm), lambda b, i: (b, 0, i))),
        compiler_params=_params("parallel", "parallel"),
        name="dsa_prep",
    )(z3d, *tabs, q_gain.reshape(1, LANES), k_gain.reshape(1, LANES), jnp.tile(ik_gain, 2).reshape(1, LANES))


def _rope_tables(pos):
    pos = pos.astype(F32)[:, None]

    def cs(half):
        inv = ROPE_THETA ** (-jnp.arange(half, dtype=F32) / half)
        ang = pos * inv[None, :]
        return jnp.cos(ang), jnp.sin(ang)

    c, s = cs(HEAD_DIM // 2)
    c128 = jnp.concatenate([c, c], axis=1)
    s128 = jnp.concatenate([-s, s], axis=1)
    c, s = cs(IDX_DIM // 2)
    z = jnp.zeros_like(s)
    c64 = jnp.concatenate([c, c, c, c], axis=1)
    s64lo = jnp.concatenate([-s, z, -s, z], axis=1)
    s64hi = jnp.concatenate([z, s, z, s], axis=1)
    return c128, s128, c64, s64lo, s64hi


def _ordered_bits_to_float(u):
    key = u ^ INT_MIN
    bits = key ^ ((key >> 31) & 0x7FFFFFFF)
    return lax.bitcast_convert_type(bits, F32)


def _topk_threshold(count_ge, count_gt, count_eq_before, any_true, shape, idx_bits):
    def value_step(i, prefix):
        cand = prefix | jnp.left_shift(jnp.int32(1), 31 - i)
        below_neg_inf = jnp.logical_and(cand >= 0, cand < 0x00800000)
        ok = jnp.logical_or(count_ge(_ordered_bits_to_float(cand)) >= TOPK, below_neg_inf)
        return jnp.where(ok, cand, prefix)

    thr = _ordered_bits_to_float(lax.fori_loop(0, 32, value_step, jnp.zeros(shape, I32)))
    n_gt = count_gt(thr)
    need = TOPK - n_gt
    n_eq = count_ge(thr) - n_gt
    finite_thr = thr > -jnp.inf
    excess = jnp.logical_and(n_eq > need, finite_thr)

    def index_search():
        def index_step(i, cut):
            cand = cut | jnp.left_shift(jnp.int32(1), idx_bits - 1 - i)
            return jnp.where(count_eq_before(thr, cand) < need, cand, cut)
        return lax.fori_loop(0, idx_bits, index_step, jnp.zeros(shape, I32))

    everything = jnp.full(shape, 2 ** idx_bits - 1, I32)
    cut = lax.cond(any_true(excess), index_search, lambda: everything)
    return thr, jnp.where(finite_thr, cut, -1)


def _dsa_prompt_kernel(q_ref, iq_ref, iwt_ref, ik_ref, k_ref, vt_ref, o_ref,
                       score_ref, iqm_ref, qm_ref, m_ref, l_ref, acc_ref):
    qi = pl.program_id(1)
    q0 = qi * Q_TILE
    n_kb = (q0 + Q_TILE + KEY_BLOCK - 1) // KEY_BLOCK
    lane = lax.broadcasted_iota(I32, (Q_TILE, LANES), 1)
    first_half = lane < IDX_DIM
    for p in range(IDX_HEADS // 2):
        pair = iq_ref[0, :, p * LANES:(p + 1) * LANES]
        swapped = pltpu.roll(pair, IDX_DIM, 1)
        for half in range(2):
            dup = jnp.where(first_half, pair, swapped) if half == 0 else jnp.where(first_half, swapped, pair)
            hi, lo = _split_bf16(dup)
            rows = slice(half * Q_TILE, (half + 1) * Q_TILE)
            iqm_ref[p, rows, :LANES] = hi
            iqm_ref[p, rows, LANES:] = lo
    for h in range(N_Q_HEADS):
        g = h // Q_PER_KV
        qh = q_ref[0, :, h * HEAD_DIM:(h + 1) * HEAD_DIM]
        rows = slice(h * Q_TILE, (h + 1) * Q_TILE)
        qm_ref[rows, g * HEAD_DIM:(g + 1) * HEAD_DIM] = qh
        qm_ref[rows, (1 - g) * HEAD_DIM:(2 - g) * HEAD_DIM] = jnp.zeros_like(qh)
    q_pos = q0 + lax.broadcasted_iota(I32, (1, Q_TILE), 1)
    key_row = lax.broadcasted_iota(I32, (KEY_BLOCK, 1), 0)

    def score_block(kb, carry):
        s0 = pl.multiple_of(kb * KEY_BLOCK, KEY_BLOCK)
        ik = ik_ref[0, pl.ds(s0, KEY_BLOCK), :]
        acc = jnp.zeros((KEY_BLOCK, Q_TILE), F32)
        for p in range(IDX_HEADS // 2):
            d = _dot_nt(ik, iqm_ref[p])
            for half in range(2):
                h = 2 * p + half
                acc = acc + iwt_ref[0, h:h + 1, :] * jnp.maximum(d[:, half * Q_TILE:(half + 1) * Q_TILE], 0.0)
        score_ref[pl.ds(s0, KEY_BLOCK), :] = jnp.where(s0 + key_row <= q_pos, acc, -jnp.inf)
        return carry

    lax.fori_loop(0, n_kb, score_block, 0)

    def count(pred):
        def body(kb, cnt):
            s0 = pl.multiple_of(kb * KEY_BLOCK, KEY_BLOCK)
            hit = pred(score_ref[pl.ds(s0, KEY_BLOCK), :], s0 + key_row).astype(I32)
            return cnt + jnp.sum(hit.reshape(KEY_BLOCK // SUBLANES, SUBLANES, Q_TILE), axis=0)
        part = lax.fori_loop(0, n_kb, body, jnp.zeros((SUBLANES, Q_TILE), I32))
        return jnp.sum(part, axis=0, keepdims=True)

    thr, cut = _topk_threshold(
        count_ge=lambda c: count(lambda s, idx: s >= c),
        count_gt=lambda c: count(lambda s, idx: s > c),
        count_eq_before=lambda t, j: count(lambda s, idx: jnp.logical_and(s == t, idx < j)),
        any_true=lambda m: jnp.max(m.astype(I32)) > 0,
        shape=(1, Q_TILE), idx_bits=13)

    m_ref[...] = jnp.full(m_ref.shape, NEG_BIG, F32)
    l_ref[...] = jnp.zeros(l_ref.shape, F32)
    acc_ref[...] = jnp.zeros(acc_ref.shape, F32)

    def attend_block(kb, carry):
        s0 = pl.multiple_of(kb * KEY_BLOCK, KEY_BLOCK)
        sc = score_ref[pl.ds(s0, KEY_BLOCK), :]
        sel = jnp.logical_or(sc > thr, jnp.logical_and(sc == thr, s0 + key_row <= cut))
        k_blk = k_ref[0, pl.ds(s0, KEY_BLOCK), :]
        alphas, ps = [], []
        for j in range(N_Q_HEADS // 2):
            s2 = _dot_nt(k_blk, qm_ref[2 * j * Q_TILE:2 * (j + 1) * Q_TILE, :])
            for half in range(2):
                h = 2 * j + half
                s = jnp.where(sel, s2[:, half * Q_TILE:(half + 1) * Q_TILE], NEG_BIG)
                m_old = m_ref[h:h + 1, :]
                m_new = jnp.maximum(m_old, jnp.max(s, axis=0, keepdims=True))
                alpha = jnp.exp2(m_old - m_new)
                p = jnp.exp2(s - m_new)
                l_ref[h:h + 1, :] = alpha * l_ref[h:h + 1, :] + jnp.sum(p, axis=0, keepdims=True)
                m_ref[h:h + 1, :] = m_new
                alphas.append(alpha)
                ps.append(p.astype(BF16))
        for g in range(N_KV_HEADS):
            vtg = vt_ref[0, kb, g * HEAD_DIM:(g + 1) * HEAD_DIM, :]
            pv = jnp.dot(vtg, jnp.concatenate(ps[g * Q_PER_KV:(g + 1) * Q_PER_KV], axis=1),
                         preferred_element_type=F32)
            for r in range(Q_PER_KV):
                h = g * Q_PER_KV + r
                acc_ref[h] = alphas[h] * acc_ref[h] + pv[:, r * Q_TILE:(r + 1) * Q_TILE]
        return carry

    lax.fori_loop(0, n_kb, attend_block, 0)
    for h in range(N_Q_HEADS):
        o_ref[0, :, h * HEAD_DIM:(h + 1) * HEAD_DIM] = (acc_ref[h] / l_ref[h:h + 1, :]).T


def _dsa_prompt(q, iq, iwt, ikb, kb, vt):
    bsz, t, _ = q.shape
    kvw = N_KV_HEADS * HEAD_DIM
    full = lambda w: pl.BlockSpec((1, t, w), lambda b, i: (b, 0, 0))
    return pl.pallas_call(
        _dsa_prompt_kernel,
        out_shape=jax.ShapeDtypeStruct((bsz, t, TOK_WIDTH), F32),
        grid=(bsz, t // Q_TILE),
        in_specs=[pl.BlockSpec((1, Q_TILE, TOK_WIDTH), lambda b, i: (b, i, 0)),
                  pl.BlockSpec((1, Q_TILE, IDX_HEADS * IDX_DIM), lambda b, i: (b, i, 0)),
                  pl.BlockSpec((1, IDX_HEADS, Q_TILE), lambda b, i: (b, 0, i)),
                  full(2 * LANES), full(kvw),
                  pl.BlockSpec((1, t // KEY_BLOCK, kvw, KEY_BLOCK), lambda b, i: (b, 0, 0, 0))],
        out_specs=pl.BlockSpec((1, Q_TILE, TOK_WIDTH), lambda b, i: (b, i, 0)),
        scratch_shapes=[pltpu.VMEM((t, Q_TILE), F32),
                        pltpu.VMEM((IDX_HEADS // 2, 2 * Q_TILE, 2 * LANES), BF16),
                        pltpu.VMEM((N_Q_HEADS * Q_TILE, N_KV_HEADS * HEAD_DIM), BF16),
                        pltpu.VMEM((SUBLANES, Q_TILE), F32),
                        pltpu.VMEM((SUBLANES, Q_TILE), F32),
                        pltpu.VMEM((N_Q_HEADS, HEAD_DIM, Q_TILE), F32)],
        compiler_params=_params("parallel", "arbitrary"),
        name="dsa_prompt",
    )(q, iq, iwt, ikb, kb, vt)


SAMPLE_ROWS = 16
SAMPLE_T = 4
N_PAGES = PAST_LEN // PAGE_SIZE
N_KEY_PAGES = N_PAGES + 1


def _sample_score_kernel(pt_ref, iq4_ref, iw_ref, ikn_ref, *rest):
    page_refs = rest[:SCORE_PAGES_PER_STEP]
    (score_ref,) = rest[SCORE_PAGES_PER_STEP:]
    step = pl.program_id(1)
    iq_hi, iq_lo = _split_bf16(iq4_ref[0])
    iq_parts = jnp.where(lax.broadcasted_iota(I32, iq_hi.shape, 1) < 2 * IDX_DIM, iq_hi, iq_lo)
    w = iw_ref[0]

    def page_scores(ik_t, page, allowed):
        k_hi, k_lo = _split_bf16(ik_t)
        d = jnp.dot(iq_parts, jnp.concatenate([k_hi, k_lo, k_hi, k_lo], axis=0), preferred_element_type=F32)
        sc = jnp.sum((jnp.maximum(d, 0.0) * w).reshape(SAMPLE_T, IDX_HEADS, PAGE_SIZE), axis=1)
        if allowed is not None:
            sc = jnp.where(allowed, sc, -jnp.inf)
        for r in range(SAMPLE_ROWS // SAMPLE_T):
            score_ref[0, page, r * SAMPLE_T:(r + 1) * SAMPLE_T, :] = sc

    for i in range(SCORE_PAGES_PER_STEP):
        page_scores(page_refs[i][0], step * SCORE_PAGES_PER_STEP + i, None)

    @pl.when(step == pl.num_programs(1) - 1)
    def _():
        new_idx = lax.broadcasted_iota(I32, (SAMPLE_T, PAGE_SIZE), 1)
        page_scores(ikn_ref[0], N_PAGES, new_idx <= lax.broadcasted_iota(I32, (SAMPLE_T, PAGE_SIZE), 0))


def _sample_scores(page_table, iq4, iw_col, ik_new_t, pool_ik_t):
    bsz = iq4.shape[0]
    page_spec = lambda i: pl.BlockSpec((1, IDX_DIM, PAGE_SIZE),
                                       lambda b, s, pt, i=i: (pt[b, s * SCORE_PAGES_PER_STEP + i], 0, 0))
    per_b = lambda shape: pl.BlockSpec((1,) + shape, lambda b, s, pt: (b,) + (0,) * len(shape))
    grid_spec = pltpu.PrefetchScalarGridSpec(
        num_scalar_prefetch=1,
        grid=(bsz, N_PAGES // SCORE_PAGES_PER_STEP),
        in_specs=[per_b((SAMPLE_T * IDX_HEADS, 4 * IDX_DIM)), per_b((SAMPLE_T * IDX_HEADS, 1)),
                  per_b((IDX_DIM, PAGE_SIZE))] + [page_spec(i) for i in range(SCORE_PAGES_PER_STEP)],
        out_specs=per_b((N_KEY_PAGES, SAMPLE_ROWS, PAGE_SIZE)))
    return pl.pallas_call(
        _sample_score_kernel,
        out_shape=jax.ShapeDtypeStruct((bsz, N_KEY_PAGES, SAMPLE_ROWS, PAGE_SIZE), F32),
        grid_spec=grid_spec,
        compiler_params=_params("parallel", "arbitrary"),
        name="dsa_sample_scores",
    )(page_table, iq4, iw_col, ik_new_t, *([pool_ik_t] * SCORE_PAGES_PER_STEP))


def _sample_select_kernel(score_ref, thr_ref, cut_ref):
    n_q, n_k = score_ref.shape
    key_idx = lax.broadcasted_iota(I32, (1, n_k), 1)

    def count(pred):
        return jnp.sum(pred(score_ref[...], key_idx).astype(I32), axis=-1, keepdims=True)

    thr, cut = _topk_threshold(
        count_ge=lambda c: count(lambda s, idx: s >= c),
        count_gt=lambda c: count(lambda s, idx: s > c),
        count_eq_before=lambda t, j: count(lambda s, idx: jnp.logical_and(s == t, idx < j)),
        any_true=lambda m: jnp.max(m.astype(I32)) > 0,
        shape=(n_q, 1), idx_bits=14)
    thr_ref[...] = jnp.broadcast_to(thr, thr_ref.shape)
    cut_ref[...] = jnp.broadcast_to(cut, cut_ref.shape)


def _sample_select(scores2d):
    n_q, n_k = scores2d.shape
    whole = lambda shape: pl.BlockSpec(shape, lambda i: (0, 0))
    return pl.pallas_call(
        _sample_select_kernel,
        out_shape=(jax.ShapeDtypeStruct((n_q, LANES), F32), jax.ShapeDtypeStruct((n_q, LANES), I32)),
        grid=(1,),
        in_specs=[whole((n_q, n_k))],
        out_specs=(whole((n_q, LANES)), whole((n_q, LANES))),
        compiler_params=_params("arbitrary"),
        name="dsa_sample_select",
    )(scores2d)


def _sample_attn_kernel(pt_ref, q_ref, sc_ref, thr_ref, cut_ref, kn_ref, vn_ref, *rest):
    k_pages = rest[:ATTN_PAGES_PER_STEP]
    v_pages = rest[ATTN_PAGES_PER_STEP:2 * ATTN_PAGES_PER_STEP]
    o_ref, m_ref, l_ref, acc_ref = rest[2 * ATTN_PAGES_PER_STEP:]
    step = pl.program_id(1)

    @pl.when(step == 0)
    def _():
        m_ref[...] = jnp.full(m_ref.shape, NEG_BIG, F32)
        l_ref[...] = jnp.zeros(l_ref.shape, F32)
        acc_ref[...] = jnp.zeros(acc_ref.shape, F32)

    thr = thr_ref[0]
    cut = cut_ref[0]
    lane = lax.broadcasted_iota(I32, (SAMPLE_ROWS, PAGE_SIZE), 1)

    def attend(pages):
        sels = []
        for page, _, _ in pages:
            sc = sc_ref[0, page]
            sels.append(jnp.logical_or(sc > thr, jnp.logical_and(sc == thr, page * PAGE_SIZE + lane <= cut)))
        cat = lambda parts, axis: parts[0] if len(parts) == 1 else jnp.concatenate(parts, axis=axis)
        sel = cat(sels, 1)
        for g in range(N_KV_HEADS):
            rows_g = pl.ds(g, PAGE_SIZE, stride=N_KV_HEADS)
            k_all = cat([k[0, rows_g, :].astype(BF16) for _, k, _ in pages], 0)
            v_all = cat([v[0, rows_g, :].astype(BF16) for _, _, v in pages], 0)
            s = jnp.where(sel, _dot_nt(q_ref[0, g], k_all), NEG_BIG)
            m_old = m_ref[g]
            m_new = jnp.maximum(m_old, jnp.max(s, axis=-1, keepdims=True))
            alpha = jnp.exp2(m_old - m_new)
            p = jnp.exp2(s - m_new)
            l_ref[g] = alpha * l_ref[g] + jnp.sum(p, axis=-1, keepdims=True)
            acc_ref[g] = alpha * acc_ref[g] + jnp.dot(p.astype(BF16), v_all, preferred_element_type=F32)
            m_ref[g] = m_new

    attend([(step * ATTN_PAGES_PER_STEP + i, k_pages[i], v_pages[i]) for i in range(ATTN_PAGES_PER_STEP)])

    @pl.when(step == pl.num_programs(1) - 1)
    def _():
        attend([(N_PAGES, kn_ref, vn_ref)])
        for g in range(N_KV_HEADS):
            o_ref[0, g] = acc_ref[g] / l_ref[g]


def _sample_attn(page_table, q16, scores, thr, cut, k_new_pad, v_new_pad, pool_k, pool_v):
    bsz = q16.shape[0]
    page_rows = PAGE_SIZE * N_KV_HEADS
    page_spec = lambda i: pl.BlockSpec((1, page_rows, HEAD_DIM),
                                       lambda b, s, pt, i=i: (pt[b, s * ATTN_PAGES_PER_STEP + i], 0, 0))
    per_b = lambda shape: pl.BlockSpec((1,) + shape, lambda b, s, pt: (b,) + (0,) * len(shape))
    pages = [page_spec(i) for i in range(ATTN_PAGES_PER_STEP)]
    grid_spec = pltpu.PrefetchScalarGridSpec(
        num_scalar_prefetch=1,
        grid=(bsz, N_PAGES // ATTN_PAGES_PER_STEP),
        in_specs=[per_b((N_KV_HEADS, SAMPLE_ROWS, HEAD_DIM)), per_b((N_KEY_PAGES, SAMPLE_ROWS, PAGE_SIZE)),
                  per_b((SAMPLE_ROWS, LANES)), per_b((SAMPLE_ROWS, LANES)),
                  per_b((page_rows, HEAD_DIM)), per_b((page_rows, HEAD_DIM))] + pages + pages,
        out_specs=per_b((N_KV_HEADS, SAMPLE_ROWS, HEAD_DIM)),
        scratch_shapes=[pltpu.VMEM((N_KV_HEADS, SAMPLE_ROWS, 1), F32),
                        pltpu.VMEM((N_KV_HEADS, SAMPLE_ROWS, 1), F32),
                        pltpu.VMEM((N_KV_HEADS, SAMPLE_ROWS, HEAD_DIM), F32)])
    return pl.pallas_call(
        _sample_attn_kernel,
        out_shape=jax.ShapeDtypeStruct((bsz, N_KV_HEADS, SAMPLE_ROWS, HEAD_DIM), F32),
        grid_spec=grid_spec,
        compiler_params=_params("parallel", "arbitrary"),
        name="dsa_sample_attn",
    )(page_table, q16, scores, thr, cut, k_new_pad, v_new_pad,
      *([pool_k] * ATTN_PAGES_PER_STEP), *([pool_v] * ATTN_PAGES_PER_STEP))


def _pad_cols(w, width):
    return jnp.pad(w, ((0, 0), (0, width - w.shape[1])))


def _dsa_weight(w):
    sizes = (TOK_WIDTH, N_KV_HEADS * HEAD_DIM, N_KV_HEADS * HEAD_DIM, IDX_HEADS * IDX_DIM, IDX_DIM, IDX_HEADS, MEM_WIDTH)
    offs = [0]
    for s in sizes:
        offs.append(offs[-1] + s)
    q, k, v, iq, ik, iw, mq = [w[:, offs[i]:offs[i + 1]] for i in range(len(sizes))]
    return jnp.concatenate([q, k, v, iq, ik, ik, _pad_cols(iw, LANES), mq], axis=1).astype(BF16)


def kernel(x_prompt, x_sample, cache_k, cache_v, cache_idx_k, cache_mem_k, cache_mem_v, state_hgrn, state_ffn_conv, page_table, mem_prompt, norm_mix, norm_ffn, w_in_hgrn, hgrn_lb_logits, hgrn_out_norm, w_in_dsa, dsa_q_norm, dsa_k_norm, idx_k_norm, w_mem_kv, mem_q_norm, mem_k_norm, w_out, w_ffn_gate, w_ffn_up, ffn_conv_w, ffn_conv_b, w_ffn_down):
    bp, seq, _ = x_prompt.shape
    bs, t_s, _ = x_sample.shape
    assert t_s == SAMPLE_T and seq % KEY_BLOCK == 0
    n_p = bp * seq
    n_s = bs * SAMPLE_T_PAD
    kvw = N_KV_HEADS * HEAD_DIM
    lower_bounds = jnp.cumsum(jax.nn.softmax(hgrn_lb_logits.astype(F32), axis=0), axis=0)

    xp = x_prompt
    xs = jnp.pad(x_sample, ((0, 0), (0, SAMPLE_T_PAD - t_s), (0, 0)))
    mem2d = mem_prompt.reshape(bp * N_MEM, D_MODEL)
    tabs_p = _rope_tables(jnp.arange(seq))
    pos_s = PAST_LEN + jnp.arange(SAMPLE_T_PAD)
    tabs_s = tuple(jnp.tile(tb, (bs, 1)) for tb in _rope_tables(pos_s))
    row_t = jnp.arange(n_s) % SAMPLE_T_PAD

    hgrn_p, hgrn_s = [], []
    kp_l, vp_l, ikp_l, ks_l, vs_l, iks_l = [], [], [], [], [], []
    mk_l, mv_l, cvp_l, cvs_l = [], [], [], []
    for layer in range(DEPTH):
        j = layer // 2
        if layer % 2 == 0:
            w_in = w_in_hgrn[j].astype(BF16)
            mem_col = HGRN_MEM0 // MEM_WIDTH
        else:
            w_in = _dsa_weight(w_in_dsa[j])
            mem_col = DSA_MEM0 // MEM_WIDTH
        cols = w_in.shape[1]
        zp = _norm_proj(xp.reshape(n_p, D_MODEL), norm_mix[layer], w_in, 256).reshape(bp, seq, cols)
        zs = _norm_proj(xs.reshape(n_s, D_MODEL), norm_mix[layer], w_in, n_s).reshape(bs, SAMPLE_T_PAD, cols)
        if layer % 2 == 0:
            s0 = jnp.zeros((bp, HGRN_HEADS, HGRN_DK, HGRN_DK), F32)
            tp, sp = _hgrn(zp, lower_bounds[j], hgrn_out_norm[j], s0, HGRN_CHUNK, HGRN_SUB, 512, HGRN_CHUNK)
            ts, ss = _hgrn(zs, lower_bounds[j], hgrn_out_norm[j], state_hgrn[j], SAMPLE_T_PAD, SAMPLE_T_PAD,
                           SAMPLE_T_PAD, math.gcd(t_s, HGRN_CHUNK))
            hgrn_p.append(sp.astype(state_hgrn.dtype))
            hgrn_s.append(ss.astype(state_hgrn.dtype))
        else:
            qp, kp, kpb, vp, vtp, iqp, ikp, ikpb, iwtp = _dsa_prep(
                zp, tabs_p, dsa_q_norm[j], dsa_k_norm[j], idx_k_norm[j], KEY_BLOCK)
            tp = _dsa_prompt(qp, iqp, iwtp, ikpb, kpb, vtp)
            kp_l.append(kp.reshape(bp, seq, N_KV_HEADS, HEAD_DIM))
            vp_l.append(vp.reshape(bp, seq, N_KV_HEADS, HEAD_DIM))
            ikp_l.append(ikp[..., :IDX_DIM])

            qs, kn, _, vn, _, iqs, ikn, _, iwts = _dsa_prep(
                zs.reshape(1, n_s, cols), tabs_s, dsa_q_norm[j], dsa_k_norm[j], idx_k_norm[j], n_s)
            sel_t = lambda a: a.reshape(bs, SAMPLE_T_PAD, -1)[:, :t_s]
            kn, vn, ikn = sel_t(kn), sel_t(vn), sel_t(ikn)[..., :IDX_DIM]
            ks_l.append(kn.reshape(bs, t_s, N_KV_HEADS, HEAD_DIM))
            vs_l.append(vn.reshape(bs, t_s, N_KV_HEADS, HEAD_DIM))
            iks_l.append(ikn)
            iq4 = jnp.tile(sel_t(iqs).reshape(bs, t_s * IDX_HEADS, IDX_DIM), (1, 1, 4))
            iw_col = sel_t(iwts[0].T).reshape(bs, t_s * IDX_HEADS, 1)
            ikn_t = jnp.pad(ikn.transpose(0, 2, 1), ((0, 0), (0, 0), (0, PAGE_SIZE - t_s)))
            scores = _sample_scores(page_table, iq4, iw_col, ikn_t, cache_idx_k[j].transpose(0, 2, 1))
            per_query = scores[:, :, :t_s, :].transpose(0, 2, 1, 3).reshape(bs * t_s, N_KEY_PAGES * PAGE_SIZE)
            thr, cut = _sample_select(per_query)
            rep = lambda a: jnp.tile(a.reshape(bs, t_s, LANES), (1, SAMPLE_ROWS // t_s, 1))
            thr, cut = rep(thr), rep(cut)
            q16 = sel_t(qs).reshape(bs, t_s, N_KV_HEADS, Q_PER_KV, HEAD_DIM).transpose(0, 2, 3, 1, 4)
            q16 = jnp.pad(q16.reshape(bs, N_KV_HEADS, Q_PER_KV * t_s, HEAD_DIM),
                          ((0, 0), (0, 0), (0, SAMPLE_ROWS - Q_PER_KV * t_s), (0, 0)))
            page_rows = lambda a: a.reshape(a.shape[0], -1, HEAD_DIM)
            pad_page = lambda a: jnp.pad(page_rows(a), ((0, 0), (0, (PAGE_SIZE - t_s) * N_KV_HEADS), (0, 0)))
            o16 = _sample_attn(page_table, q16, scores, thr, cut, pad_page(kn), pad_page(vn),
                               page_rows(cache_k[j]), page_rows(cache_v[j]))
            o = o16[:, :, :Q_PER_KV * t_s].reshape(bs, N_KV_HEADS, Q_PER_KV, t_s, HEAD_DIM)
            ts = jnp.pad(o.transpose(0, 3, 1, 2, 4).reshape(bs, t_s, TOK_WIDTH),
                         ((0, 0), (0, SAMPLE_T_PAD - t_s), (0, 0)))

        mkp, mvp = _mem_kv(mem2d, w_mem_kv[layer].astype(BF16), mem_k_norm[layer])
        mk_l.append(mkp.reshape(bp, N_MEM, MEM_HEADS, MEM_HD))
        mv_l.append(mvp.reshape(bp, N_MEM, MEM_HEADS, MEM_HD))
        cp = _mem_attn(zp, mem_col, mem_q_norm[layer], mkp.reshape(bp, N_MEM, MEM_WIDTH),
                       mvp.reshape(bp, N_MEM, MEM_WIDTH), 512)
        cs = _mem_attn(zs, mem_col, mem_q_norm[layer], cache_mem_k[layer].reshape(bs, N_MEM, MEM_WIDTH),
                       cache_mem_v[layer].reshape(bs, N_MEM, MEM_WIDTH), SAMPLE_T_PAD)
        w_o = w_out[layer].astype(BF16)
        xp = _out_proj(xp.reshape(n_p, D_MODEL), tp.reshape(n_p, TOK_WIDTH), cp.reshape(n_p, MEM_WIDTH),
                       w_o, 512).reshape(bp, seq, D_MODEL)
        xs2 = _out_proj(xs.reshape(n_s, D_MODEL), ts.reshape(n_s, TOK_WIDTH), cs.reshape(n_s, MEM_WIDTH), w_o, n_s)

        wg, wu, wd = (w.astype(BF16) for w in (w_ffn_gate[layer], w_ffn_up[layer], w_ffn_down[layer]))
        ffn_w = (norm_ffn[layer], wg, wu, ffn_conv_w[layer], ffn_conv_b[layer], wd)
        xp, u_last = _ffn_long(xp, *ffn_w, jnp.zeros((bp, CONV_W - 1, D_FF), F32), 256)
        cvp_l.append(u_last[:, SUBLANES - (CONV_W - 1):, :])
        st = state_ffn_conv[layer]
        zrow = jnp.zeros((bs, SAMPLE_T_PAD - 1, D_FF), F32)
        p1 = jnp.concatenate([st[:, 1:2], zrow], axis=1).reshape(n_s, D_FF)
        p2 = jnp.concatenate([st, zrow[:, 1:]], axis=1).reshape(n_s, D_FF)
        xs2, u_s = _ffn_short(xs2, *ffn_w, p1, p2, SAMPLE_T_PAD)
        cvs_l.append(u_s.reshape(bs, SAMPLE_T_PAD, D_FF)[:, t_s - (CONV_W - 1):t_s])
        xs = jnp.where((row_t < t_s)[:, None], xs2, 0.0).reshape(bs, SAMPLE_T_PAD, D_MODEL)

    return (xp, xs[:, :t_s],
            jnp.stack(hgrn_p), jnp.stack(hgrn_s),
            jnp.stack(kp_l), jnp.stack(vp_l), jnp.stack(ikp_l),
            jnp.stack(ks_l), jnp.stack(vs_l), jnp.stack(iks_l),
            jnp.stack(mk_l), jnp.stack(mv_l),
            jnp.stack(cvp_l), jnp.stack(cvs_l))
```

```python
import functools
import math

import jax
import jax.numpy as jnp
from jax import lax
from jax.experimental import pallas as pl
from jax.experimental.pallas import tpu as pltpu

D_MODEL = 1024
DEPTH = 2
PAST_LEN = 8192
PAGE_SIZE = 128
MEM_WIDTH = 256
TOK_WIDTH = 768
MEM_HEADS = 4
MEM_HD = 64
N_MEM = 256
HGRN_DK = 128
HGRN_HEADS = 6
HGRN_CHUNK = 32
HEAD_DIM = 128
N_Q_HEADS = 6
N_KV_HEADS = 2
Q_PER_KV = N_Q_HEADS // N_KV_HEADS
IDX_HEADS = 8
IDX_DIM = 64
TOPK = 256
ROPE_THETA = 10000.0
D_FF = 2816
CONV_W = 3
EPS = 1e-6

LANES = 128
SUBLANES = 8
VMEM_LIMIT_BYTES = 56 * 1024 * 1024
FF_CHUNK = 256
HGRN_SUB = 256
KEY_BLOCK = 512
Q_TILE = 128
SAMPLE_T_PAD = 8
SCORE_PAGES_PER_STEP = 32
ATTN_PAGES_PER_STEP = 16
NEG_BIG = -1e30
QK_SCALE_LOG2E = HEAD_DIM ** -0.5 * math.log2(math.e)
F32 = jnp.float32
BF16 = jnp.bfloat16
I32 = jnp.int32
INT_MIN = -2 ** 31

DSA_Q0, DSA_K0, DSA_V0, DSA_IQ0, DSA_IK0, DSA_IW0, DSA_MEM0, DSA_COLS_PAD = 0, 768, 1024, 1280, 1792, 1920, 2048, 2304
HGRN_MEM0 = 4 * TOK_WIDTH


def _params(*sem):
    return pltpu.CompilerParams(dimension_semantics=sem, vmem_limit_bytes=VMEM_LIMIT_BYTES)


def _rms(x, gain):
    ms = jnp.mean(x * x, axis=-1, keepdims=True)
    return x * lax.rsqrt(ms + EPS) * gain


def _split_bf16(x):
    hi = x.astype(BF16)
    lo = (x - hi.astype(F32)).astype(BF16)
    return hi, lo


def _group_mean_sq(x, group):
    n = x.shape[-1]
    r = lax.broadcasted_iota(I32, (n, n), 0) // group
    c = lax.broadcasted_iota(I32, (n, n), 1) // group
    ones = jnp.where(r == c, 1.0, 0.0).astype(BF16)
    hi, lo = _split_bf16(x * x)
    s = jnp.dot(hi, ones, preferred_element_type=F32) + jnp.dot(lo, ones, preferred_element_type=F32)
    return s * (1.0 / group)


def _dot_nt(a, b):
    return lax.dot_general(a, b, (((1,), (1,)), ((), ())), preferred_element_type=F32)


def _dot_tn(a, b):
    return lax.dot_general(a, b, (((0,), (0,)), ((), ())), preferred_element_type=F32)


def _silu(x):
    return x * jax.nn.sigmoid(x)


def _norm_proj_kernel(x_ref, g_ref, w_ref, o_ref):
    h = _rms(x_ref[...], g_ref[...]).astype(BF16)
    o_ref[...] = jnp.dot(h, w_ref[...], preferred_element_type=F32)


def _norm_proj(x2d, gain, w_bf16, tm):
    n, d = x2d.shape
    c = w_bf16.shape[1]
    return pl.pallas_call(
        _norm_proj_kernel,
        out_shape=jax.ShapeDtypeStruct((n, c), F32),
        grid=(n // tm,),
        in_specs=[pl.BlockSpec((tm, d), lambda i: (i, 0)),
                  pl.BlockSpec((1, d), lambda i: (0, 0)),
                  pl.BlockSpec((d, c), lambda i: (0, 0))],
        out_specs=pl.BlockSpec((tm, c), lambda i: (i, 0)),
        compiler_params=_params("parallel"),
        name="norm_proj",
    )(x2d, gain.reshape(1, d), w_bf16)


def _hgrn_kernel(zq_ref, zf_ref, zi_ref, zg_ref, lb_ref, on_ref, s0_ref, o_ref, so_ref, st_ref,
                 *, chunk, sub, n_sub, valid_rows):
    step = pl.program_id(1)

    @pl.when(step == 0)
    def _():
        for h in range(HGRN_HEADS):
            st_ref[h] = s0_ref[0, h].T

    row = lax.broadcasted_iota(I32, (sub, sub), 0)
    col = lax.broadcasted_iota(I32, (sub, sub), 1)
    same_chunk = (row // chunk) == (col // chunk)
    causal = jnp.logical_and(same_chunk, row >= col)
    sum_lhs = jnp.concatenate([jnp.where(causal, 1.0, 0.0), jnp.where(same_chunk, 1.0, 0.0)], axis=0).astype(BF16)
    row_id = lax.broadcasted_iota(I32, (sub, HGRN_DK), 0)
    row_ok = row_id % chunk < valid_rows
    n_c = sub // chunk
    in_chunk = [row_id // chunk == c for c in range(n_c)]

    def body(si, carry):
        rows = pl.ds(pl.multiple_of(si * sub, sub), sub)
        for h in range(HGRN_HEADS):
            sl = slice(h * HGRN_DK, (h + 1) * HGRN_DK)
            lb = lb_ref[:, sl]
            q = _silu(zq_ref[0, rows, sl])
            f = lb + (1.0 - lb) * jax.nn.sigmoid(zf_ref[0, rows, sl])
            if valid_rows < chunk:
                f = jnp.where(row_ok, f, 1.0)
            k = 1.0 - f
            vb = zi_ref[0, rows, sl].astype(BF16)
            g_hi, g_lo = _split_bf16(jnp.log(f))
            sums = jnp.dot(sum_lhs, jnp.concatenate([g_hi, g_lo], axis=1), preferred_element_type=F32)
            b = sums[:sub, :HGRN_DK] + sums[:sub, HGRN_DK:]
            b_chunk = sums[sub:, :HGRN_DK] + sums[sub:, HGRN_DK:]
            q_dec = q * jnp.exp(b)
            k_dec = (k * jnp.exp(-b)).astype(BF16)
            k_rem = k * jnp.exp(b_chunk - b)
            chunk_decay = jnp.exp(b_chunk)
            scores = jnp.where(causal, _dot_nt(q_dec.astype(BF16), k_dec), 0.0)
            o = jnp.dot(scores.astype(BF16), vb, preferred_element_type=F32)
            spread = lambda a: jnp.concatenate([jnp.where(in_chunk[c], a, 0.0) for c in range(n_c)],
                                               axis=1).astype(BF16)
            incr = _dot_tn(vb, spread(k_rem))
            st = st_ref[h]
            states = []
            for c in range(n_c):
                states.append(st.astype(BF16))
                st = st * chunk_decay[c * chunk:c * chunk + 1] + incr[:, c * HGRN_DK:(c + 1) * HGRN_DK]
            st_ref[h] = st
            o = o + _dot_nt(spread(q_dec), jnp.concatenate(states, axis=1))
            o_ref[0, rows, sl] = _rms(o, on_ref[...]) * _silu(zg_ref[0, rows, sl])
        return carry

    lax.fori_loop(0, n_sub, body, 0)

    @pl.when(step == pl.num_programs(1) - 1)
    def _():
        for h in range(HGRN_HEADS):
            so_ref[0, h] = st_ref[h].T


def _hgrn(z3d, lb, out_norm, s0, chunk, sub, t_step, valid_rows):
    bsz, t, _ = z3d.shape
    col = lambda j: pl.BlockSpec((1, t_step, TOK_WIDTH), lambda b, i, j=j: (b, i, j))
    state_spec = pl.BlockSpec((1, HGRN_HEADS, HGRN_DK, HGRN_DK), lambda b, i: (b, 0, 0, 0))
    kern = functools.partial(_hgrn_kernel, chunk=chunk, sub=sub, n_sub=t_step // sub, valid_rows=valid_rows)
    return pl.pallas_call(
        kern,
        out_shape=(jax.ShapeDtypeStruct((bsz, t, TOK_WIDTH), F32),
                   jax.ShapeDtypeStruct((bsz, HGRN_HEADS, HGRN_DK, HGRN_DK), F32)),
        grid=(bsz, t // t_step),
        in_specs=[col(0), col(1), col(2), col(3),
                  pl.BlockSpec((1, TOK_WIDTH), lambda b, i: (0, 0)),
                  pl.BlockSpec((1, HGRN_DK), lambda b, i: (0, 0)),
                  state_spec],
        out_specs=(pl.BlockSpec((1, t_step, TOK_WIDTH), lambda b, i: (b, i, 0)), state_spec),
        scratch_shapes=[pltpu.VMEM((HGRN_HEADS, HGRN_DK, HGRN_DK), F32)],
        compiler_params=_params("parallel", "arbitrary"),
        name="hgrn2",
    )(z3d, z3d, z3d, z3d, lb.reshape(1, TOK_WIDTH), out_norm.reshape(1, HGRN_DK), s0)


def _mem_kv_kernel(m_ref, w_ref, g_ref, k_ref, v_ref):
    kv = jnp.dot(m_ref[...].astype(BF16), w_ref[...], preferred_element_type=F32)
    k = kv[:, :MEM_WIDTH]
    k_ref[...] = k * lax.rsqrt(_group_mean_sq(k, MEM_HD) + EPS) * g_ref[...]
    v_ref[...] = kv[:, MEM_WIDTH:]


def _mem_kv(mem2d, w_bf16, k_gain):
    n = mem2d.shape[0]
    out = jax.ShapeDtypeStruct((n, MEM_WIDTH), F32)
    return pl.pallas_call(
        _mem_kv_kernel,
        out_shape=(out, out),
        grid=(n // N_MEM,),
        in_specs=[pl.BlockSpec((N_MEM, D_MODEL), lambda i: (i, 0)),
                  pl.BlockSpec((D_MODEL, 2 * MEM_WIDTH), lambda i: (0, 0)),
                  pl.BlockSpec((1, MEM_WIDTH), lambda i: (0, 0))],
        out_specs=(pl.BlockSpec((N_MEM, MEM_WIDTH), lambda i: (i, 0)),
                   pl.BlockSpec((N_MEM, MEM_WIDTH), lambda i: (i, 0))),
        compiler_params=_params("parallel"),
        name="mem_kv",
    )(mem2d, w_bf16, jnp.tile(k_gain, MEM_HEADS).reshape(1, MEM_WIDTH))


def _mem_attn_kernel(zq_ref, g_ref, mk_ref, mv_ref, o_ref):
    q = zq_ref[0]
    qn = q * lax.rsqrt(_group_mean_sq(q, MEM_HD) + EPS) * g_ref[...]
    mk = mk_ref[0].astype(BF16)
    mv = mv_ref[0].astype(BF16)
    head_of_lane = lax.broadcasted_iota(I32, qn.shape, 1) // MEM_HD
    out = jnp.zeros(qn.shape, F32)
    for h in range(MEM_HEADS):
        mine = head_of_lane == h
        s = _dot_nt(jnp.where(mine, qn, 0.0).astype(BF16), mk) * (MEM_HD ** -0.5)
        p = jnp.exp(s - jnp.max(s, axis=-1, keepdims=True))
        oh = jnp.dot(p.astype(BF16), mv, preferred_element_type=F32) / jnp.sum(p, axis=-1, keepdims=True)
        out = jnp.where(mine, oh, out)
    o_ref[0] = out


def _mem_attn(z3d, q_col_block, q_gain, mk, mv, tm):
    bsz, t, _ = z3d.shape
    return pl.pallas_call(
        _mem_attn_kernel,
        out_shape=jax.ShapeDtypeStruct((bsz, t, MEM_WIDTH), F32),
        grid=(bsz, t // tm),
        in_specs=[pl.BlockSpec((1, tm, MEM_WIDTH), lambda b, i: (b, i, q_col_block)),
                  pl.BlockSpec((1, MEM_WIDTH), lambda b, i: (0, 0)),
                  pl.BlockSpec((1, N_MEM, MEM_WIDTH), lambda b, i: (b, 0, 0)),
                  pl.BlockSpec((1, N_MEM, MEM_WIDTH), lambda b, i: (b, 0, 0))],
        out_specs=pl.BlockSpec((1, tm, MEM_WIDTH), lambda b, i: (b, i, 0)),
        compiler_params=_params("parallel", "parallel"),
        name="mem_attn",
    )(z3d, jnp.tile(q_gain, MEM_HEADS).reshape(1, MEM_WIDTH), mk, mv)


def _out_proj_kernel(x_ref, t_ref, c_ref, wt_ref, wc_ref, o_ref):
    o_ref[...] = (x_ref[...]
                  + jnp.dot(t_ref[...].astype(BF16), wt_ref[...], preferred_element_type=F32)
                  + jnp.dot(c_ref[...].astype(BF16), wc_ref[...], preferred_element_type=F32))


def _out_proj(x2d, t2d, c2d, w_bf16, tm):
    n = x2d.shape[0]
    row = lambda w: pl.BlockSpec((tm, w), lambda i: (i, 0))
    return pl.pallas_call(
        _out_proj_kernel,
        out_shape=jax.ShapeDtypeStruct((n, D_MODEL), F32),
        grid=(n // tm,),
        in_specs=[row(D_MODEL), row(TOK_WIDTH), row(MEM_WIDTH),
                  pl.BlockSpec((TOK_WIDTH, D_MODEL), lambda i: (0, 0)),
                  pl.BlockSpec((MEM_WIDTH, D_MODEL), lambda i: (0, 0))],
        out_specs=row(D_MODEL),
        compiler_params=_params("parallel"),
        name="out_proj",
    )(x2d, t2d, c2d, w_bf16[:TOK_WIDTH], w_bf16[TOK_WIDTH:])


def _ffn_kernel(x_ref, g_ref, wg_ref, wu_ref, cw_ref, cb_ref, wd_ref, p1_ref, p2_ref, y_ref, u_ref, carry_ref,
                *, seq_rows, carry):
    tm = x_ref.shape[0]
    x = x_ref[...]
    h = _rms(x, g_ref[...]).astype(BF16)
    t_in_seq = lax.broadcasted_iota(I32, (tm, FF_CHUNK), 0) % seq_rows
    first_block = pl.program_id(1) == 0 if carry else None
    n_chunks = D_FF // FF_CHUNK

    def project(c):
        cs = slice(c * FF_CHUNK, (c + 1) * FF_CHUNK)
        return (jnp.dot(h, wg_ref[:, cs], preferred_element_type=F32),
                jnp.dot(h, wu_ref[:, cs], preferred_element_type=F32))

    acc = x
    ahead = project(0)
    for c in range(n_chunks):
        cs = slice(c * FF_CHUNK, (c + 1) * FF_CHUNK)
        u, up = ahead
        if c + 1 < n_chunks:
            ahead = project(c + 1)
        u1 = pltpu.roll(u, 1, 0)
        u2 = pltpu.roll(u, 2, 0)
        if carry:
            prev = carry_ref[:, cs]
            start1 = jnp.where(first_block, p1_ref[0:1, cs], prev[7:8, :])
            start2a = jnp.where(first_block, p2_ref[0:1, cs], prev[6:7, :])
            start2b = jnp.where(first_block, p2_ref[1:2, cs], prev[7:8, :])
            u1 = jnp.where(t_in_seq == 0, start1, u1)
            u2 = jnp.where(t_in_seq == 0, start2a, jnp.where(t_in_seq == 1, start2b, u2))
            carry_ref[:, cs] = u[tm - SUBLANES:, :]
        else:
            u1 = jnp.where(t_in_seq == 0, p1_ref[:, cs], u1)
            u2 = jnp.where(t_in_seq < 2, p2_ref[:, cs], u2)
        conv = cb_ref[:, cs] + u2 * cw_ref[0:1, cs] + u1 * cw_ref[1:2, cs] + u * cw_ref[2:3, cs]
        a = (_silu(conv) * up).astype(BF16)
        acc = acc + jnp.dot(a, wd_ref[cs, :], preferred_element_type=F32)
        if carry:
            u_ref[0, :, cs] = u[tm - SUBLANES:, :]
        else:
            u_ref[:, cs] = u
    y_ref[...] = acc


def _ffn_weights_specs(idx):
    once = pl.Buffered(1)
    return [pl.BlockSpec((1, D_MODEL), idx),
            pl.BlockSpec((D_MODEL, D_FF), idx, pipeline_mode=once),
            pl.BlockSpec((D_MODEL, D_FF), idx, pipeline_mode=once),
            pl.BlockSpec((CONV_W, D_FF), idx),
            pl.BlockSpec((1, D_FF), idx),
            pl.BlockSpec((D_FF, D_MODEL), idx, pipeline_mode=once)]


def _ffn_long(x3d, gain, wg, wu, cw, cb, wd, state, tm):
    bsz, t, _ = x3d.shape
    idx = lambda b, i: (0, 0)
    kern = functools.partial(_ffn_kernel, seq_rows=t, carry=True)
    x2d = x3d.reshape(bsz * t, D_MODEL)
    nb = t // tm
    p1 = state[:, 1:2, :]
    y, u_last = pl.pallas_call(
        kern,
        out_shape=(jax.ShapeDtypeStruct((bsz * t, D_MODEL), F32),
                   jax.ShapeDtypeStruct((bsz, SUBLANES, D_FF), F32)),
        grid=(bsz, nb),
        in_specs=[pl.BlockSpec((tm, D_MODEL), lambda b, i: (b * nb + i, 0))] + _ffn_weights_specs(idx)
                 + [pl.BlockSpec((None, 1, D_FF), lambda b, i: (b, 0, 0)),
                    pl.BlockSpec((None, 2, D_FF), lambda b, i: (b, 0, 0))],
        out_specs=(pl.BlockSpec((tm, D_MODEL), lambda b, i: (b * nb + i, 0)),
                   pl.BlockSpec((1, SUBLANES, D_FF), lambda b, i: (b, 0, 0))),
        scratch_shapes=[pltpu.VMEM((SUBLANES, D_FF), F32)],
        compiler_params=_params("parallel", "arbitrary"),
        name="conv_ffn_long",
    )(x2d, gain.reshape(1, D_MODEL), wg, wu, cw, cb.reshape(1, D_FF), wd, p1, state)
    return y.reshape(bsz, t, D_MODEL), u_last


def _ffn_short(x2d, gain, wg, wu, cw, cb, wd, p1, p2, seq_rows):
    n = x2d.shape[0]
    idx = lambda i: (0, 0)
    kern = functools.partial(_ffn_kernel, seq_rows=seq_rows, carry=False)
    row = lambda w: pl.BlockSpec((n, w), lambda i: (0, 0))
    return pl.pallas_call(
        kern,
        out_shape=(jax.ShapeDtypeStruct((n, D_MODEL), F32), jax.ShapeDtypeStruct((n, D_FF), F32)),
        grid=(1,),
        in_specs=[row(D_MODEL)] + _ffn_weights_specs(idx) + [row(D_FF), row(D_FF)],
        out_specs=(row(D_MODEL), row(D_FF)),
        scratch_shapes=[pltpu.VMEM((SUBLANES, D_FF), F32)],
        compiler_params=_params("arbitrary"),
        name="conv_ffn_short",
    )(x2d, gain.reshape(1, D_MODEL), wg, wu, cw, cb.reshape(1, D_FF), wd, p1, p2)


def _rope128(x, cos, sin_signed):
    return x * cos + pltpu.roll(x, HEAD_DIM // 2, 1) * sin_signed


def _rope64_pairs(x, cos, sin_lo, sin_hi):
    return (x * cos + pltpu.roll(x, LANES - IDX_DIM // 2, 1) * sin_lo
            + pltpu.roll(x, IDX_DIM // 2, 1) * sin_hi)


def _dsa_prep_kernel(z_ref, c128_ref, s128_ref, c64_ref, s64lo_ref, s64hi_ref, qg_ref, kg_ref, ikg_ref,
                     q_ref, k_ref, kb_ref, v_ref, vt_ref, iq_ref, ik_ref, ikb_ref, iwt_ref):
    c128, s128 = c128_ref[...], s128_ref[...]
    c64, s64lo, s64hi = c64_ref[...], s64lo_ref[...], s64hi_ref[...]
    for h in range(N_Q_HEADS):
        sl = slice(DSA_Q0 + h * HEAD_DIM, DSA_Q0 + (h + 1) * HEAD_DIM)
        q = _rope128(_rms(z_ref[0, :, sl], qg_ref[...]), c128, s128)
        q_ref[0, :, h * HEAD_DIM:(h + 1) * HEAD_DIM] = (q * QK_SCALE_LOG2E).astype(BF16)
    for h in range(N_KV_HEADS):
        sl = slice(DSA_K0 + h * HEAD_DIM, DSA_K0 + (h + 1) * HEAD_DIM)
        k = _rope128(_rms(z_ref[0, :, sl], kg_ref[...]), c128, s128)
        k_ref[0, :, h * HEAD_DIM:(h + 1) * HEAD_DIM] = k
        kb_ref[0, :, h * HEAD_DIM:(h + 1) * HEAD_DIM] = k.astype(BF16)
    v = z_ref[0, :, DSA_V0:DSA_V0 + N_KV_HEADS * HEAD_DIM]
    v_ref[0] = v
    vt_ref[0, 0] = v.T.astype(BF16)
    for p in range(IDX_HEADS // 2):
        sl = slice(DSA_IQ0 + p * LANES, DSA_IQ0 + (p + 1) * LANES)
        iq_ref[0, :, p * LANES:(p + 1) * LANES] = _rope64_pairs(z_ref[0, :, sl], c64, s64lo, s64hi)
    ik = _rope64_pairs(_rms(z_ref[0, :, DSA_IK0:DSA_IK0 + LANES], ikg_ref[...]), c64, s64lo, s64hi)
    ik_ref[0] = ik
    ik_hi, ik_lo = _split_bf16(ik)
    hi_lo = jnp.where(lax.broadcasted_iota(I32, ik.shape, 1) < IDX_DIM, ik_hi, ik_lo)
    ikb_ref[0, :, :LANES] = hi_lo
    ikb_ref[0, :, LANES:] = hi_lo
    iw = z_ref[0, :, DSA_IW0:DSA_IW0 + LANES] * (IDX_HEADS ** -0.5 * IDX_DIM ** -0.5)
    iwt_ref[0] = iw.T[:IDX_HEADS, :]


def _dsa_prep(z3d, tabs, q_gain, k_gain, ik_gain, tm):
    bsz, t, _ = z3d.shape
    nb = t // tm
    tab = pl.BlockSpec((tm, LANES), lambda b, i: (i, 0))
    gain = pl.BlockSpec((1, LANES), lambda b, i: (0, 0))
    rows = lambda w: pl.BlockSpec((1, tm, w), lambda b, i: (b, i, 0))
    kvw = N_KV_HEADS * HEAD_DIM
    sds = jax.ShapeDtypeStruct
    return pl.pallas_call(
        _dsa_prep_kernel,
        out_shape=(sds((bsz, t, TOK_WIDTH), BF16),
                   sds((bsz, t, kvw), F32), sds((bsz, t, kvw), BF16),
                   sds((bsz, t, kvw), F32), sds((bsz, nb, kvw, tm), BF16),
                   sds((bsz, t, IDX_HEADS * IDX_DIM), F32),
                   sds((bsz, t, LANES), F32),
                   sds((bsz, t, 2 * LANES), BF16),
                   sds((bsz, IDX_HEADS, t), F32)),
        grid=(bsz, nb),
        in_specs=[rows(DSA_COLS_PAD), tab, tab, tab, tab, tab, gain, gain, gain],
        out_specs=(rows(TOK_WIDTH), rows(kvw), rows(kvw), rows(kvw),
                   pl.BlockSpec((1, 1, kvw, tm), lambda b, i: (b, i, 0, 0)),
                   rows(IDX_HEADS * IDX_DIM), rows(LANES), rows(2 * LANES),
                   pl.BlockSpec((1, IDX_HEADS, tm), lambda b, i: (b, 0, i))),
        compiler_params=_params("parallel", "parallel"),
        name="dsa_prep",
    )(z3d, *tabs, q_gain.reshape(1, LANES), k_gain.reshape(1, LANES), jnp.tile(ik_gain, 2).reshape(1, LANES))


def _rope_tables(pos):
    pos = pos.astype(F32)[:, None]

    def cs(half):
        inv = ROPE_THETA ** (-jnp.arange(half, dtype=F32) / half)
        ang = pos * inv[None, :]
        return jnp.cos(ang), jnp.sin(ang)

    c, s = cs(HEAD_DIM // 2)
    c128 = jnp.concatenate([c, c], axis=1)
    s128 = jnp.concatenate([-s, s], axis=1)
    c, s = cs(IDX_DIM // 2)
    z = jnp.zeros_like(s)
    c64 = jnp.concatenate([c, c, c, c], axis=1)
    s64lo = jnp.concatenate([-s, z, -s, z], axis=1)
    s64hi = jnp.concatenate([z, s, z, s], axis=1)
    return c128, s128, c64, s64lo, s64hi


def _ordered_bits_to_float(u):
    key = u ^ INT_MIN
    bits = key ^ ((key >> 31) & 0x7FFFFFFF)
    return lax.bitcast_convert_type(bits, F32)


def _topk_threshold(count_ge, count_gt, count_eq_before, any_true, shape, idx_bits):
    def value_step(i, prefix):
        cand = prefix | jnp.left_shift(jnp.int32(1), 31 - i)
        below_neg_inf = jnp.logical_and(cand >= 0, cand < 0x00800000)
        ok = jnp.logical_or(count_ge(_ordered_bits_to_float(cand)) >= TOPK, below_neg_inf)
        return jnp.where(ok, cand, prefix)

    thr = _ordered_bits_to_float(lax.fori_loop(0, 32, value_step, jnp.zeros(shape, I32)))
    n_gt = count_gt(thr)
    need = TOPK - n_gt
    n_eq = count_ge(thr) - n_gt
    finite_thr = thr > -jnp.inf
    excess = jnp.logical_and(n_eq > need, finite_thr)

    def index_search():
        def index_step(i, cut):
            cand = cut | jnp.left_shift(jnp.int32(1), idx_bits - 1 - i)
            return jnp.where(count_eq_before(thr, cand) < need, cand, cut)
        return lax.fori_loop(0, idx_bits, index_step, jnp.zeros(shape, I32))

    everything = jnp.full(shape, 2 ** idx_bits - 1, I32)
    cut = lax.cond(any_true(excess), index_search, lambda: everything)
    return thr, jnp.where(finite_thr, cut, -1)


def _dsa_prompt_kernel(q_ref, iq_ref, iwt_ref, ik_ref, k_ref, vt_ref, prev_ref, o_ref,
                       score_ref, iqm_ref, qm_ref, acc_ref, *, n_kb):
    del prev_ref
    q0 = ((n_kb - 1) * (KEY_BLOCK // Q_TILE) + pl.program_id(1)) * Q_TILE
    lane = lax.broadcasted_iota(I32, (Q_TILE, LANES), 1)
    first_half = lane < IDX_DIM
    for p in range(IDX_HEADS // 2):
        pair = iq_ref[0, :, p * LANES:(p + 1) * LANES]
        swapped = pltpu.roll(pair, IDX_DIM, 1)
        for half in range(2):
            dup = jnp.where(first_half, pair, swapped) if half == 0 else jnp.where(first_half, swapped, pair)
            hi, lo = _split_bf16(dup)
            rows = slice(half * Q_TILE, (half + 1) * Q_TILE)
            iqm_ref[p, rows, :LANES] = hi
            iqm_ref[p, rows, LANES:] = lo
    for h in range(N_Q_HEADS):
        g = h // Q_PER_KV
        qh = q_ref[0, :, h * HEAD_DIM:(h + 1) * HEAD_DIM]
        rows = slice(h * Q_TILE, (h + 1) * Q_TILE)
        qm_ref[rows, g * HEAD_DIM:(g + 1) * HEAD_DIM] = qh
        qm_ref[rows, (1 - g) * HEAD_DIM:(2 - g) * HEAD_DIM] = jnp.zeros_like(qh)
    q_pos = q0 + lax.broadcasted_iota(I32, (1, Q_TILE), 1)
    key_row = lax.broadcasted_iota(I32, (KEY_BLOCK, 1), 0)

    blocks = [slice(kb * KEY_BLOCK, (kb + 1) * KEY_BLOCK) for kb in range(n_kb)]

    for kb in range(n_kb):
        ik = ik_ref[0, blocks[kb], :]
        acc = jnp.zeros((KEY_BLOCK, Q_TILE), F32)
        for p in range(IDX_HEADS // 2):
            d = _dot_nt(ik, iqm_ref[p])
            for half in range(2):
                h = 2 * p + half
                acc = acc + iwt_ref[0, h:h + 1, :] * jnp.maximum(d[:, half * Q_TILE:(half + 1) * Q_TILE], 0.0)
        if kb == n_kb - 1:
            acc = jnp.where(kb * KEY_BLOCK + key_row <= q_pos, acc, -jnp.inf)
        score_ref[blocks[kb], :] = acc

    def count(pred):
        part = jnp.zeros((SUBLANES, Q_TILE), I32)
        for kb in range(n_kb):
            hit = pred(score_ref[blocks[kb], :], kb * KEY_BLOCK + key_row).astype(I32)
            part = part + jnp.sum(hit.reshape(KEY_BLOCK // SUBLANES, SUBLANES, Q_TILE), axis=0)
        return jnp.sum(part, axis=0, keepdims=True)

    thr, cut = _topk_threshold(
        count_ge=lambda c: count(lambda s, idx: s >= c),
        count_gt=lambda c: count(lambda s, idx: s > c),
        count_eq_before=lambda t, j: count(lambda s, idx: jnp.logical_and(s == t, idx < j)),
        any_true=lambda m: jnp.max(m.astype(I32)) > 0,
        shape=(1, Q_TILE), idx_bits=13)

    m = [jnp.full((1, Q_TILE), NEG_BIG, F32)] * N_Q_HEADS
    l = [jnp.zeros((1, Q_TILE), F32)] * N_Q_HEADS
    for kb in range(n_kb):
        sc = score_ref[blocks[kb], :]
        sel = jnp.logical_or(sc > thr, jnp.logical_and(sc == thr, kb * KEY_BLOCK + key_row <= cut))
        k_blk = k_ref[0, blocks[kb], :]
        alphas, ps = [], []
        for j in range(N_Q_HEADS // 2):
            s2 = _dot_nt(k_blk, qm_ref[2 * j * Q_TILE:2 * (j + 1) * Q_TILE, :])
            for half in range(2):
                h = 2 * j + half
                s = jnp.where(sel, s2[:, half * Q_TILE:(half + 1) * Q_TILE], NEG_BIG)
                m_new = jnp.maximum(m[h], jnp.max(s, axis=0, keepdims=True))
                alpha = jnp.exp2(m[h] - m_new)
                p = jnp.exp2(s - m_new)
                l[h] = alpha * l[h] + jnp.sum(p, axis=0, keepdims=True)
                m[h] = m_new
                alphas.append(alpha)
                ps.append(p.astype(BF16))
        for g in range(N_KV_HEADS):
            vtg = vt_ref[0, kb, g * HEAD_DIM:(g + 1) * HEAD_DIM, :]
            pv = jnp.dot(vtg, jnp.concatenate(ps[g * Q_PER_KV:(g + 1) * Q_PER_KV], axis=1),
                         preferred_element_type=F32)
            for r in range(Q_PER_KV):
                h = g * Q_PER_KV + r
                part = pv[:, r * Q_TILE:(r + 1) * Q_TILE]
                acc_ref[h] = part if kb == 0 else alphas[h] * acc_ref[h] + part
    for h in range(N_Q_HEADS):
        o_ref[0, :, h * HEAD_DIM:(h + 1) * HEAD_DIM] = (acc_ref[h] / l[h]).T


def _dsa_prompt(q, iq, iwt, ikb, kb, vt):
    bsz, t, _ = q.shape
    kvw = N_KV_HEADS * HEAD_DIM
    tiles = KEY_BLOCK // Q_TILE
    out = jnp.zeros((bsz, t, TOK_WIDTH), F32)
    for n_kb in range(1, t // KEY_BLOCK + 1):
        tile = lambda b, i, n_kb=n_kb: (n_kb - 1) * tiles + i
        keys = lambda w, n_kb=n_kb: pl.BlockSpec((1, n_kb * KEY_BLOCK, w), lambda b, i: (b, 0, 0))
        out = pl.pallas_call(
            functools.partial(_dsa_prompt_kernel, n_kb=n_kb),
            out_shape=jax.ShapeDtypeStruct((bsz, t, TOK_WIDTH), F32),
            grid=(bsz, tiles),
            in_specs=[pl.BlockSpec((1, Q_TILE, TOK_WIDTH), lambda b, i, tile=tile: (b, tile(b, i), 0)),
                      pl.BlockSpec((1, Q_TILE, IDX_HEADS * IDX_DIM), lambda b, i, tile=tile: (b, tile(b, i), 0)),
                      pl.BlockSpec((1, IDX_HEADS, Q_TILE), lambda b, i, tile=tile: (b, 0, tile(b, i))),
                      keys(2 * LANES), keys(kvw),
                      pl.BlockSpec((1, n_kb, kvw, KEY_BLOCK), lambda b, i: (b, 0, 0, 0)),
                      pl.BlockSpec(memory_space=pl.ANY)],
            out_specs=pl.BlockSpec((1, Q_TILE, TOK_WIDTH), lambda b, i, tile=tile: (b, tile(b, i), 0)),
            scratch_shapes=[pltpu.VMEM((n_kb * KEY_BLOCK, Q_TILE), F32),
                            pltpu.VMEM((IDX_HEADS // 2, 2 * Q_TILE, 2 * LANES), BF16),
                            pltpu.VMEM((N_Q_HEADS * Q_TILE, N_KV_HEADS * HEAD_DIM), BF16),
                            pltpu.VMEM((N_Q_HEADS, HEAD_DIM, Q_TILE), F32)],
            input_output_aliases={6: 0},
            compiler_params=_params("parallel", "arbitrary"),
            name=f"dsa_prompt_{n_kb}",
        )(q, iq, iwt, ikb, kb, vt, out)
    return out


SAMPLE_ROWS = 16
SAMPLE_T = 4
N_PAGES = PAST_LEN // PAGE_SIZE
N_KEY_PAGES = N_PAGES + 1


def _sample_score_kernel(pt_ref, iq4_ref, iw_ref, ikn_ref, *rest):
    page_refs = rest[:SCORE_PAGES_PER_STEP]
    (score_ref,) = rest[SCORE_PAGES_PER_STEP:]
    step = pl.program_id(1)
    iq_hi, iq_lo = _split_bf16(iq4_ref[0])
    iq_parts = jnp.where(lax.broadcasted_iota(I32, iq_hi.shape, 1) < 2 * IDX_DIM, iq_hi, iq_lo)
    w = iw_ref[0]

    def page_scores(ik_t, page, allowed):
        k_hi, k_lo = _split_bf16(ik_t)
        d = jnp.dot(iq_parts, jnp.concatenate([k_hi, k_lo, k_hi, k_lo], axis=0), preferred_element_type=F32)
        sc = jnp.sum((jnp.maximum(d, 0.0) * w).reshape(SAMPLE_T, IDX_HEADS, PAGE_SIZE), axis=1)
        if allowed is not None:
            sc = jnp.where(allowed, sc, -jnp.inf)
        for r in range(SAMPLE_ROWS // SAMPLE_T):
            score_ref[0, page, r * SAMPLE_T:(r + 1) * SAMPLE_T, :] = sc

    for i in range(SCORE_PAGES_PER_STEP):
        page_scores(page_refs[i][0], step * SCORE_PAGES_PER_STEP + i, None)

    @pl.when(step == pl.num_programs(1) - 1)
    def _():
        new_idx = lax.broadcasted_iota(I32, (SAMPLE_T, PAGE_SIZE), 1)
        page_scores(ikn_ref[0], N_PAGES, new_idx <= lax.broadcasted_iota(I32, (SAMPLE_T, PAGE_SIZE), 0))


def _sample_scores(page_table, iq4, iw_col, ik_new_t, pool_ik_t):
    bsz = iq4.shape[0]
    page_spec = lambda i: pl.BlockSpec((1, IDX_DIM, PAGE_SIZE),
                                       lambda b, s, pt, i=i: (pt[b, s * SCORE_PAGES_PER_STEP + i], 0, 0))
    per_b = lambda shape: pl.BlockSpec((1,) + shape, lambda b, s, pt: (b,) + (0,) * len(shape))
    grid_spec = pltpu.PrefetchScalarGridSpec(
        num_scalar_prefetch=1,
        grid=(bsz, N_PAGES // SCORE_PAGES_PER_STEP),
        in_specs=[per_b((SAMPLE_T * IDX_HEADS, 4 * IDX_DIM)), per_b((SAMPLE_T * IDX_HEADS, 1)),
                  per_b((IDX_DIM, PAGE_SIZE))] + [page_spec(i) for i in range(SCORE_PAGES_PER_STEP)],
        out_specs=per_b((N_KEY_PAGES, SAMPLE_ROWS, PAGE_SIZE)))
    return pl.pallas_call(
        _sample_score_kernel,
        out_shape=jax.ShapeDtypeStruct((bsz, N_KEY_PAGES, SAMPLE_ROWS, PAGE_SIZE), F32),
        grid_spec=grid_spec,
        compiler_params=_params("parallel", "arbitrary"),
        name="dsa_sample_scores",
    )(page_table, iq4, iw_col, ik_new_t, *([pool_ik_t] * SCORE_PAGES_PER_STEP))


def _sample_select_kernel(score_ref, thr_ref, cut_ref):
    n_q, n_k = score_ref.shape
    key_idx = lax.broadcasted_iota(I32, (1, n_k), 1)

    def count(pred):
        return jnp.sum(pred(score_ref[...], key_idx).astype(I32), axis=-1, keepdims=True)

    thr, cut = _topk_threshold(
        count_ge=lambda c: count(lambda s, idx: s >= c),
        count_gt=lambda c: count(lambda s, idx: s > c),
        count_eq_before=lambda t, j: count(lambda s, idx: jnp.logical_and(s == t, idx < j)),
        any_true=lambda m: jnp.max(m.astype(I32)) > 0,
        shape=(n_q, 1), idx_bits=14)
    thr_ref[...] = jnp.broadcast_to(thr, thr_ref.shape)
    cut_ref[...] = jnp.broadcast_to(cut, cut_ref.shape)


def _sample_select(scores2d):
    n_q, n_k = scores2d.shape
    whole = lambda shape: pl.BlockSpec(shape, lambda i: (0, 0))
    return pl.pallas_call(
        _sample_select_kernel,
        out_shape=(jax.ShapeDtypeStruct((n_q, LANES), F32), jax.ShapeDtypeStruct((n_q, LANES), I32)),
        grid=(1,),
        in_specs=[whole((n_q, n_k))],
        out_specs=(whole((n_q, LANES)), whole((n_q, LANES))),
        compiler_params=_params("arbitrary"),
        name="dsa_sample_select",
    )(scores2d)


def _sample_attn_kernel(pt_ref, q_ref, sc_ref, thr_ref, cut_ref, kn_ref, vn_ref, *rest):
    k_pages = rest[:ATTN_PAGES_PER_STEP]
    v_pages = rest[ATTN_PAGES_PER_STEP:2 * ATTN_PAGES_PER_STEP]
    o_ref, m_ref, l_ref, acc_ref = rest[2 * ATTN_PAGES_PER_STEP:]
    step = pl.program_id(1)

    @pl.when(step == 0)
    def _():
        m_ref[...] = jnp.full(m_ref.shape, NEG_BIG, F32)
        l_ref[...] = jnp.zeros(l_ref.shape, F32)
        acc_ref[...] = jnp.zeros(acc_ref.shape, F32)

    thr = thr_ref[0]
    cut = cut_ref[0]
    lane = lax.broadcasted_iota(I32, (SAMPLE_ROWS, PAGE_SIZE), 1)

    def attend(pages):
        sels = []
        for page, _, _ in pages:
            sc = sc_ref[0, page]
            sels.append(jnp.logical_or(sc > thr, jnp.logical_and(sc == thr, page * PAGE_SIZE + lane <= cut)))
        cat = lambda parts, axis: parts[0] if len(parts) == 1 else jnp.concatenate(parts, axis=axis)
        sel = cat(sels, 1)
        for g in range(N_KV_HEADS):
            rows_g = pl.ds(g, PAGE_SIZE, stride=N_KV_HEADS)
            k_all = cat([k[0, rows_g, :].astype(BF16) for _, k, _ in pages], 0)
            v_all = cat([v[0, rows_g, :].astype(BF16) for _, _, v in pages], 0)
            s = jnp.where(sel, _dot_nt(q_ref[0, g], k_all), NEG_BIG)
            m_old = m_ref[g]
            m_new = jnp.maximum(m_old, jnp.max(s, axis=-1, keepdims=True))
            alpha = jnp.exp2(m_old - m_new)
            p = jnp.exp2(s - m_new)
            l_ref[g] = alpha * l_ref[g] + jnp.sum(p, axis=-1, keepdims=True)
            acc_ref[g] = alpha * acc_ref[g] + jnp.dot(p.astype(BF16), v_all, preferred_element_type=F32)
            m_ref[g] = m_new

    attend([(step * ATTN_PAGES_PER_STEP + i, k_pages[i], v_pages[i]) for i in range(ATTN_PAGES_PER_STEP)])

    @pl.when(step == pl.num_programs(1) - 1)
    def _():
        attend([(N_PAGES, kn_ref, vn_ref)])
        for g in range(N_KV_HEADS):
            o_ref[0, g] = acc_ref[g] / l_ref[g]


def _sample_attn(page_table, q16, scores, thr, cut, k_new_pad, v_new_pad, pool_k, pool_v):
    bsz = q16.shape[0]
    page_rows = PAGE_SIZE * N_KV_HEADS
    page_spec = lambda i: pl.BlockSpec((1, page_rows, HEAD_DIM),
                                       lambda b, s, pt, i=i: (pt[b, s * ATTN_PAGES_PER_STEP + i], 0, 0))
    per_b = lambda shape: pl.BlockSpec((1,) + shape, lambda b, s, pt: (b,) + (0,) * len(shape))
    pages = [page_spec(i) for i in range(ATTN_PAGES_PER_STEP)]
    grid_spec = pltpu.PrefetchScalarGridSpec(
        num_scalar_prefetch=1,
        grid=(bsz, N_PAGES // ATTN_PAGES_PER_STEP),
        in_specs=[per_b((N_KV_HEADS, SAMPLE_ROWS, HEAD_DIM)), per_b((N_KEY_PAGES, SAMPLE_ROWS, PAGE_SIZE)),
                  per_b((SAMPLE_ROWS, LANES)), per_b((SAMPLE_ROWS, LANES)),
                  per_b((page_rows, HEAD_DIM)), per_b((page_rows, HEAD_DIM))] + pages + pages,
        out_specs=per_b((N_KV_HEADS, SAMPLE_ROWS, HEAD_DIM)),
        scratch_shapes=[pltpu.VMEM((N_KV_HEADS, SAMPLE_ROWS, 1), F32),
                        pltpu.VMEM((N_KV_HEADS, SAMPLE_ROWS, 1), F32),
                        pltpu.VMEM((N_KV_HEADS, SAMPLE_ROWS, HEAD_DIM), F32)])
    return pl.pallas_call(
        _sample_attn_kernel,
        out_shape=jax.ShapeDtypeStruct((bsz, N_KV_HEADS, SAMPLE_ROWS, HEAD_DIM), F32),
        grid_spec=grid_spec,
        compiler_params=_params("parallel", "arbitrary"),
        name="dsa_sample_attn",
    )(page_table, q16, scores, thr, cut, k_new_pad, v_new_pad,
      *([pool_k] * ATTN_PAGES_PER_STEP), *([pool_v] * ATTN_PAGES_PER_STEP))


def _pad_cols(w, width):
    return jnp.pad(w, ((0, 0), (0, width - w.shape[1])))


def _dsa_weight(w):
    sizes = (TOK_WIDTH, N_KV_HEADS * HEAD_DIM, N_KV_HEADS * HEAD_DIM, IDX_HEADS * IDX_DIM, IDX_DIM, IDX_HEADS, MEM_WIDTH)
    offs = [0]
    for s in sizes:
        offs.append(offs[-1] + s)
    q, k, v, iq, ik, iw, mq = [w[:, offs[i]:offs[i + 1]] for i in range(len(sizes))]
    return jnp.concatenate([q, k, v, iq, ik, ik, _pad_cols(iw, LANES), mq], axis=1).astype(BF16)


def kernel(x_prompt, x_sample, cache_k, cache_v, cache_idx_k, cache_mem_k, cache_mem_v, state_hgrn, state_ffn_conv, page_table, mem_prompt, norm_mix, norm_ffn, w_in_hgrn, hgrn_lb_logits, hgrn_out_norm, w_in_dsa, dsa_q_norm, dsa_k_norm, idx_k_norm, w_mem_kv, mem_q_norm, mem_k_norm, w_out, w_ffn_gate, w_ffn_up, ffn_conv_w, ffn_conv_b, w_ffn_down):
    bp, seq, _ = x_prompt.shape
    bs, t_s, _ = x_sample.shape
    assert t_s == SAMPLE_T and seq % KEY_BLOCK == 0
    n_p = bp * seq
    n_s = bs * SAMPLE_T_PAD
    kvw = N_KV_HEADS * HEAD_DIM
    lower_bounds = jnp.cumsum(jax.nn.softmax(hgrn_lb_logits.astype(F32), axis=0), axis=0)

    xp = x_prompt
    xs = jnp.pad(x_sample, ((0, 0), (0, SAMPLE_T_PAD - t_s), (0, 0)))
    mem2d = mem_prompt.reshape(bp * N_MEM, D_MODEL)
    tabs_p = _rope_tables(jnp.arange(seq))
    pos_s = PAST_LEN + jnp.arange(SAMPLE_T_PAD)
    tabs_s = tuple(jnp.tile(tb, (bs, 1)) for tb in _rope_tables(pos_s))
    row_t = jnp.arange(n_s) % SAMPLE_T_PAD

    hgrn_p, hgrn_s = [], []
    kp_l, vp_l, ikp_l, ks_l, vs_l, iks_l = [], [], [], [], [], []
    mk_l, mv_l, cvp_l, cvs_l = [], [], [], []
    for layer in range(DEPTH):
        j = layer // 2
        if layer % 2 == 0:
            w_in = w_in_hgrn[j].astype(BF16)
            mem_col = HGRN_MEM0 // MEM_WIDTH
        else:
            w_in = _dsa_weight(w_in_dsa[j])
            mem_col = DSA_MEM0 // MEM_WIDTH
        cols = w_in.shape[1]
        zp = _norm_proj(xp.reshape(n_p, D_MODEL), norm_mix[layer], w_in, 256).reshape(bp, seq, cols)
        zs = _norm_proj(xs.reshape(n_s, D_MODEL), norm_mix[layer], w_in, n_s).reshape(bs, SAMPLE_T_PAD, cols)
        if layer % 2 == 0:
            s0 = jnp.zeros((bp, HGRN_HEADS, HGRN_DK, HGRN_DK), F32)
            tp, sp = _hgrn(zp, lower_bounds[j], hgrn_out_norm[j], s0, HGRN_CHUNK, HGRN_SUB, 512, HGRN_CHUNK)
            ts, ss = _hgrn(zs, lower_bounds[j], hgrn_out_norm[j], state_hgrn[j], SAMPLE_T_PAD, SAMPLE_T_PAD,
                           SAMPLE_T_PAD, math.gcd(t_s, HGRN_CHUNK))
            hgrn_p.append(sp.astype(state_hgrn.dtype))
            hgrn_s.append(ss.astype(state_hgrn.dtype))
        else:
            qp, kp, kpb, vp, vtp, iqp, ikp, ikpb, iwtp = _dsa_prep(
                zp, tabs_p, dsa_q_norm[j], dsa_k_norm[j], idx_k_norm[j], KEY_BLOCK)
            tp = _dsa_prompt(qp, iqp, iwtp, ikpb, kpb, vtp)
            kp_l.append(kp.reshape(bp, seq, N_KV_HEADS, HEAD_DIM))
            vp_l.append(vp.reshape(bp, seq, N_KV_HEADS, HEAD_DIM))
            ikp_l.append(ikp[..., :IDX_DIM])

            qs, kn, _, vn, _, iqs, ikn, _, iwts = _dsa_prep(
                zs.reshape(1, n_s, cols), tabs_s, dsa_q_norm[j], dsa_k_norm[j], idx_k_norm[j], n_s)
            sel_t = lambda a: a.reshape(bs, SAMPLE_T_PAD, -1)[:, :t_s]
            kn, vn, ikn = sel_t(kn), sel_t(vn), sel_t(ikn)[..., :IDX_DIM]
            ks_l.append(kn.reshape(bs, t_s, N_KV_HEADS, HEAD_DIM))
            vs_l.append(vn.reshape(bs, t_s, N_KV_HEADS, HEAD_DIM))
            iks_l.append(ikn)
            iq4 = jnp.tile(sel_t(iqs).reshape(bs, t_s * IDX_HEADS, IDX_DIM), (1, 1, 4))
            iw_col = sel_t(iwts[0].T).reshape(bs, t_s * IDX_HEADS, 1)
            ikn_t = jnp.pad(ikn.transpose(0, 2, 1), ((0, 0), (0, 0), (0, PAGE_SIZE - t_s)))
            scores = _sample_scores(page_table, iq4, iw_col, ikn_t, cache_idx_k[j].transpose(0, 2, 1))
            per_query = scores[:, :, :t_s, :].transpose(0, 2, 1, 3).reshape(bs * t_s, N_KEY_PAGES * PAGE_SIZE)
            thr, cut = _sample_select(per_query)
            rep = lambda a: jnp.tile(a.reshape(bs, t_s, LANES), (1, SAMPLE_ROWS // t_s, 1))
            thr, cut = rep(thr), rep(cut)
            q16 = sel_t(qs).reshape(bs, t_s, N_KV_HEADS, Q_PER_KV, HEAD_DIM).transpose(0, 2, 3, 1, 4)
            q16 = jnp.pad(q16.reshape(bs, N_KV_HEADS, Q_PER_KV * t_s, HEAD_DIM),
                          ((0, 0), (0, 0), (0, SAMPLE_ROWS - Q_PER_KV * t_s), (0, 0)))
            page_rows = lambda a: a.reshape(a.shape[0], -1, HEAD_DIM)
            pad_page = lambda a: jnp.pad(page_rows(a), ((0, 0), (0, (PAGE_SIZE - t_s) * N_KV_HEADS), (0, 0)))
            o16 = _sample_attn(page_table, q16, scores, thr, cut, pad_page(kn), pad_page(vn),
                               page_rows(cache_k[j]), page_rows(cache_v[j]))
            o = o16[:, :, :Q_PER_KV * t_s].reshape(bs, N_KV_HEADS, Q_PER_KV, t_s, HEAD_DIM)
            ts = jnp.pad(o.transpose(0, 3, 1, 2, 4).reshape(bs, t_s, TOK_WIDTH),
                         ((0, 0), (0, SAMPLE_T_PAD - t_s), (0, 0)))

        mkp, mvp = _mem_kv(mem2d, w_mem_kv[layer].astype(BF16), mem_k_norm[layer])
        mk_l.append(mkp.reshape(bp, N_MEM, MEM_HEADS, MEM_HD))
        mv_l.append(mvp.reshape(bp, N_MEM, MEM_HEADS, MEM_HD))
        cp = _mem_attn(zp, mem_col, mem_q_norm[layer], mkp.reshape(bp, N_MEM, MEM_WIDTH),
                       mvp.reshape(bp, N_MEM, MEM_WIDTH), 512)
        cs = _mem_attn(zs, mem_col, mem_q_norm[layer], cache_mem_k[layer].reshape(bs, N_MEM, MEM_WIDTH),
                       cache_mem_v[layer].reshape(bs, N_MEM, MEM_WIDTH), SAMPLE_T_PAD)
        w_o = w_out[layer].astype(BF16)
        xp = _out_proj(xp.reshape(n_p, D_MODEL), tp.reshape(n_p, TOK_WIDTH), cp.reshape(n_p, MEM_WIDTH),
                       w_o, 512).reshape(bp, seq, D_MODEL)
        xs2 = _out_proj(xs.reshape(n_s, D_MODEL), ts.reshape(n_s, TOK_WIDTH), cs.reshape(n_s, MEM_WIDTH), w_o, n_s)

        wg, wu, wd = (w.astype(BF16) for w in (w_ffn_gate[layer], w_ffn_up[layer], w_ffn_down[layer]))
        ffn_w = (norm_ffn[layer], wg, wu, ffn_conv_w[layer], ffn_conv_b[layer], wd)
        xp, u_last = _ffn_long(xp, *ffn_w, jnp.zeros((bp, CONV_W - 1, D_FF), F32), 256)
        cvp_l.append(u_last[:, SUBLANES - (CONV_W - 1):, :])
        st = state_ffn_conv[layer]
        zrow = jnp.zeros((bs, SAMPLE_T_PAD - 1, D_FF), F32)
        p1 = jnp.concatenate([st[:, 1:2], zrow], axis=1).reshape(n_s, D_FF)
        p2 = jnp.concatenate([st, zrow[:, 1:]], axis=1).reshape(n_s, D_FF)
        xs2, u_s = _ffn_short(xs2, *ffn_w, p1, p2, SAMPLE_T_PAD)
        cvs_l.append(u_s.reshape(bs, SAMPLE_T_PAD, D_FF)[:, t_s - (CONV_W - 1):t_s])
        xs = jnp.where((row_t < t_s)[:, None], xs2, 0.0).reshape(bs, SAMPLE_T_PAD, D_MODEL)

    return (xp, xs[:, :t_s],
            jnp.stack(hgrn_p), jnp.stack(hgrn_s),
            jnp.stack(kp_l), jnp.stack(vp_l), jnp.stack(ikp_l),
            jnp.stack(ks_l), jnp.stack(vs_l), jnp.stack(iks_l),
            jnp.stack(mk_l), jnp.stack(mv_l),
            jnp.stack(cvp_l), jnp.stack(cvs_l))
```

```python
import functools
import math

import jax
import jax.numpy as jnp
from jax import lax
from jax.experimental import pallas as pl
from jax.experimental.pallas import tpu as pltpu

D_MODEL = 1024
DEPTH = 2
PAST_LEN = 8192
PAGE_SIZE = 128
MEM_WIDTH = 256
TOK_WIDTH = 768
MEM_HEADS = 4
MEM_HD = 64
N_MEM = 256
HGRN_DK = 128
HGRN_HEADS = 6
HGRN_CHUNK = 32
HEAD_DIM = 128
N_Q_HEADS = 6
N_KV_HEADS = 2
Q_PER_KV = N_Q_HEADS // N_KV_HEADS
IDX_HEADS = 8
IDX_DIM = 64
TOPK = 256
ROPE_THETA = 10000.0
D_FF = 2816
CONV_W = 3
EPS = 1e-6

LANES = 128
SUBLANES = 8
VMEM_LIMIT_BYTES = 56 * 1024 * 1024
FF_CHUNK = 256
HGRN_SUB = 256
PROJ_CHUNK = 256
KEY_BLOCK = 512
Q_TILE = 128
SAMPLE_T_PAD = 8
SCORE_PAGES_PER_STEP = 32
ATTN_PAGES_PER_STEP = 16
NEG_BIG = -1e30
QK_SCALE_LOG2E = HEAD_DIM ** -0.5 * math.log2(math.e)
F32 = jnp.float32
BF16 = jnp.bfloat16
I32 = jnp.int32
INT_MIN = -2 ** 31

DSA_Q0, DSA_K0, DSA_V0, DSA_IQ0, DSA_IK0, DSA_IW0, DSA_MEM0, DSA_COLS_PAD = 0, 768, 1024, 1280, 1792, 1920, 2048, 2304
HGRN_MEM0 = 4 * TOK_WIDTH


def _params(*sem):
    return pltpu.CompilerParams(dimension_semantics=sem, vmem_limit_bytes=VMEM_LIMIT_BYTES)


def _rms(x, gain):
    ms = jnp.mean(x * x, axis=-1, keepdims=True)
    return x * lax.rsqrt(ms + EPS) * gain


def _split_bf16(x):
    hi = x.astype(BF16)
    lo = (x - hi.astype(F32)).astype(BF16)
    return hi, lo


def _group_mean_sq(x, group):
    n = x.shape[-1]
    r = lax.broadcasted_iota(I32, (n, n), 0) // group
    c = lax.broadcasted_iota(I32, (n, n), 1) // group
    ones = jnp.where(r == c, 1.0, 0.0).astype(BF16)
    hi, lo = _split_bf16(x * x)
    s = jnp.dot(hi, ones, preferred_element_type=F32) + jnp.dot(lo, ones, preferred_element_type=F32)
    return s * (1.0 / group)


def _dot_nt(a, b):
    return lax.dot_general(a, b, (((1,), (1,)), ((), ())), preferred_element_type=F32)


def _dot_tn(a, b):
    return lax.dot_general(a, b, (((0,), (0,)), ((), ())), preferred_element_type=F32)


def _silu(x):
    return x * jax.nn.sigmoid(x)


def _norm_proj_kernel(x_ref, g_ref, w_ref, o_ref):
    h = _rms(x_ref[...], g_ref[...]).astype(BF16)
    o_ref[...] = jnp.dot(h, w_ref[...], preferred_element_type=F32)


def _norm_proj(x2d, gain, w_bf16, tm):
    n, d = x2d.shape
    c = w_bf16.shape[1]
    return pl.pallas_call(
        _norm_proj_kernel,
        out_shape=jax.ShapeDtypeStruct((n, c), F32),
        grid=(n // tm,),
        in_specs=[pl.BlockSpec((tm, d), lambda i: (i, 0)),
                  pl.BlockSpec((1, d), lambda i: (0, 0)),
                  pl.BlockSpec((d, c), lambda i: (0, 0))],
        out_specs=pl.BlockSpec((tm, c), lambda i: (i, 0)),
        compiler_params=_params("parallel"),
        name="norm_proj",
    )(x2d, gain.reshape(1, d), w_bf16)


def _hgrn_kernel(zin_ref, lb_ref, on_ref, s0_ref, o_ref, so_ref, st_ref,
                 *, chunk, sub, n_sub, valid_rows):
    step = pl.program_id(1)

    @pl.when(step == 0)
    def _():
        for h in range(HGRN_HEADS):
            st_ref[h] = s0_ref[0, h].T

    z_ref = zin_ref.at[0]
    zcol = lambda part, h: slice(part * TOK_WIDTH + h * HGRN_DK, part * TOK_WIDTH + (h + 1) * HGRN_DK)

    row = lax.broadcasted_iota(I32, (sub, sub), 0)
    col = lax.broadcasted_iota(I32, (sub, sub), 1)
    same_chunk = (row // chunk) == (col // chunk)
    causal = jnp.logical_and(same_chunk, row >= col)
    sum_lhs = jnp.concatenate([jnp.where(causal, 1.0, 0.0), jnp.where(same_chunk, 1.0, 0.0)], axis=0).astype(BF16)
    row_id = lax.broadcasted_iota(I32, (sub, HGRN_DK), 0)
    row_ok = row_id % chunk < valid_rows
    n_c = sub // chunk
    in_chunk = [row_id // chunk == c for c in range(n_c)]

    def body(si, carry):
        rows = pl.ds(pl.multiple_of(si * sub, sub), sub)
        for h in range(HGRN_HEADS):
            sl = slice(h * HGRN_DK, (h + 1) * HGRN_DK)
            lb = lb_ref[:, sl]
            q = _silu(z_ref[rows, zcol(0, h)])
            f = lb + (1.0 - lb) * jax.nn.sigmoid(z_ref[rows, zcol(1, h)])
            if valid_rows < chunk:
                f = jnp.where(row_ok, f, 1.0)
            k = 1.0 - f
            vb = z_ref[rows, zcol(2, h)].astype(BF16)
            g_hi, g_lo = _split_bf16(jnp.log(f))
            sums = jnp.dot(sum_lhs, jnp.concatenate([g_hi, g_lo], axis=1), preferred_element_type=F32)
            b = sums[:sub, :HGRN_DK] + sums[:sub, HGRN_DK:]
            b_chunk = sums[sub:, :HGRN_DK] + sums[sub:, HGRN_DK:]
            q_dec = q * jnp.exp(b)
            k_dec = (k * jnp.exp(-b)).astype(BF16)
            k_rem = k * jnp.exp(b_chunk - b)
            chunk_decay = jnp.exp(b_chunk)
            scores = jnp.where(causal, _dot_nt(q_dec.astype(BF16), k_dec), 0.0)
            o = jnp.dot(scores.astype(BF16), vb, preferred_element_type=F32)
            spread = lambda a: jnp.concatenate([jnp.where(in_chunk[c], a, 0.0) for c in range(n_c)],
                                               axis=1).astype(BF16)
            incr = _dot_tn(vb, spread(k_rem))
            st = st_ref[h]
            states = []
            for c in range(n_c):
                states.append(st.astype(BF16))
                st = st * chunk_decay[c * chunk:c * chunk + 1] + incr[:, c * HGRN_DK:(c + 1) * HGRN_DK]
            st_ref[h] = st
            o = o + _dot_nt(spread(q_dec), jnp.concatenate(states, axis=1))
            o_ref[0, rows, sl] = _rms(o, on_ref[...]) * _silu(z_ref[rows, zcol(3, h)])
        return carry

    lax.fori_loop(0, n_sub, body, 0)

    @pl.when(step == pl.num_programs(1) - 1)
    def _():
        for h in range(HGRN_HEADS):
            so_ref[0, h] = st_ref[h].T


def _hgrn(z3d, lb, out_norm, s0, chunk, sub, t_step, valid_rows):
    bsz, t, _ = z3d.shape
    const = lambda shape: pl.BlockSpec(shape, lambda b, i: (0,) * len(shape))
    rows = lambda w: pl.BlockSpec((1, t_step, w), lambda b, i: (b, i, 0))
    state_spec = pl.BlockSpec((1, HGRN_HEADS, HGRN_DK, HGRN_DK), lambda b, i: (b, 0, 0, 0))
    kern = functools.partial(_hgrn_kernel, chunk=chunk, sub=sub, n_sub=t_step // sub, valid_rows=valid_rows)
    return pl.pallas_call(
        kern,
        out_shape=(jax.ShapeDtypeStruct((bsz, t, TOK_WIDTH), F32),
                   jax.ShapeDtypeStruct((bsz, HGRN_HEADS, HGRN_DK, HGRN_DK), F32)),
        grid=(bsz, t // t_step),
        in_specs=[rows(4 * TOK_WIDTH), const((1, TOK_WIDTH)), const((1, HGRN_DK)), state_spec],
        out_specs=(rows(TOK_WIDTH), state_spec),
        scratch_shapes=[pltpu.VMEM((HGRN_HEADS, HGRN_DK, HGRN_DK), F32)],
        compiler_params=_params("parallel", "arbitrary"),
        name="hgrn2",
    )(z3d, lb.reshape(1, TOK_WIDTH), out_norm.reshape(1, HGRN_DK), s0)


def _mem_kv_kernel(m_ref, w_ref, g_ref, k_ref, v_ref):
    kv = jnp.dot(m_ref[...].astype(BF16), w_ref[...], preferred_element_type=F32)
    k = kv[:, :MEM_WIDTH]
    k_ref[...] = k * lax.rsqrt(_group_mean_sq(k, MEM_HD) + EPS) * g_ref[...]
    v_ref[...] = kv[:, MEM_WIDTH:]


def _mem_kv(mem2d, w_bf16, k_gain):
    n = mem2d.shape[0]
    out = jax.ShapeDtypeStruct((n, MEM_WIDTH), F32)
    return pl.pallas_call(
        _mem_kv_kernel,
        out_shape=(out, out),
        grid=(n // N_MEM,),
        in_specs=[pl.BlockSpec((N_MEM, D_MODEL), lambda i: (i, 0)),
                  pl.BlockSpec((D_MODEL, 2 * MEM_WIDTH), lambda i: (0, 0)),
                  pl.BlockSpec((1, MEM_WIDTH), lambda i: (0, 0))],
        out_specs=(pl.BlockSpec((N_MEM, MEM_WIDTH), lambda i: (i, 0)),
                   pl.BlockSpec((N_MEM, MEM_WIDTH), lambda i: (i, 0))),
        compiler_params=_params("parallel"),
        name="mem_kv",
    )(mem2d, w_bf16, jnp.tile(k_gain, MEM_HEADS).reshape(1, MEM_WIDTH))


def _mem_attn_kernel(zq_ref, g_ref, mk_ref, mv_ref, o_ref):
    q = zq_ref[0]
    qn = q * lax.rsqrt(_group_mean_sq(q, MEM_HD) + EPS) * g_ref[...]
    mk = mk_ref[0].astype(BF16)
    mv = mv_ref[0].astype(BF16)
    head_of_lane = lax.broadcasted_iota(I32, qn.shape, 1) // MEM_HD
    out = jnp.zeros(qn.shape, F32)
    for h in range(MEM_HEADS):
        mine = head_of_lane == h
        s = _dot_nt(jnp.where(mine, qn, 0.0).astype(BF16), mk) * (MEM_HD ** -0.5)
        p = jnp.exp(s - jnp.max(s, axis=-1, keepdims=True))
        oh = jnp.dot(p.astype(BF16), mv, preferred_element_type=F32) / jnp.sum(p, axis=-1, keepdims=True)
        out = jnp.where(mine, oh, out)
    o_ref[0] = out


def _mem_attn(z3d, q_col_block, q_gain, mk, mv, tm):
    bsz, t, _ = z3d.shape
    return pl.pallas_call(
        _mem_attn_kernel,
        out_shape=jax.ShapeDtypeStruct((bsz, t, MEM_WIDTH), F32),
        grid=(bsz, t // tm),
        in_specs=[pl.BlockSpec((1, tm, MEM_WIDTH), lambda b, i: (b, i, q_col_block)),
                  pl.BlockSpec((1, MEM_WIDTH), lambda b, i: (0, 0)),
                  pl.BlockSpec((1, N_MEM, MEM_WIDTH), lambda b, i: (b, 0, 0)),
                  pl.BlockSpec((1, N_MEM, MEM_WIDTH), lambda b, i: (b, 0, 0))],
        out_specs=pl.BlockSpec((1, tm, MEM_WIDTH), lambda b, i: (b, i, 0)),
        compiler_params=_params("parallel", "parallel"),
        name="mem_attn",
    )(z3d, jnp.tile(q_gain, MEM_HEADS).reshape(1, MEM_WIDTH), mk, mv)


def _out_proj_kernel(x_ref, t_ref, c_ref, wt_ref, wc_ref, o_ref):
    o_ref[...] = (x_ref[...]
                  + jnp.dot(t_ref[...].astype(BF16), wt_ref[...], preferred_element_type=F32)
                  + jnp.dot(c_ref[...].astype(BF16), wc_ref[...], preferred_element_type=F32))


def _out_proj(x2d, t2d, c2d, w_bf16, tm):
    n = x2d.shape[0]
    row = lambda w: pl.BlockSpec((tm, w), lambda i: (i, 0))
    return pl.pallas_call(
        _out_proj_kernel,
        out_shape=jax.ShapeDtypeStruct((n, D_MODEL), F32),
        grid=(n // tm,),
        in_specs=[row(D_MODEL), row(TOK_WIDTH), row(MEM_WIDTH),
                  pl.BlockSpec((TOK_WIDTH, D_MODEL), lambda i: (0, 0)),
                  pl.BlockSpec((MEM_WIDTH, D_MODEL), lambda i: (0, 0))],
        out_specs=row(D_MODEL),
        compiler_params=_params("parallel"),
        name="out_proj",
    )(x2d, t2d, c2d, w_bf16[:TOK_WIDTH], w_bf16[TOK_WIDTH:])


def _ffn_kernel(x_ref, g_ref, wg_ref, wu_ref, cw_ref, cb_ref, wd_ref, p1_ref, p2_ref,
                y_ref, u_ref, carry_ref, *, seq_rows, carry):
    tm = x_ref.shape[0]
    x = x_ref[...]
    h = _rms(x, g_ref[...]).astype(BF16)
    t_in_seq = lax.broadcasted_iota(I32, (tm, FF_CHUNK), 0) % seq_rows
    first_block = pl.program_id(1) == 0 if carry else None
    n_chunks = D_FF // FF_CHUNK

    def project(c):
        cs = slice(c * FF_CHUNK, (c + 1) * FF_CHUNK)
        return (jnp.dot(h, wg_ref[:, cs], preferred_element_type=F32),
                jnp.dot(h, wu_ref[:, cs], preferred_element_type=F32))

    acc = x
    ahead = project(0)
    for c in range(n_chunks):
        cs = slice(c * FF_CHUNK, (c + 1) * FF_CHUNK)
        u, up = ahead
        if c + 1 < n_chunks:
            ahead = project(c + 1)
        u1 = pltpu.roll(u, 1, 0)
        u2 = pltpu.roll(u, 2, 0)
        if carry:
            prev = carry_ref[:, cs]
            start1 = jnp.where(first_block, p1_ref[0:1, cs], prev[7:8, :])
            start2a = jnp.where(first_block, p2_ref[0:1, cs], prev[6:7, :])
            start2b = jnp.where(first_block, p2_ref[1:2, cs], prev[7:8, :])
            u1 = jnp.where(t_in_seq == 0, start1, u1)
            u2 = jnp.where(t_in_seq == 0, start2a, jnp.where(t_in_seq == 1, start2b, u2))
            carry_ref[:, cs] = u[tm - SUBLANES:, :]
        else:
            u1 = jnp.where(t_in_seq == 0, p1_ref[:, cs], u1)
            u2 = jnp.where(t_in_seq < 2, p2_ref[:, cs], u2)
        conv = cb_ref[:, cs] + u2 * cw_ref[0:1, cs] + u1 * cw_ref[1:2, cs] + u * cw_ref[2:3, cs]
        a = (_silu(conv) * up).astype(BF16)
        acc = acc + jnp.dot(a, wd_ref[cs, :], preferred_element_type=F32)
        if carry:
            u_ref[0, :, cs] = u[tm - SUBLANES:, :]
        else:
            u_ref[:, cs] = u
    y_ref[...] = acc


def _ffn_weights_specs(idx):
    once = pl.Buffered(1)
    return [pl.BlockSpec((1, D_MODEL), idx),
            pl.BlockSpec((D_MODEL, D_FF), idx, pipeline_mode=once),
            pl.BlockSpec((D_MODEL, D_FF), idx, pipeline_mode=once),
            pl.BlockSpec((CONV_W, D_FF), idx),
            pl.BlockSpec((1, D_FF), idx),
            pl.BlockSpec((D_FF, D_MODEL), idx, pipeline_mode=once)]


def _ffn_long(x3d, gain, wg, wu, cw, cb, wd, state, tm):
    bsz, t, _ = x3d.shape
    idx = lambda b, i: (0, 0)
    kern = functools.partial(_ffn_kernel, seq_rows=t, carry=True)
    nb = t // tm
    row = lambda w: pl.BlockSpec((tm, w), lambda b, i: (b * nb + i, 0))
    p1 = state[:, 1:2, :]
    y, u_last = pl.pallas_call(
        kern,
        out_shape=(jax.ShapeDtypeStruct((bsz * t, D_MODEL), F32),
                   jax.ShapeDtypeStruct((bsz, SUBLANES, D_FF), F32)),
        grid=(bsz, nb),
        in_specs=[row(D_MODEL)] + _ffn_weights_specs(idx)
                 + [pl.BlockSpec((None, 1, D_FF), lambda b, i: (b, 0, 0)),
                    pl.BlockSpec((None, 2, D_FF), lambda b, i: (b, 0, 0))],
        out_specs=(row(D_MODEL), pl.BlockSpec((1, SUBLANES, D_FF), lambda b, i: (b, 0, 0))),
        scratch_shapes=[pltpu.VMEM((SUBLANES, D_FF), F32)],
        compiler_params=_params("parallel", "arbitrary"),
        name="conv_ffn_long",
    )(x3d.reshape(bsz * t, D_MODEL), gain.reshape(1, D_MODEL), wg, wu, cw, cb.reshape(1, D_FF), wd, p1, state)
    return y.reshape(bsz, t, D_MODEL), u_last


def _ffn_short(x2d, gain, wg, wu, cw, cb, wd, p1, p2, seq_rows):
    n = x2d.shape[0]
    idx = lambda i: (0, 0)
    kern = functools.partial(_ffn_kernel, seq_rows=seq_rows, carry=False)
    row = lambda w: pl.BlockSpec((n, w), lambda i: (0, 0))
    return pl.pallas_call(
        kern,
        out_shape=(jax.ShapeDtypeStruct((n, D_MODEL), F32), jax.ShapeDtypeStruct((n, D_FF), F32)),
        grid=(1,),
        in_specs=[row(D_MODEL)] + _ffn_weights_specs(idx) + [row(D_FF), row(D_FF)],
        out_specs=(row(D_MODEL), row(D_FF)),
        scratch_shapes=[pltpu.VMEM((SUBLANES, D_FF), F32)],
        compiler_params=_params("arbitrary"),
        name="conv_ffn_short",
    )(x2d, gain.reshape(1, D_MODEL), wg, wu, cw, cb.reshape(1, D_FF), wd, p1, p2)


def _rope128(x, cos, sin_signed):
    return x * cos + pltpu.roll(x, HEAD_DIM // 2, 1) * sin_signed


def _rope64_pairs(x, cos, sin_lo, sin_hi):
    return (x * cos + pltpu.roll(x, LANES - IDX_DIM // 2, 1) * sin_lo
            + pltpu.roll(x, IDX_DIM // 2, 1) * sin_hi)


def _dsa_prep_kernel(x_ref, ng_ref, w_ref, c128_ref, s128_ref, c64_ref, s64lo_ref, s64hi_ref, qg_ref, kg_ref, ikg_ref,
                     q_ref, k_ref, kb_ref, v_ref, vt_ref, iq_ref, ik_ref, ikb_ref, iwt_ref, mq_ref, z_ref):
    hn = _rms(x_ref[0], ng_ref[...]).astype(BF16)
    for j in range(DSA_COLS_PAD // PROJ_CHUNK):
        cs = slice(j * PROJ_CHUNK, (j + 1) * PROJ_CHUNK)
        z_ref[0, :, cs] = jnp.dot(hn, w_ref[:, cs], preferred_element_type=F32)
    mq_ref[0] = z_ref[0, :, DSA_MEM0:]
    c128, s128 = c128_ref[...], s128_ref[...]
    c64, s64lo, s64hi = c64_ref[...], s64lo_ref[...], s64hi_ref[...]
    for h in range(N_Q_HEADS):
        sl = slice(DSA_Q0 + h * HEAD_DIM, DSA_Q0 + (h + 1) * HEAD_DIM)
        q = _rope128(_rms(z_ref[0, :, sl], qg_ref[...]), c128, s128)
        q_ref[0, :, h * HEAD_DIM:(h + 1) * HEAD_DIM] = (q * QK_SCALE_LOG2E).astype(BF16)
    for h in range(N_KV_HEADS):
        sl = slice(DSA_K0 + h * HEAD_DIM, DSA_K0 + (h + 1) * HEAD_DIM)
        k = _rope128(_rms(z_ref[0, :, sl], kg_ref[...]), c128, s128)
        k_ref[0, :, h * HEAD_DIM:(h + 1) * HEAD_DIM] = k
        kb_ref[0, :, h * HEAD_DIM:(h + 1) * HEAD_DIM] = k.astype(BF16)
    v = z_ref[0, :, DSA_V0:DSA_V0 + N_KV_HEADS * HEAD_DIM]
    v_ref[0] = v
    vt_ref[0, 0] = v.T.astype(BF16)
    for p in range(IDX_HEADS // 2):
        sl = slice(DSA_IQ0 + p * LANES, DSA_IQ0 + (p + 1) * LANES)
        iq_ref[0, :, p * LANES:(p + 1) * LANES] = _rope64_pairs(z_ref[0, :, sl], c64, s64lo, s64hi)
    ik = _rope64_pairs(_rms(z_ref[0, :, DSA_IK0:DSA_IK0 + LANES], ikg_ref[...]), c64, s64lo, s64hi)
    ik_ref[0] = ik
    ik_hi, ik_lo = _split_bf16(ik)
    hi_lo = jnp.where(lax.broadcasted_iota(I32, ik.shape, 1) < IDX_DIM, ik_hi, ik_lo)
    ikb_ref[0, :, :LANES] = hi_lo
    ikb_ref[0, :, LANES:] = hi_lo
    iw = z_ref[0, :, DSA_IW0:DSA_IW0 + LANES] * (IDX_HEADS ** -0.5 * IDX_DIM ** -0.5)
    iwt_ref[0] = iw.T[:IDX_HEADS, :]


def _dsa_prep(x3d, norm_gain, w_bf16, tabs, q_gain, k_gain, ik_gain, tm):
    bsz, t, _ = x3d.shape
    nb = t // tm
    tab = pl.BlockSpec((tm, LANES), lambda b, i: (i, 0))
    gain = pl.BlockSpec((1, LANES), lambda b, i: (0, 0))
    rows = lambda w: pl.BlockSpec((1, tm, w), lambda b, i: (b, i, 0))
    const = lambda shape, **kw: pl.BlockSpec(shape, lambda b, i: (0,) * len(shape), **kw)
    kvw = N_KV_HEADS * HEAD_DIM
    sds = jax.ShapeDtypeStruct
    return pl.pallas_call(
        _dsa_prep_kernel,
        out_shape=(sds((bsz, t, TOK_WIDTH), BF16),
                   sds((bsz, t, kvw), F32), sds((bsz, t, kvw), BF16),
                   sds((bsz, t, kvw), F32), sds((bsz, nb, kvw, tm), BF16),
                   sds((bsz, t, IDX_HEADS * IDX_DIM), F32),
                   sds((bsz, t, LANES), F32),
                   sds((bsz, t, 2 * LANES), BF16),
                   sds((bsz, IDX_HEADS, t), F32),
                   sds((bsz, t, MEM_WIDTH), F32)),
        grid=(bsz, nb),
        in_specs=[rows(D_MODEL), const((1, D_MODEL)),
                  const((D_MODEL, DSA_COLS_PAD), pipeline_mode=pl.Buffered(1)),
                  tab, tab, tab, tab, tab, gain, gain, gain],
        out_specs=(rows(TOK_WIDTH), rows(kvw), rows(kvw), rows(kvw),
                   pl.BlockSpec((1, 1, kvw, tm), lambda b, i: (b, i, 0, 0)),
                   rows(IDX_HEADS * IDX_DIM), rows(LANES), rows(2 * LANES),
                   pl.BlockSpec((1, IDX_HEADS, tm), lambda b, i: (b, 0, i)),
                   rows(MEM_WIDTH)),
        scratch_shapes=[pltpu.VMEM((1, tm, DSA_COLS_PAD), F32)],
        compiler_params=_params("parallel", "parallel"),
        name="dsa_prep",
    )(x3d, norm_gain.reshape(1, D_MODEL), w_bf16, *tabs,
      q_gain.reshape(1, LANES), k_gain.reshape(1, LANES), jnp.tile(ik_gain, 2).reshape(1, LANES))


def _rope_tables(pos):
    pos = pos.astype(F32)[:, None]

    def cs(half):
        inv = ROPE_THETA ** (-jnp.arange(half, dtype=F32) / half)
        ang = pos * inv[None, :]
        return jnp.cos(ang), jnp.sin(ang)

    c, s = cs(HEAD_DIM // 2)
    c128 = jnp.concatenate([c, c], axis=1)
    s128 = jnp.concatenate([-s, s], axis=1)
    c, s = cs(IDX_DIM // 2)
    z = jnp.zeros_like(s)
    c64 = jnp.concatenate([c, c, c, c], axis=1)
    s64lo = jnp.concatenate([-s, z, -s, z], axis=1)
    s64hi = jnp.concatenate([z, s, z, s], axis=1)
    return c128, s128, c64, s64lo, s64hi


def _ordered_bits_to_float(u):
    key = u ^ INT_MIN
    bits = key ^ ((key >> 31) & 0x7FFFFFFF)
    return lax.bitcast_convert_type(bits, F32)


def _topk_threshold(count_ge, count_gt, count_eq_before, any_true, shape, idx_bits):
    def value_step(i, prefix):
        cand = prefix | jnp.left_shift(jnp.int32(1), 31 - i)
        below_neg_inf = jnp.logical_and(cand >= 0, cand < 0x00800000)
        ok = jnp.logical_or(count_ge(_ordered_bits_to_float(cand)) >= TOPK, below_neg_inf)
        return jnp.where(ok, cand, prefix)

    thr = _ordered_bits_to_float(lax.fori_loop(0, 32, value_step, jnp.zeros(shape, I32)))
    n_gt = count_gt(thr)
    need = TOPK - n_gt
    n_eq = count_ge(thr) - n_gt
    finite_thr = thr > -jnp.inf
    excess = jnp.logical_and(n_eq > need, finite_thr)

    def index_search():
        def index_step(i, cut):
            cand = cut | jnp.left_shift(jnp.int32(1), idx_bits - 1 - i)
            return jnp.where(count_eq_before(thr, cand) < need, cand, cut)
        return lax.fori_loop(0, idx_bits, index_step, jnp.zeros(shape, I32))

    everything = jnp.full(shape, 2 ** idx_bits - 1, I32)
    cut = lax.cond(any_true(excess), index_search, lambda: everything)
    return thr, jnp.where(finite_thr, cut, -1)


def _dsa_prompt_kernel(q_ref, iq_ref, iwt_ref, ik_ref, k_ref, vt_ref, prev_ref, o_ref,
                       score_ref, iqm_ref, qm_ref, acc_ref, *, n_kb):
    del prev_ref
    q0 = ((n_kb - 1) * (KEY_BLOCK // Q_TILE) + pl.program_id(1)) * Q_TILE
    lane = lax.broadcasted_iota(I32, (Q_TILE, LANES), 1)
    first_half = lane < IDX_DIM
    for p in range(IDX_HEADS // 2):
        pair = iq_ref[0, :, p * LANES:(p + 1) * LANES]
        swapped = pltpu.roll(pair, IDX_DIM, 1)
        for half in range(2):
            dup = jnp.where(first_half, pair, swapped) if half == 0 else jnp.where(first_half, swapped, pair)
            hi, lo = _split_bf16(dup)
            rows = slice(half * Q_TILE, (half + 1) * Q_TILE)
            iqm_ref[p, rows, :LANES] = hi
            iqm_ref[p, rows, LANES:] = lo
    for h in range(N_Q_HEADS):
        g = h // Q_PER_KV
        qh = q_ref[0, :, h * HEAD_DIM:(h + 1) * HEAD_DIM]
        rows = slice(h * Q_TILE, (h + 1) * Q_TILE)
        qm_ref[rows, g * HEAD_DIM:(g + 1) * HEAD_DIM] = qh
        qm_ref[rows, (1 - g) * HEAD_DIM:(2 - g) * HEAD_DIM] = jnp.zeros_like(qh)
    q_pos = q0 + lax.broadcasted_iota(I32, (1, Q_TILE), 1)
    key_row = lax.broadcasted_iota(I32, (KEY_BLOCK, 1), 0)

    blocks = [slice(kb * KEY_BLOCK, (kb + 1) * KEY_BLOCK) for kb in range(n_kb)]

    for kb in range(n_kb):
        ik = ik_ref[0, blocks[kb], :]
        acc = jnp.zeros((KEY_BLOCK, Q_TILE), F32)
        for p in range(IDX_HEADS // 2):
            d = _dot_nt(ik, iqm_ref[p])
            for half in range(2):
                h = 2 * p + half
                acc = acc + iwt_ref[0, h:h + 1, :] * jnp.maximum(d[:, half * Q_TILE:(half + 1) * Q_TILE], 0.0)
        if kb == n_kb - 1:
            acc = jnp.where(kb * KEY_BLOCK + key_row <= q_pos, acc, -jnp.inf)
        score_ref[blocks[kb], :] = acc

    def count(pred):
        part = jnp.zeros((SUBLANES, Q_TILE), I32)
        for kb in range(n_kb):
            hit = pred(score_ref[blocks[kb], :], kb * KEY_BLOCK + key_row).astype(I32)
            part = part + jnp.sum(hit.reshape(KEY_BLOCK // SUBLANES, SUBLANES, Q_TILE), axis=0)
        return jnp.sum(part, axis=0, keepdims=True)

    thr, cut = _topk_threshold(
        count_ge=lambda c: count(lambda s, idx: s >= c),
        count_gt=lambda c: count(lambda s, idx: s > c),
        count_eq_before=lambda t, j: count(lambda s, idx: jnp.logical_and(s == t, idx < j)),
        any_true=lambda m: jnp.max(m.astype(I32)) > 0,
        shape=(1, Q_TILE), idx_bits=13)

    m = [jnp.full((1, Q_TILE), NEG_BIG, F32)] * N_Q_HEADS
    l = [jnp.zeros((1, Q_TILE), F32)] * N_Q_HEADS
    for kb in range(n_kb):
        sc = score_ref[blocks[kb], :]
        sel = jnp.logical_or(sc > thr, jnp.logical_and(sc == thr, kb * KEY_BLOCK + key_row <= cut))
        k_blk = k_ref[0, blocks[kb], :]
        alphas, ps = [], []
        for j in range(N_Q_HEADS // 2):
            s2 = _dot_nt(k_blk, qm_ref[2 * j * Q_TILE:2 * (j + 1) * Q_TILE, :])
            for half in range(2):
                h = 2 * j + half
                s = jnp.where(sel, s2[:, half * Q_TILE:(half + 1) * Q_TILE], NEG_BIG)
                m_new = jnp.maximum(m[h], jnp.max(s, axis=0, keepdims=True))
                alpha = jnp.exp2(m[h] - m_new)
                p = jnp.exp2(s - m_new)
                l[h] = alpha * l[h] + jnp.sum(p, axis=0, keepdims=True)
                m[h] = m_new
                alphas.append(alpha)
                ps.append(p.astype(BF16))
        for g in range(N_KV_HEADS):
            vtg = vt_ref[0, kb, g * HEAD_DIM:(g + 1) * HEAD_DIM, :]
            pv = jnp.dot(vtg, jnp.concatenate(ps[g * Q_PER_KV:(g + 1) * Q_PER_KV], axis=1),
                         preferred_element_type=F32)
            for r in range(Q_PER_KV):
                h = g * Q_PER_KV + r
                part = pv[:, r * Q_TILE:(r + 1) * Q_TILE]
                acc_ref[h] = part if kb == 0 else alphas[h] * acc_ref[h] + part
    for h in range(N_Q_HEADS):
        o_ref[0, :, h * HEAD_DIM:(h + 1) * HEAD_DIM] = (acc_ref[h] / l[h]).T


def _dsa_prompt(q, iq, iwt, ikb, kb, vt):
    bsz, t, _ = q.shape
    kvw = N_KV_HEADS * HEAD_DIM
    tiles = KEY_BLOCK // Q_TILE
    out = jnp.zeros((bsz, t, TOK_WIDTH), F32)
    for n_kb in range(1, t // KEY_BLOCK + 1):
        tile = lambda b, i, n_kb=n_kb: (n_kb - 1) * tiles + i
        keys = lambda w, n_kb=n_kb: pl.BlockSpec((1, n_kb * KEY_BLOCK, w), lambda b, i: (b, 0, 0))
        out = pl.pallas_call(
            functools.partial(_dsa_prompt_kernel, n_kb=n_kb),
            out_shape=jax.ShapeDtypeStruct((bsz, t, TOK_WIDTH), F32),
            grid=(bsz, tiles),
            in_specs=[pl.BlockSpec((1, Q_TILE, TOK_WIDTH), lambda b, i, tile=tile: (b, tile(b, i), 0)),
                      pl.BlockSpec((1, Q_TILE, IDX_HEADS * IDX_DIM), lambda b, i, tile=tile: (b, tile(b, i), 0)),
                      pl.BlockSpec((1, IDX_HEADS, Q_TILE), lambda b, i, tile=tile: (b, 0, tile(b, i))),
                      keys(2 * LANES), keys(kvw),
                      pl.BlockSpec((1, n_kb, kvw, KEY_BLOCK), lambda b, i: (b, 0, 0, 0)),
                      pl.BlockSpec(memory_space=pl.ANY)],
            out_specs=pl.BlockSpec((1, Q_TILE, TOK_WIDTH), lambda b, i, tile=tile: (b, tile(b, i), 0)),
            scratch_shapes=[pltpu.VMEM((n_kb * KEY_BLOCK, Q_TILE), F32),
                            pltpu.VMEM((IDX_HEADS // 2, 2 * Q_TILE, 2 * LANES), BF16),
                            pltpu.VMEM((N_Q_HEADS * Q_TILE, N_KV_HEADS * HEAD_DIM), BF16),
                            pltpu.VMEM((N_Q_HEADS, HEAD_DIM, Q_TILE), F32)],
            input_output_aliases={6: 0},
            compiler_params=_params("parallel", "arbitrary"),
            name=f"dsa_prompt_{n_kb}",
        )(q, iq, iwt, ikb, kb, vt, out)
    return out


SAMPLE_ROWS = 16
SAMPLE_T = 4
N_PAGES = PAST_LEN // PAGE_SIZE
N_KEY_PAGES = N_PAGES + 1


def _sample_score_kernel(pt_ref, iq4_ref, iw_ref, ikn_ref, *rest):
    page_refs = rest[:SCORE_PAGES_PER_STEP]
    (score_ref,) = rest[SCORE_PAGES_PER_STEP:]
    step = pl.program_id(1)
    iq_hi, iq_lo = _split_bf16(iq4_ref[0])
    iq_parts = jnp.where(lax.broadcasted_iota(I32, iq_hi.shape, 1) < 2 * IDX_DIM, iq_hi, iq_lo)
    w = iw_ref[0]

    def page_scores(ik_t, page, allowed):
        k_hi, k_lo = _split_bf16(ik_t)
        d = jnp.dot(iq_parts, jnp.concatenate([k_hi, k_lo, k_hi, k_lo], axis=0), preferred_element_type=F32)
        sc = jnp.sum((jnp.maximum(d, 0.0) * w).reshape(SAMPLE_T, IDX_HEADS, PAGE_SIZE), axis=1)
        if allowed is not None:
            sc = jnp.where(allowed, sc, -jnp.inf)
        for r in range(SAMPLE_ROWS // SAMPLE_T):
            score_ref[0, page, r * SAMPLE_T:(r + 1) * SAMPLE_T, :] = sc

    for i in range(SCORE_PAGES_PER_STEP):
        page_scores(page_refs[i][0], step * SCORE_PAGES_PER_STEP + i, None)

    @pl.when(step == pl.num_programs(1) - 1)
    def _():
        new_idx = lax.broadcasted_iota(I32, (SAMPLE_T, PAGE_SIZE), 1)
        page_scores(ikn_ref[0], N_PAGES, new_idx <= lax.broadcasted_iota(I32, (SAMPLE_T, PAGE_SIZE), 0))


def _sample_scores(page_table, iq4, iw_col, ik_new_t, pool_ik_t):
    bsz = iq4.shape[0]
    page_spec = lambda i: pl.BlockSpec((1, IDX_DIM, PAGE_SIZE),
                                       lambda b, s, pt, i=i: (pt[b, s * SCORE_PAGES_PER_STEP + i], 0, 0))
    per_b = lambda shape: pl.BlockSpec((1,) + shape, lambda b, s, pt: (b,) + (0,) * len(shape))
    grid_spec = pltpu.PrefetchScalarGridSpec(
        num_scalar_prefetch=1,
        grid=(bsz, N_PAGES // SCORE_PAGES_PER_STEP),
        in_specs=[per_b((SAMPLE_T * IDX_HEADS, 4 * IDX_DIM)), per_b((SAMPLE_T * IDX_HEADS, 1)),
                  per_b((IDX_DIM, PAGE_SIZE))] + [page_spec(i) for i in range(SCORE_PAGES_PER_STEP)],
        out_specs=per_b((N_KEY_PAGES, SAMPLE_ROWS, PAGE_SIZE)))
    return pl.pallas_call(
        _sample_score_kernel,
        out_shape=jax.ShapeDtypeStruct((bsz, N_KEY_PAGES, SAMPLE_ROWS, PAGE_SIZE), F32),
        grid_spec=grid_spec,
        compiler_params=_params("parallel", "arbitrary"),
        name="dsa_sample_scores",
    )(page_table, iq4, iw_col, ik_new_t, *([pool_ik_t] * SCORE_PAGES_PER_STEP))


def _sample_select_kernel(score_ref, thr_ref, cut_ref):
    n_q, n_k = score_ref.shape
    key_idx = lax.broadcasted_iota(I32, (1, n_k), 1)

    def count(pred):
        return jnp.sum(pred(score_ref[...], key_idx).astype(I32), axis=-1, keepdims=True)

    thr, cut = _topk_threshold(
        count_ge=lambda c: count(lambda s, idx: s >= c),
        count_gt=lambda c: count(lambda s, idx: s > c),
        count_eq_before=lambda t, j: count(lambda s, idx: jnp.logical_and(s == t, idx < j)),
        any_true=lambda m: jnp.max(m.astype(I32)) > 0,
        shape=(n_q, 1), idx_bits=14)
    thr_ref[...] = jnp.broadcast_to(thr, thr_ref.shape)
    cut_ref[...] = jnp.broadcast_to(cut, cut_ref.shape)


def _sample_select(scores2d):
    n_q, n_k = scores2d.shape
    whole = lambda shape: pl.BlockSpec(shape, lambda i: (0, 0))
    return pl.pallas_call(
        _sample_select_kernel,
        out_shape=(jax.ShapeDtypeStruct((n_q, LANES), F32), jax.ShapeDtypeStruct((n_q, LANES), I32)),
        grid=(1,),
        in_specs=[whole((n_q, n_k))],
        out_specs=(whole((n_q, LANES)), whole((n_q, LANES))),
        compiler_params=_params("arbitrary"),
        name="dsa_sample_select",
    )(scores2d)


def _sample_attn_kernel(pt_ref, q_ref, sc_ref, thr_ref, cut_ref, kn_ref, vn_ref, *rest):
    k_pages = rest[:ATTN_PAGES_PER_STEP]
    v_pages = rest[ATTN_PAGES_PER_STEP:2 * ATTN_PAGES_PER_STEP]
    o_ref, m_ref, l_ref, acc_ref = rest[2 * ATTN_PAGES_PER_STEP:]
    step = pl.program_id(1)

    @pl.when(step == 0)
    def _():
        m_ref[...] = jnp.full(m_ref.shape, NEG_BIG, F32)
        l_ref[...] = jnp.zeros(l_ref.shape, F32)
        acc_ref[...] = jnp.zeros(acc_ref.shape, F32)

    thr = thr_ref[0]
    cut = cut_ref[0]
    lane = lax.broadcasted_iota(I32, (SAMPLE_ROWS, PAGE_SIZE), 1)

    def attend(pages):
        sels = []
        for page, _, _ in pages:
            sc = sc_ref[0, page]
            sels.append(jnp.logical_or(sc > thr, jnp.logical_and(sc == thr, page * PAGE_SIZE + lane <= cut)))
        cat = lambda parts, axis: parts[0] if len(parts) == 1 else jnp.concatenate(parts, axis=axis)
        sel = cat(sels, 1)
        for g in range(N_KV_HEADS):
            rows_g = pl.ds(g, PAGE_SIZE, stride=N_KV_HEADS)
            k_all = cat([k[0, rows_g, :].astype(BF16) for _, k, _ in pages], 0)
            v_all = cat([v[0, rows_g, :].astype(BF16) for _, _, v in pages], 0)
            s = jnp.where(sel, _dot_nt(q_ref[0, g], k_all), NEG_BIG)
            m_old = m_ref[g]
            m_new = jnp.maximum(m_old, jnp.max(s, axis=-1, keepdims=True))
            alpha = jnp.exp2(m_old - m_new)
            p = jnp.exp2(s - m_new)
            l_ref[g] = alpha * l_ref[g] + jnp.sum(p, axis=-1, keepdims=True)
            acc_ref[g] = alpha * acc_ref[g] + jnp.dot(p.astype(BF16), v_all, preferred_element_type=F32)
            m_ref[g] = m_new

    attend([(step * ATTN_PAGES_PER_STEP + i, k_pages[i], v_pages[i]) for i in range(ATTN_PAGES_PER_STEP)])

    @pl.when(step == pl.num_programs(1) - 1)
    def _():
        attend([(N_PAGES, kn_ref, vn_ref)])
        for g in range(N_KV_HEADS):
            o_ref[0, g] = acc_ref[g] / l_ref[g]


def _sample_attn(page_table, q16, scores, thr, cut, k_new_pad, v_new_pad, pool_k, pool_v):
    bsz = q16.shape[0]
    page_rows = PAGE_SIZE * N_KV_HEADS
    page_spec = lambda i: pl.BlockSpec((1, page_rows, HEAD_DIM),
                                       lambda b, s, pt, i=i: (pt[b, s * ATTN_PAGES_PER_STEP + i], 0, 0))
    per_b = lambda shape: pl.BlockSpec((1,) + shape, lambda b, s, pt: (b,) + (0,) * len(shape))
    pages = [page_spec(i) for i in range(ATTN_PAGES_PER_STEP)]
    grid_spec = pltpu.PrefetchScalarGridSpec(
        num_scalar_prefetch=1,
        grid=(bsz, N_PAGES // ATTN_PAGES_PER_STEP),
        in_specs=[per_b((N_KV_HEADS, SAMPLE_ROWS, HEAD_DIM)), per_b((N_KEY_PAGES, SAMPLE_ROWS, PAGE_SIZE)),
                  per_b((SAMPLE_ROWS, LANES)), per_b((SAMPLE_ROWS, LANES)),
                  per_b((page_rows, HEAD_DIM)), per_b((page_rows, HEAD_DIM))] + pages + pages,
        out_specs=per_b((N_KV_HEADS, SAMPLE_ROWS, HEAD_DIM)),
        scratch_shapes=[pltpu.VMEM((N_KV_HEADS, SAMPLE_ROWS, 1), F32),
                        pltpu.VMEM((N_KV_HEADS, SAMPLE_ROWS, 1), F32),
                        pltpu.VMEM((N_KV_HEADS, SAMPLE_ROWS, HEAD_DIM), F32)])
    return pl.pallas_call(
        _sample_attn_kernel,
        out_shape=jax.ShapeDtypeStruct((bsz, N_KV_HEADS, SAMPLE_ROWS, HEAD_DIM), F32),
        grid_spec=grid_spec,
        compiler_params=_params("parallel", "arbitrary"),
        name="dsa_sample_attn",
    )(page_table, q16, scores, thr, cut, k_new_pad, v_new_pad,
      *([pool_k] * ATTN_PAGES_PER_STEP), *([pool_v] * ATTN_PAGES_PER_STEP))


def _pad_cols(w, width):
    return jnp.pad(w, ((0, 0), (0, width - w.shape[1])))


def _dsa_weight(w):
    sizes = (TOK_WIDTH, N_KV_HEADS * HEAD_DIM, N_KV_HEADS * HEAD_DIM, IDX_HEADS * IDX_DIM, IDX_DIM, IDX_HEADS, MEM_WIDTH)
    offs = [0]
    for s in sizes:
        offs.append(offs[-1] + s)
    q, k, v, iq, ik, iw, mq = [w[:, offs[i]:offs[i + 1]] for i in range(len(sizes))]
    return jnp.concatenate([q, k, v, iq, ik, ik, _pad_cols(iw, LANES), mq], axis=1).astype(BF16)


def kernel(x_prompt, x_sample, cache_k, cache_v, cache_idx_k, cache_mem_k, cache_mem_v, state_hgrn, state_ffn_conv, page_table, mem_prompt, norm_mix, norm_ffn, w_in_hgrn, hgrn_lb_logits, hgrn_out_norm, w_in_dsa, dsa_q_norm, dsa_k_norm, idx_k_norm, w_mem_kv, mem_q_norm, mem_k_norm, w_out, w_ffn_gate, w_ffn_up, ffn_conv_w, ffn_conv_b, w_ffn_down):
    bp, seq, _ = x_prompt.shape
    bs, t_s, _ = x_sample.shape
    assert t_s == SAMPLE_T and seq % KEY_BLOCK == 0
    n_s = bs * SAMPLE_T_PAD
    lower_bounds = jnp.cumsum(jax.nn.softmax(hgrn_lb_logits.astype(F32), axis=0), axis=0)

    xp = x_prompt
    xs = jnp.pad(x_sample, ((0, 0), (0, SAMPLE_T_PAD - t_s), (0, 0)))
    mem2d = mem_prompt.reshape(bp * N_MEM, D_MODEL)
    tabs_p = _rope_tables(jnp.arange(seq))
    pos_s = PAST_LEN + jnp.arange(SAMPLE_T_PAD)
    tabs_s = tuple(jnp.tile(tb, (bs, 1)) for tb in _rope_tables(pos_s))
    row_t = jnp.arange(n_s) % SAMPLE_T_PAD

    hgrn_p, hgrn_s = [], []
    kp_l, vp_l, ikp_l, ks_l, vs_l, iks_l = [], [], [], [], [], []
    mk_l, mv_l, cvp_l, cvs_l = [], [], [], []
    for layer in range(DEPTH):
        j = layer // 2
        if layer % 2 == 0:
            w_in = w_in_hgrn[j].astype(BF16)
            cols = w_in.shape[1]
            mq_col = HGRN_MEM0 // MEM_WIDTH
            mqp = _norm_proj(xp.reshape(bp * seq, D_MODEL), norm_mix[layer], w_in, 256).reshape(bp, seq, cols)
            mqs = _norm_proj(xs.reshape(n_s, D_MODEL), norm_mix[layer], w_in, n_s).reshape(bs, SAMPLE_T_PAD, cols)
            hgrn_args = (lower_bounds[j], hgrn_out_norm[j])
            s0 = jnp.zeros((bp, HGRN_HEADS, HGRN_DK, HGRN_DK), F32)
            tp, sp = _hgrn(mqp, *hgrn_args, s0, HGRN_CHUNK, HGRN_SUB, 512, HGRN_CHUNK)
            ts, ss = _hgrn(mqs, *hgrn_args, state_hgrn[j], SAMPLE_T_PAD, SAMPLE_T_PAD,
                           SAMPLE_T_PAD, math.gcd(t_s, HGRN_CHUNK))
            hgrn_p.append(sp.astype(state_hgrn.dtype))
            hgrn_s.append(ss.astype(state_hgrn.dtype))
        else:
            mq_col = 0
            prep_args = (norm_mix[layer], _dsa_weight(w_in_dsa[j]))
            gains = (dsa_q_norm[j], dsa_k_norm[j], idx_k_norm[j])
            qp, kp, kpb, vp, vtp, iqp, ikp, ikpb, iwtp, mqp = _dsa_prep(xp, *prep_args, tabs_p, *gains, KEY_BLOCK)
            tp = _dsa_prompt(qp, iqp, iwtp, ikpb, kpb, vtp)
            kp_l.append(kp.reshape(bp, seq, N_KV_HEADS, HEAD_DIM))
            vp_l.append(vp.reshape(bp, seq, N_KV_HEADS, HEAD_DIM))
            ikp_l.append(ikp[..., :IDX_DIM])

            qs, kn, _, vn, _, iqs, ikn, _, iwts, mqs = _dsa_prep(
                xs.reshape(1, n_s, D_MODEL), *prep_args, tabs_s, *gains, n_s)
            mqs = mqs.reshape(bs, SAMPLE_T_PAD, MEM_WIDTH)
            sel_t = lambda a: a.reshape(bs, SAMPLE_T_PAD, -1)[:, :t_s]
            kn, vn, ikn = sel_t(kn), sel_t(vn), sel_t(ikn)[..., :IDX_DIM]
            ks_l.append(kn.reshape(bs, t_s, N_KV_HEADS, HEAD_DIM))
            vs_l.append(vn.reshape(bs, t_s, N_KV_HEADS, HEAD_DIM))
            iks_l.append(ikn)
            iq4 = jnp.tile(sel_t(iqs).reshape(bs, t_s * IDX_HEADS, IDX_DIM), (1, 1, 4))
            iw_col = sel_t(iwts[0].T).reshape(bs, t_s * IDX_HEADS, 1)
            ikn_t = jnp.pad(ikn.transpose(0, 2, 1), ((0, 0), (0, 0), (0, PAGE_SIZE - t_s)))
            scores = _sample_scores(page_table, iq4, iw_col, ikn_t, cache_idx_k[j].transpose(0, 2, 1))
            per_query = scores[:, :, :t_s, :].transpose(0, 2, 1, 3).reshape(bs * t_s, N_KEY_PAGES * PAGE_SIZE)
            thr, cut = _sample_select(per_query)
            rep = lambda a: jnp.tile(a.reshape(bs, t_s, LANES), (1, SAMPLE_ROWS // t_s, 1))
            thr, cut = rep(thr), rep(cut)
            q16 = sel_t(qs).reshape(bs, t_s, N_KV_HEADS, Q_PER_KV, HEAD_DIM).transpose(0, 2, 3, 1, 4)
            q16 = jnp.pad(q16.reshape(bs, N_KV_HEADS, Q_PER_KV * t_s, HEAD_DIM),
                          ((0, 0), (0, 0), (0, SAMPLE_ROWS - Q_PER_KV * t_s), (0, 0)))
            page_rows = lambda a: a.reshape(a.shape[0], -1, HEAD_DIM)
            pad_page = lambda a: jnp.pad(page_rows(a), ((0, 0), (0, (PAGE_SIZE - t_s) * N_KV_HEADS), (0, 0)))
            o16 = _sample_attn(page_table, q16, scores, thr, cut, pad_page(kn), pad_page(vn),
                               page_rows(cache_k[j]), page_rows(cache_v[j]))
            o = o16[:, :, :Q_PER_KV * t_s].reshape(bs, N_KV_HEADS, Q_PER_KV, t_s, HEAD_DIM)
            ts = jnp.pad(o.transpose(0, 3, 1, 2, 4).reshape(bs, t_s, TOK_WIDTH),
                         ((0, 0), (0, SAMPLE_T_PAD - t_s), (0, 0)))

        mkp, mvp = _mem_kv(mem2d, w_mem_kv[layer].astype(BF16), mem_k_norm[layer])
        mk_l.append(mkp.reshape(bp, N_MEM, MEM_HEADS, MEM_HD))
        mv_l.append(mvp.reshape(bp, N_MEM, MEM_HEADS, MEM_HD))
        cp = _mem_attn(mqp, mq_col, mem_q_norm[layer], mkp.reshape(bp, N_MEM, MEM_WIDTH),
                       mvp.reshape(bp, N_MEM, MEM_WIDTH), 512)
        cs = _mem_attn(mqs, mq_col, mem_q_norm[layer], cache_mem_k[layer].reshape(bs, N_MEM, MEM_WIDTH),
                       cache_mem_v[layer].reshape(bs, N_MEM, MEM_WIDTH), SAMPLE_T_PAD)

        w_o = w_out[layer].astype(BF16)
        n_p = bp * seq
        xp = _out_proj(xp.reshape(n_p, D_MODEL), tp.reshape(n_p, TOK_WIDTH), cp.reshape(n_p, MEM_WIDTH),
                       w_o, 512).reshape(bp, seq, D_MODEL)
        xs2 = _out_proj(xs.reshape(n_s, D_MODEL), ts.reshape(n_s, TOK_WIDTH), cs.reshape(n_s, MEM_WIDTH), w_o, n_s)

        wg, wu, wd = (w.astype(BF16) for w in (w_ffn_gate[layer], w_ffn_up[layer], w_ffn_down[layer]))
        ffn_w = (norm_ffn[layer], wg, wu, ffn_conv_w[layer], ffn_conv_b[layer], wd)
        xp, u_last = _ffn_long(xp, *ffn_w, jnp.zeros((bp, CONV_W - 1, D_FF), F32), 256)
        cvp_l.append(u_last[:, SUBLANES - (CONV_W - 1):, :])
        st = state_ffn_conv[layer]
        zrow = jnp.zeros((bs, SAMPLE_T_PAD - 1, D_FF), F32)
        p1 = jnp.concatenate([st[:, 1:2], zrow], axis=1).reshape(n_s, D_FF)
        p2 = jnp.concatenate([st, zrow[:, 1:]], axis=1).reshape(n_s, D_FF)
        xs2, u_s = _ffn_short(xs2, *ffn_w, p1, p2, SAMPLE_T_PAD)
        cvs_l.append(u_s.reshape(bs, SAMPLE_T_PAD, D_FF)[:, t_s - (CONV_W - 1):t_s])
        xs = jnp.where((row_t < t_s)[:, None], xs2, 0.0).reshape(bs, SAMPLE_T_PAD, D_MODEL)

    return (xp, xs[:, :t_s],
            jnp.stack(hgrn_p), jnp.stack(hgrn_s),
            jnp.stack(kp_l), jnp.stack(vp_l), jnp.stack(ikp_l),
            jnp.stack(ks_l), jnp.stack(vs_l), jnp.stack(iks_l),
            jnp.stack(mk_l), jnp.stack(mv_l),
            jnp.stack(cvp_l), jnp.stack(cvs_l))
```

```python
import functools
import math

import jax
import jax.numpy as jnp
from jax import lax
from jax.experimental import pallas as pl
from jax.experimental.pallas import tpu as pltpu

D_MODEL = 1024
DEPTH = 2
PAST_LEN = 8192
PAGE_SIZE = 128
MEM_WIDTH = 256
TOK_WIDTH = 768
MEM_HEADS = 4
MEM_HD = 64
N_MEM = 256
HGRN_DK = 128
HGRN_HEADS = 6
HGRN_CHUNK = 32
HEAD_DIM = 128
N_Q_HEADS = 6
N_KV_HEADS = 2
Q_PER_KV = N_Q_HEADS // N_KV_HEADS
IDX_HEADS = 8
IDX_DIM = 64
TOPK = 256
ROPE_THETA = 10000.0
D_FF = 2816
CONV_W = 3
EPS = 1e-6

LANES = 128
SUBLANES = 8
VMEM_LIMIT_BYTES = 56 * 1024 * 1024
FF_CHUNK = 256
HGRN_SUB = 256
PROJ_CHUNK = 256
KEY_BLOCK = 512
Q_TILE = 128
ATTN_SUB = 512
SAMPLE_T_PAD = 8
SCORE_PAGES_PER_STEP = 32
ATTN_PAGES_PER_STEP = 16
NEG_BIG = -1e30
QK_SCALE_LOG2E = HEAD_DIM ** -0.5 * math.log2(math.e)
F32 = jnp.float32
BF16 = jnp.bfloat16
I32 = jnp.int32
INT_MIN = -2 ** 31

DSA_Q0, DSA_K0, DSA_V0, DSA_IQ0, DSA_IK0, DSA_IW0, DSA_MEM0, DSA_COLS_PAD = 0, 768, 1024, 1280, 1792, 1920, 2048, 2304
HGRN_MEM0 = 4 * TOK_WIDTH


def _params(*sem):
    return pltpu.CompilerParams(dimension_semantics=sem, vmem_limit_bytes=VMEM_LIMIT_BYTES)


def _rms(x, gain):
    ms = jnp.mean(x * x, axis=-1, keepdims=True)
    return x * lax.rsqrt(ms + EPS) * gain


def _split_bf16(x):
    hi = x.astype(BF16)
    lo = (x - hi.astype(F32)).astype(BF16)
    return hi, lo


def _group_mean_sq(x, group):
    n = x.shape[-1]
    r = lax.broadcasted_iota(I32, (n, n), 0) // group
    c = lax.broadcasted_iota(I32, (n, n), 1) // group
    ones = jnp.where(r == c, 1.0, 0.0).astype(BF16)
    hi, lo = _split_bf16(x * x)
    s = jnp.dot(hi, ones, preferred_element_type=F32) + jnp.dot(lo, ones, preferred_element_type=F32)
    return s * (1.0 / group)


def _dot_nt(a, b):
    return lax.dot_general(a, b, (((1,), (1,)), ((), ())), preferred_element_type=F32)


def _dot_tn(a, b):
    return lax.dot_general(a, b, (((0,), (0,)), ((), ())), preferred_element_type=F32)


def _silu(x):
    return x * jax.nn.sigmoid(x)


def _norm_proj_kernel(x_ref, g_ref, w_ref, o_ref):
    h = _rms(x_ref[...], g_ref[...]).astype(BF16)
    o_ref[...] = jnp.dot(h, w_ref[...], preferred_element_type=F32)


def _norm_proj(x2d, gain, w_bf16, tm):
    n, d = x2d.shape
    c = w_bf16.shape[1]
    return pl.pallas_call(
        _norm_proj_kernel,
        out_shape=jax.ShapeDtypeStruct((n, c), F32),
        grid=(n // tm,),
        in_specs=[pl.BlockSpec((tm, d), lambda i: (i, 0)),
                  pl.BlockSpec((1, d), lambda i: (0, 0)),
                  pl.BlockSpec((d, c), lambda i: (0, 0))],
        out_specs=pl.BlockSpec((tm, c), lambda i: (i, 0)),
        compiler_params=_params("parallel"),
        name="norm_proj",
    )(x2d, gain.reshape(1, d), w_bf16)


def _hgrn_kernel(zin_ref, lb_ref, on_ref, s0_ref, o_ref, so_ref, st_ref,
                 *, chunk, sub, n_sub, valid_rows):
    step = pl.program_id(1)

    @pl.when(step == 0)
    def _():
        for h in range(HGRN_HEADS):
            st_ref[h] = s0_ref[0, h].T

    z_ref = zin_ref.at[0]
    zcol = lambda part, h: slice(part * TOK_WIDTH + h * HGRN_DK, part * TOK_WIDTH + (h + 1) * HGRN_DK)

    row = lax.broadcasted_iota(I32, (sub, sub), 0)
    col = lax.broadcasted_iota(I32, (sub, sub), 1)
    same_chunk = (row // chunk) == (col // chunk)
    causal = jnp.logical_and(same_chunk, row >= col)
    sum_lhs = jnp.concatenate([jnp.where(causal, 1.0, 0.0), jnp.where(same_chunk, 1.0, 0.0)], axis=0).astype(BF16)
    row_id = lax.broadcasted_iota(I32, (sub, HGRN_DK), 0)
    row_ok = row_id % chunk < valid_rows
    n_c = sub // chunk
    in_chunk = [row_id // chunk == c for c in range(n_c)]

    def body(si, carry):
        rows = pl.ds(pl.multiple_of(si * sub, sub), sub)
        for h in range(HGRN_HEADS):
            sl = slice(h * HGRN_DK, (h + 1) * HGRN_DK)
            lb = lb_ref[:, sl]
            q = _silu(z_ref[rows, zcol(0, h)])
            f = lb + (1.0 - lb) * jax.nn.sigmoid(z_ref[rows, zcol(1, h)])
            if valid_rows < chunk:
                f = jnp.where(row_ok, f, 1.0)
            k = 1.0 - f
            vb = z_ref[rows, zcol(2, h)].astype(BF16)
            g_hi, g_lo = _split_bf16(jnp.log(f))
            sums = jnp.dot(sum_lhs, jnp.concatenate([g_hi, g_lo], axis=1), preferred_element_type=F32)
            b = sums[:sub, :HGRN_DK] + sums[:sub, HGRN_DK:]
            b_chunk = sums[sub:, :HGRN_DK] + sums[sub:, HGRN_DK:]
            q_dec = q * jnp.exp(b)
            k_dec = (k * jnp.exp(-b)).astype(BF16)
            k_rem = k * jnp.exp(b_chunk - b)
            chunk_decay = jnp.exp(b_chunk)
            scores = jnp.where(causal, _dot_nt(q_dec.astype(BF16), k_dec), 0.0)
            o = jnp.dot(scores.astype(BF16), vb, preferred_element_type=F32)
            spread = lambda a: jnp.concatenate([jnp.where(in_chunk[c], a, 0.0) for c in range(n_c)],
                                               axis=1).astype(BF16)
            incr = _dot_tn(vb, spread(k_rem))
            st = st_ref[h]
            states = []
            for c in range(n_c):
                states.append(st.astype(BF16))
                st = st * chunk_decay[c * chunk:c * chunk + 1] + incr[:, c * HGRN_DK:(c + 1) * HGRN_DK]
            st_ref[h] = st
            o = o + _dot_nt(spread(q_dec), jnp.concatenate(states, axis=1))
            o_ref[0, rows, sl] = _rms(o, on_ref[...]) * _silu(z_ref[rows, zcol(3, h)])
        return carry

    lax.fori_loop(0, n_sub, body, 0)

    @pl.when(step == pl.num_programs(1) - 1)
    def _():
        for h in range(HGRN_HEADS):
            so_ref[0, h] = st_ref[h].T


def _hgrn(z3d, lb, out_norm, s0, chunk, sub, t_step, valid_rows):
    bsz, t, _ = z3d.shape
    const = lambda shape: pl.BlockSpec(shape, lambda b, i: (0,) * len(shape))
    rows = lambda w: pl.BlockSpec((1, t_step, w), lambda b, i: (b, i, 0))
    state_spec = pl.BlockSpec((1, HGRN_HEADS, HGRN_DK, HGRN_DK), lambda b, i: (b, 0, 0, 0))
    kern = functools.partial(_hgrn_kernel, chunk=chunk, sub=sub, n_sub=t_step // sub, valid_rows=valid_rows)
    return pl.pallas_call(
        kern,
        out_shape=(jax.ShapeDtypeStruct((bsz, t, TOK_WIDTH), F32),
                   jax.ShapeDtypeStruct((bsz, HGRN_HEADS, HGRN_DK, HGRN_DK), F32)),
        grid=(bsz, t // t_step),
        in_specs=[rows(4 * TOK_WIDTH), const((1, TOK_WIDTH)), const((1, HGRN_DK)), state_spec],
        out_specs=(rows(TOK_WIDTH), state_spec),
        scratch_shapes=[pltpu.VMEM((HGRN_HEADS, HGRN_DK, HGRN_DK), F32)],
        compiler_params=_params("parallel", "arbitrary"),
        name="hgrn2",
    )(z3d, lb.reshape(1, TOK_WIDTH), out_norm.reshape(1, HGRN_DK), s0)


def _mem_kv_kernel(m_ref, w_ref, g_ref, k_ref, v_ref):
    kv = jnp.dot(m_ref[...].astype(BF16), w_ref[...], preferred_element_type=F32)
    k = kv[:, :MEM_WIDTH]
    k_ref[...] = k * lax.rsqrt(_group_mean_sq(k, MEM_HD) + EPS) * g_ref[...]
    v_ref[...] = kv[:, MEM_WIDTH:]


def _mem_kv(mem2d, w_bf16, k_gain):
    n = mem2d.shape[0]
    out = jax.ShapeDtypeStruct((n, MEM_WIDTH), F32)
    return pl.pallas_call(
        _mem_kv_kernel,
        out_shape=(out, out),
        grid=(n // N_MEM,),
        in_specs=[pl.BlockSpec((N_MEM, D_MODEL), lambda i: (i, 0)),
                  pl.BlockSpec((D_MODEL, 2 * MEM_WIDTH), lambda i: (0, 0)),
                  pl.BlockSpec((1, MEM_WIDTH), lambda i: (0, 0))],
        out_specs=(pl.BlockSpec((N_MEM, MEM_WIDTH), lambda i: (i, 0)),
                   pl.BlockSpec((N_MEM, MEM_WIDTH), lambda i: (i, 0))),
        compiler_params=_params("parallel"),
        name="mem_kv",
    )(mem2d, w_bf16, jnp.tile(k_gain, MEM_HEADS).reshape(1, MEM_WIDTH))


def _mem_attn_out_kernel(zq_ref, g_ref, mk_ref, mv_ref, x_ref, t_ref, wt_ref, wc_ref, o_ref, *, kv_transposed):
    q = zq_ref[0]
    qn = q * lax.rsqrt(_group_mean_sq(q, MEM_HD) + EPS) * g_ref[...]
    mk = mk_ref[0].astype(BF16)
    mv = mv_ref[0].astype(BF16)
    head_of_lane = lax.broadcasted_iota(I32, qn.shape, 1) // MEM_HD
    c = jnp.zeros(qn.shape, F32)
    for h in range(MEM_HEADS):
        mine = head_of_lane == h
        qh = jnp.where(mine, qn, 0.0).astype(BF16)
        s = (jnp.dot(qh, mk, preferred_element_type=F32) if kv_transposed else _dot_nt(qh, mk)) * (MEM_HD ** -0.5)
        p = jnp.exp(s - jnp.max(s, axis=-1, keepdims=True))
        pb = p.astype(BF16)
        oh = _dot_nt(pb, mv) if kv_transposed else jnp.dot(pb, mv, preferred_element_type=F32)
        c = jnp.where(mine, oh / jnp.sum(p, axis=-1, keepdims=True), c)
    o_ref[0] = (x_ref[0]
                + jnp.dot(t_ref[0].astype(BF16), wt_ref[...], preferred_element_type=F32)
                + jnp.dot(c.astype(BF16), wc_ref[...], preferred_element_type=F32))


def _mem_attn_out(zq3d, q_col_block, q_gain, mk, mv, layer, kv_transposed, x3d, t3d, w_out, tm):
    bsz, t, _ = x3d.shape
    rows = lambda w, col=0: pl.BlockSpec((1, tm, w), lambda b, i: (b, i, col))
    per_b = pl.BlockSpec((None, 1, N_MEM, MEM_WIDTH), lambda b, i: (layer, b, 0, 0))
    const = lambda shape: pl.BlockSpec(shape, lambda b, i: (0,) * len(shape))
    return pl.pallas_call(
        functools.partial(_mem_attn_out_kernel, kv_transposed=kv_transposed),
        out_shape=jax.ShapeDtypeStruct((bsz, t, D_MODEL), F32),
        grid=(bsz, t // tm),
        in_specs=[rows(MEM_WIDTH, q_col_block), const((1, MEM_WIDTH)), per_b, per_b,
                  rows(D_MODEL), rows(TOK_WIDTH), const((TOK_WIDTH, D_MODEL)), const((MEM_WIDTH, D_MODEL))],
        out_specs=rows(D_MODEL),
        compiler_params=_params("parallel", "parallel"),
        name="mem_attn_out",
    )(zq3d, jnp.tile(q_gain, MEM_HEADS).reshape(1, MEM_WIDTH), mk, mv, x3d, t3d,
      w_out[:TOK_WIDTH], w_out[TOK_WIDTH:])


def _ffn_kernel(x_ref, g_ref, wg_ref, wu_ref, cw_ref, cb_ref, wd_ref, p1_ref, p2_ref,
                y_ref, u_ref, carry_ref, *, seq_rows, carry):
    tm = x_ref.shape[0]
    x = x_ref[...]
    h = _rms(x, g_ref[...]).astype(BF16)
    t_in_seq = lax.broadcasted_iota(I32, (tm, FF_CHUNK), 0) % seq_rows
    first_block = pl.program_id(1) == 0 if carry else None
    n_chunks = D_FF // FF_CHUNK

    def project(c):
        cs = slice(c * FF_CHUNK, (c + 1) * FF_CHUNK)
        return (jnp.dot(h, wg_ref[:, cs], preferred_element_type=F32),
                jnp.dot(h, wu_ref[:, cs], preferred_element_type=F32))

    acc = x
    ahead = project(0)
    for c in range(n_chunks):
        cs = slice(c * FF_CHUNK, (c + 1) * FF_CHUNK)
        u, up = ahead
        if c + 1 < n_chunks:
            ahead = project(c + 1)
        u1 = pltpu.roll(u, 1, 0)
        u2 = pltpu.roll(u, 2, 0)
        if carry:
            prev = carry_ref[:, cs]
            start1 = jnp.where(first_block, p1_ref[0:1, cs], prev[7:8, :])
            start2a = jnp.where(first_block, p2_ref[0:1, cs], prev[6:7, :])
            start2b = jnp.where(first_block, p2_ref[1:2, cs], prev[7:8, :])
            u1 = jnp.where(t_in_seq == 0, start1, u1)
            u2 = jnp.where(t_in_seq == 0, start2a, jnp.where(t_in_seq == 1, start2b, u2))
            carry_ref[:, cs] = u[tm - SUBLANES:, :]
        else:
            u1 = jnp.where(t_in_seq == 0, p1_ref[:, cs], u1)
            u2 = jnp.where(t_in_seq < 2, p2_ref[:, cs], u2)
        conv = cb_ref[:, cs] + u2 * cw_ref[0:1, cs] + u1 * cw_ref[1:2, cs] + u * cw_ref[2:3, cs]
        a = (_silu(conv) * up).astype(BF16)
        acc = acc + jnp.dot(a, wd_ref[cs, :], preferred_element_type=F32)
        if carry:
            u_ref[0, :, cs] = u[tm - SUBLANES:, :]
        else:
            u_ref[:, cs] = u
    y_ref[...] = acc


def _ffn_weights_specs(idx):
    once = pl.Buffered(1)
    return [pl.BlockSpec((1, D_MODEL), idx),
            pl.BlockSpec((D_MODEL, D_FF), idx, pipeline_mode=once),
            pl.BlockSpec((D_MODEL, D_FF), idx, pipeline_mode=once),
            pl.BlockSpec((CONV_W, D_FF), idx),
            pl.BlockSpec((1, D_FF), idx),
            pl.BlockSpec((D_FF, D_MODEL), idx, pipeline_mode=once)]


def _ffn_long(x3d, gain, wg, wu, cw, cb, wd, state, tm):
    bsz, t, _ = x3d.shape
    idx = lambda b, i: (0, 0)
    kern = functools.partial(_ffn_kernel, seq_rows=t, carry=True)
    nb = t // tm
    row = lambda w: pl.BlockSpec((tm, w), lambda b, i: (b * nb + i, 0))
    p1 = state[:, 1:2, :]
    y, u_last = pl.pallas_call(
        kern,
        out_shape=(jax.ShapeDtypeStruct((bsz * t, D_MODEL), F32),
                   jax.ShapeDtypeStruct((bsz, SUBLANES, D_FF), F32)),
        grid=(bsz, nb),
        in_specs=[row(D_MODEL)] + _ffn_weights_specs(idx)
                 + [pl.BlockSpec((None, 1, D_FF), lambda b, i: (b, 0, 0)),
                    pl.BlockSpec((None, 2, D_FF), lambda b, i: (b, 0, 0))],
        out_specs=(row(D_MODEL), pl.BlockSpec((1, SUBLANES, D_FF), lambda b, i: (b, 0, 0))),
        scratch_shapes=[pltpu.VMEM((SUBLANES, D_FF), F32)],
        compiler_params=_params("parallel", "arbitrary"),
        name="conv_ffn_long",
    )(x3d.reshape(bsz * t, D_MODEL), gain.reshape(1, D_MODEL), wg, wu, cw, cb.reshape(1, D_FF), wd, p1, state)
    return y.reshape(bsz, t, D_MODEL), u_last


def _ffn_short(x2d, gain, wg, wu, cw, cb, wd, p1, p2, seq_rows):
    n = x2d.shape[0]
    idx = lambda i: (0, 0)
    kern = functools.partial(_ffn_kernel, seq_rows=seq_rows, carry=False)
    row = lambda w: pl.BlockSpec((n, w), lambda i: (0, 0))
    return pl.pallas_call(
        kern,
        out_shape=(jax.ShapeDtypeStruct((n, D_MODEL), F32), jax.ShapeDtypeStruct((n, D_FF), F32)),
        grid=(1,),
        in_specs=[row(D_MODEL)] + _ffn_weights_specs(idx) + [row(D_FF), row(D_FF)],
        out_specs=(row(D_MODEL), row(D_FF)),
        scratch_shapes=[pltpu.VMEM((SUBLANES, D_FF), F32)],
        compiler_params=_params("arbitrary"),
        name="conv_ffn_short",
    )(x2d, gain.reshape(1, D_MODEL), wg, wu, cw, cb.reshape(1, D_FF), wd, p1, p2)


def _rope128(x, cos, sin_signed):
    return x * cos + pltpu.roll(x, HEAD_DIM // 2, 1) * sin_signed


def _rope64_pairs(x, cos, sin_lo, sin_hi):
    return (x * cos + pltpu.roll(x, LANES - IDX_DIM // 2, 1) * sin_lo
            + pltpu.roll(x, IDX_DIM // 2, 1) * sin_hi)


def _dsa_prep_kernel(x_ref, ng_ref, w_ref, c128_ref, s128_ref, c64_ref, s64lo_ref, s64hi_ref, qg_ref, kg_ref, ikg_ref,
                     q_ref, k_ref, kb_ref, v_ref, vt_ref, iq_ref, ik_ref, ikb_ref, iwt_ref, mq_ref, z_ref):
    hn = _rms(x_ref[0], ng_ref[...]).astype(BF16)
    for j in range(DSA_COLS_PAD // PROJ_CHUNK):
        cs = slice(j * PROJ_CHUNK, (j + 1) * PROJ_CHUNK)
        z_ref[0, :, cs] = jnp.dot(hn, w_ref[:, cs], preferred_element_type=F32)
    mq_ref[0] = z_ref[0, :, DSA_MEM0:]
    c128, s128 = c128_ref[...], s128_ref[...]
    c64, s64lo, s64hi = c64_ref[...], s64lo_ref[...], s64hi_ref[...]
    for h in range(N_Q_HEADS):
        sl = slice(DSA_Q0 + h * HEAD_DIM, DSA_Q0 + (h + 1) * HEAD_DIM)
        q = _rope128(_rms(z_ref[0, :, sl], qg_ref[...]), c128, s128)
        q_ref[0, :, h * HEAD_DIM:(h + 1) * HEAD_DIM] = (q * QK_SCALE_LOG2E).astype(BF16)
    for h in range(N_KV_HEADS):
        sl = slice(DSA_K0 + h * HEAD_DIM, DSA_K0 + (h + 1) * HEAD_DIM)
        k = _rope128(_rms(z_ref[0, :, sl], kg_ref[...]), c128, s128)
        k_ref[0, :, h * HEAD_DIM:(h + 1) * HEAD_DIM] = k
        kb_ref[0, :, h * HEAD_DIM:(h + 1) * HEAD_DIM] = k.astype(BF16)
    v = z_ref[0, :, DSA_V0:DSA_V0 + N_KV_HEADS * HEAD_DIM]
    v_ref[0] = v
    vt_ref[0, 0] = v.T.astype(BF16)
    for p in range(IDX_HEADS // 2):
        sl = slice(DSA_IQ0 + p * LANES, DSA_IQ0 + (p + 1) * LANES)
        iq_ref[0, :, p * LANES:(p + 1) * LANES] = _rope64_pairs(z_ref[0, :, sl], c64, s64lo, s64hi)
    ik = _rope64_pairs(_rms(z_ref[0, :, DSA_IK0:DSA_IK0 + LANES], ikg_ref[...]), c64, s64lo, s64hi)
    ik_ref[0] = ik
    ik_hi, ik_lo = _split_bf16(ik)
    hi_lo = jnp.where(lax.broadcasted_iota(I32, ik.shape, 1) < IDX_DIM, ik_hi, ik_lo)
    ikb_ref[0, :, :LANES] = hi_lo
    ikb_ref[0, :, LANES:] = hi_lo
    iw = z_ref[0, :, DSA_IW0:DSA_IW0 + LANES] * (IDX_HEADS ** -0.5 * IDX_DIM ** -0.5)
    iwt_ref[0] = iw.T[:IDX_HEADS, :]


def _dsa_prep(x3d, norm_gain, w_bf16, tabs, q_gain, k_gain, ik_gain, tm):
    bsz, t, _ = x3d.shape
    nb = t // tm
    tab = pl.BlockSpec((tm, LANES), lambda b, i: (i, 0))
    gain = pl.BlockSpec((1, LANES), lambda b, i: (0, 0))
    rows = lambda w: pl.BlockSpec((1, tm, w), lambda b, i: (b, i, 0))
    const = lambda shape, **kw: pl.BlockSpec(shape, lambda b, i: (0,) * len(shape), **kw)
    kvw = N_KV_HEADS * HEAD_DIM
    sds = jax.ShapeDtypeStruct
    return pl.pallas_call(
        _dsa_prep_kernel,
        out_shape=(sds((bsz, t, TOK_WIDTH), BF16),
                   sds((bsz, t, kvw), F32), sds((bsz, t, kvw), BF16),
                   sds((bsz, t, kvw), F32), sds((bsz, nb, kvw, tm), BF16),
                   sds((bsz, t, IDX_HEADS * IDX_DIM), F32),
                   sds((bsz, t, LANES), F32),
                   sds((bsz, t, 2 * LANES), BF16),
                   sds((bsz, IDX_HEADS, t), F32),
                   sds((bsz, t, MEM_WIDTH), F32)),
        grid=(bsz, nb),
        in_specs=[rows(D_MODEL), const((1, D_MODEL)),
                  const((D_MODEL, DSA_COLS_PAD), pipeline_mode=pl.Buffered(1)),
                  tab, tab, tab, tab, tab, gain, gain, gain],
        out_specs=(rows(TOK_WIDTH), rows(kvw), rows(kvw), rows(kvw),
                   pl.BlockSpec((1, 1, kvw, tm), lambda b, i: (b, i, 0, 0)),
                   rows(IDX_HEADS * IDX_DIM), rows(LANES), rows(2 * LANES),
                   pl.BlockSpec((1, IDX_HEADS, tm), lambda b, i: (b, 0, i)),
                   rows(MEM_WIDTH)),
        scratch_shapes=[pltpu.VMEM((1, tm, DSA_COLS_PAD), F32)],
        compiler_params=_params("parallel", "parallel"),
        name="dsa_prep",
    )(x3d, norm_gain.reshape(1, D_MODEL), w_bf16, *tabs,
      q_gain.reshape(1, LANES), k_gain.reshape(1, LANES), jnp.tile(ik_gain, 2).reshape(1, LANES))


def _rope_tables(pos):
    pos = pos.astype(F32)[:, None]

    def cs(half):
        inv = ROPE_THETA ** (-jnp.arange(half, dtype=F32) / half)
        ang = pos * inv[None, :]
        return jnp.cos(ang), jnp.sin(ang)

    c, s = cs(HEAD_DIM // 2)
    c128 = jnp.concatenate([c, c], axis=1)
    s128 = jnp.concatenate([-s, s], axis=1)
    c, s = cs(IDX_DIM // 2)
    z = jnp.zeros_like(s)
    c64 = jnp.concatenate([c, c, c, c], axis=1)
    s64lo = jnp.concatenate([-s, z, -s, z], axis=1)
    s64hi = jnp.concatenate([z, s, z, s], axis=1)
    return c128, s128, c64, s64lo, s64hi


def _ordered_bits_to_float(u):
    key = u ^ INT_MIN
    bits = key ^ ((key >> 31) & 0x7FFFFFFF)
    return lax.bitcast_convert_type(bits, F32)


def _topk_threshold(count_ge, count_gt, count_eq_before, any_true, shape, idx_bits):
    def value_step(i, prefix):
        cand = prefix | jnp.left_shift(jnp.int32(1), 31 - i)
        below_neg_inf = jnp.logical_and(cand >= 0, cand < 0x00800000)
        ok = jnp.logical_or(count_ge(_ordered_bits_to_float(cand)) >= TOPK, below_neg_inf)
        return jnp.where(ok, cand, prefix)

    thr = _ordered_bits_to_float(lax.fori_loop(0, 32, value_step, jnp.zeros(shape, I32)))
    n_gt = count_gt(thr)
    need = TOPK - n_gt
    n_eq = count_ge(thr) - n_gt
    finite_thr = thr > -jnp.inf
    excess = jnp.logical_and(n_eq > need, finite_thr)

    def index_search():
        def index_step(i, cut):
            cand = cut | jnp.left_shift(jnp.int32(1), idx_bits - 1 - i)
            return jnp.where(count_eq_before(thr, cand) < need, cand, cut)
        return lax.fori_loop(0, idx_bits, index_step, jnp.zeros(shape, I32))

    everything = jnp.full(shape, 2 ** idx_bits - 1, I32)
    cut = lax.cond(any_true(excess), index_search, lambda: everything)
    return thr, jnp.where(finite_thr, cut, -1)


def _dsa_prompt_kernel(q_ref, iq_ref, iwt_ref, ik_ref, k_ref, vt_ref, prev_ref, o_ref,
                       score_ref, iqm_ref, qm_ref, acc_ref, *, n_kb):
    del prev_ref
    q0 = ((n_kb - 1) * (KEY_BLOCK // Q_TILE) + pl.program_id(1)) * Q_TILE
    lane = lax.broadcasted_iota(I32, (Q_TILE, LANES), 1)
    first_half = lane < IDX_DIM
    for p in range(IDX_HEADS // 2):
        pair = iq_ref[0, :, p * LANES:(p + 1) * LANES]
        swapped = pltpu.roll(pair, IDX_DIM, 1)
        for half in range(2):
            dup = jnp.where(first_half, pair, swapped) if half == 0 else jnp.where(first_half, swapped, pair)
            hi, lo = _split_bf16(dup)
            rows = slice(half * Q_TILE, (half + 1) * Q_TILE)
            iqm_ref[p, rows, :LANES] = hi
            iqm_ref[p, rows, LANES:] = lo
    for h in range(N_Q_HEADS):
        g = h // Q_PER_KV
        qh = q_ref[0, :, h * HEAD_DIM:(h + 1) * HEAD_DIM]
        rows = slice(h * Q_TILE, (h + 1) * Q_TILE)
        qm_ref[rows, g * HEAD_DIM:(g + 1) * HEAD_DIM] = qh
        qm_ref[rows, (1 - g) * HEAD_DIM:(2 - g) * HEAD_DIM] = jnp.zeros_like(qh)
    q_pos = q0 + lax.broadcasted_iota(I32, (1, Q_TILE), 1)
    key_row = lax.broadcasted_iota(I32, (KEY_BLOCK, 1), 0)

    blocks = [slice(kb * KEY_BLOCK, (kb + 1) * KEY_BLOCK) for kb in range(n_kb)]

    for kb in range(n_kb):
        ik = ik_ref[0, blocks[kb], :]
        acc = jnp.zeros((KEY_BLOCK, Q_TILE), F32)
        for p in range(IDX_HEADS // 2):
            d = _dot_nt(ik, iqm_ref[p])
            for half in range(2):
                h = 2 * p + half
                acc = acc + iwt_ref[0, h:h + 1, :] * jnp.maximum(d[:, half * Q_TILE:(half + 1) * Q_TILE], 0.0)
        if kb == n_kb - 1:
            acc = jnp.where(kb * KEY_BLOCK + key_row <= q_pos, acc, -jnp.inf)
        score_ref[blocks[kb], :] = acc

    def count(pred):
        part = jnp.zeros((SUBLANES, Q_TILE), I32)
        for kb in range(n_kb):
            hit = pred(score_ref[blocks[kb], :], kb * KEY_BLOCK + key_row).astype(I32)
            part = part + jnp.sum(hit.reshape(KEY_BLOCK // SUBLANES, SUBLANES, Q_TILE), axis=0)
        return jnp.sum(part, axis=0, keepdims=True)

    thr, cut = _topk_threshold(
        count_ge=lambda c: count(lambda s, idx: s >= c),
        count_gt=lambda c: count(lambda s, idx: s > c),
        count_eq_before=lambda t, j: count(lambda s, idx: jnp.logical_and(s == t, idx < j)),
        any_true=lambda m: jnp.max(m.astype(I32)) > 0,
        shape=(1, Q_TILE), idx_bits=13)

    m = [jnp.full((1, Q_TILE), NEG_BIG, F32)] * N_Q_HEADS
    l = [jnp.zeros((1, Q_TILE), F32)] * N_Q_HEADS
    sub_row = lax.broadcasted_iota(I32, (ATTN_SUB, 1), 0)
    for ab in range(n_kb * KEY_BLOCK // ATTN_SUB):
        kb, off = divmod(ab * ATTN_SUB, KEY_BLOCK)
        keys = slice(ab * ATTN_SUB, (ab + 1) * ATTN_SUB)
        sc = score_ref[keys, :]
        sel = jnp.logical_or(sc > thr, jnp.logical_and(sc == thr, ab * ATTN_SUB + sub_row <= cut))
        k_blk = k_ref[0, keys, :]
        alphas, ps = [], []
        for j in range(N_Q_HEADS // 2):
            s2 = _dot_nt(k_blk, qm_ref[2 * j * Q_TILE:2 * (j + 1) * Q_TILE, :])
            for half in range(2):
                h = 2 * j + half
                s = jnp.where(sel, s2[:, half * Q_TILE:(half + 1) * Q_TILE], NEG_BIG)
                m_new = jnp.maximum(m[h], jnp.max(s, axis=0, keepdims=True))
                alpha = jnp.exp2(m[h] - m_new)
                p = jnp.exp2(s - m_new)
                l[h] = alpha * l[h] + jnp.sum(p, axis=0, keepdims=True)
                m[h] = m_new
                alphas.append(alpha)
                ps.append(p.astype(BF16))
        for g in range(N_KV_HEADS):
            vtg = vt_ref[0, kb, g * HEAD_DIM:(g + 1) * HEAD_DIM, off:off + ATTN_SUB]
            pv = jnp.dot(vtg, jnp.concatenate(ps[g * Q_PER_KV:(g + 1) * Q_PER_KV], axis=1),
                         preferred_element_type=F32)
            for r in range(Q_PER_KV):
                h = g * Q_PER_KV + r
                part = pv[:, r * Q_TILE:(r + 1) * Q_TILE]
                acc_ref[h] = part if ab == 0 else alphas[h] * acc_ref[h] + part
    for h in range(N_Q_HEADS):
        o_ref[0, :, h * HEAD_DIM:(h + 1) * HEAD_DIM] = (acc_ref[h] / l[h]).T


def _dsa_prompt(q, iq, iwt, ikb, kb, vt):
    bsz, t, _ = q.shape
    kvw = N_KV_HEADS * HEAD_DIM
    tiles = KEY_BLOCK // Q_TILE
    out = jnp.zeros((bsz, t, TOK_WIDTH), F32)
    for n_kb in range(1, t // KEY_BLOCK + 1):
        tile = lambda b, i, n_kb=n_kb: (n_kb - 1) * tiles + i
        keys = lambda w, n_kb=n_kb: pl.BlockSpec((1, n_kb * KEY_BLOCK, w), lambda b, i: (b, 0, 0))
        out = pl.pallas_call(
            functools.partial(_dsa_prompt_kernel, n_kb=n_kb),
            out_shape=jax.ShapeDtypeStruct((bsz, t, TOK_WIDTH), F32),
            grid=(bsz, tiles),
            in_specs=[pl.BlockSpec((1, Q_TILE, TOK_WIDTH), lambda b, i, tile=tile: (b, tile(b, i), 0)),
                      pl.BlockSpec((1, Q_TILE, IDX_HEADS * IDX_DIM), lambda b, i, tile=tile: (b, tile(b, i), 0)),
                      pl.BlockSpec((1, IDX_HEADS, Q_TILE), lambda b, i, tile=tile: (b, 0, tile(b, i))),
                      keys(2 * LANES), keys(kvw),
                      pl.BlockSpec((1, n_kb, kvw, KEY_BLOCK), lambda b, i: (b, 0, 0, 0)),
                      pl.BlockSpec(memory_space=pl.ANY)],
            out_specs=pl.BlockSpec((1, Q_TILE, TOK_WIDTH), lambda b, i, tile=tile: (b, tile(b, i), 0)),
            scratch_shapes=[pltpu.VMEM((n_kb * KEY_BLOCK, Q_TILE), F32),
                            pltpu.VMEM((IDX_HEADS // 2, 2 * Q_TILE, 2 * LANES), BF16),
                            pltpu.VMEM((N_Q_HEADS * Q_TILE, N_KV_HEADS * HEAD_DIM), BF16),
                            pltpu.VMEM((N_Q_HEADS, HEAD_DIM, Q_TILE), F32)],
            input_output_aliases={6: 0},
            compiler_params=_params("parallel", "arbitrary"),
            name=f"dsa_prompt_{n_kb}",
        )(q, iq, iwt, ikb, kb, vt, out)
    return out


SAMPLE_ROWS = 16
SAMPLE_T = 4
N_PAGES = PAST_LEN // PAGE_SIZE
N_KEY_PAGES = N_PAGES + 1


def _sample_score_kernel(pt_ref, iq4_ref, iw_ref, ikn_ref, *rest):
    page_refs = rest[:SCORE_PAGES_PER_STEP]
    (score_ref,) = rest[SCORE_PAGES_PER_STEP:]
    step = pl.program_id(1)
    iq_hi, iq_lo = _split_bf16(iq4_ref[0])
    iq_parts = jnp.where(lax.broadcasted_iota(I32, iq_hi.shape, 1) < 2 * IDX_DIM, iq_hi, iq_lo)
    w = iw_ref[0]

    def page_scores(ik_t, page, allowed):
        k_hi, k_lo = _split_bf16(ik_t)
        d = jnp.dot(iq_parts, jnp.concatenate([k_hi, k_lo, k_hi, k_lo], axis=0), preferred_element_type=F32)
        sc = jnp.sum((jnp.maximum(d, 0.0) * w).reshape(SAMPLE_T, IDX_HEADS, PAGE_SIZE), axis=1)
        if allowed is not None:
            sc = jnp.where(allowed, sc, -jnp.inf)
        for r in range(SAMPLE_ROWS // SAMPLE_T):
            score_ref[0, page, r * SAMPLE_T:(r + 1) * SAMPLE_T, :] = sc

    for i in range(SCORE_PAGES_PER_STEP):
        page_scores(page_refs[i][0], step * SCORE_PAGES_PER_STEP + i, None)

    @pl.when(step == pl.num_programs(1) - 1)
    def _():
        new_idx = lax.broadcasted_iota(I32, (SAMPLE_T, PAGE_SIZE), 1)
        page_scores(ikn_ref[0], N_PAGES, new_idx <= lax.broadcasted_iota(I32, (SAMPLE_T, PAGE_SIZE), 0))


def _sample_scores(page_table, iq4, iw_col, ik_new_t, pool_ik_t):
    bsz = iq4.shape[0]
    page_spec = lambda i: pl.BlockSpec((1, IDX_DIM, PAGE_SIZE),
                                       lambda b, s, pt, i=i: (pt[b, s * SCORE_PAGES_PER_STEP + i], 0, 0))
    per_b = lambda shape: pl.BlockSpec((1,) + shape, lambda b, s, pt: (b,) + (0,) * len(shape))
    grid_spec = pltpu.PrefetchScalarGridSpec(
        num_scalar_prefetch=1,
        grid=(bsz, N_PAGES // SCORE_PAGES_PER_STEP),
        in_specs=[per_b((SAMPLE_T * IDX_HEADS, 4 * IDX_DIM)), per_b((SAMPLE_T * IDX_HEADS, 1)),
                  per_b((IDX_DIM, PAGE_SIZE))] + [page_spec(i) for i in range(SCORE_PAGES_PER_STEP)],
        out_specs=per_b((N_KEY_PAGES, SAMPLE_ROWS, PAGE_SIZE)))
    return pl.pallas_call(
        _sample_score_kernel,
        out_shape=jax.ShapeDtypeStruct((bsz, N_KEY_PAGES, SAMPLE_ROWS, PAGE_SIZE), F32),
        grid_spec=grid_spec,
        compiler_params=_params("parallel", "arbitrary"),
        name="dsa_sample_scores",
    )(page_table, iq4, iw_col, ik_new_t, *([pool_ik_t] * SCORE_PAGES_PER_STEP))


def _sample_select_kernel(score_ref, thr_ref, cut_ref):
    n_q, n_k = score_ref.shape
    key_idx = lax.broadcasted_iota(I32, (1, n_k), 1)

    def count(pred):
        return jnp.sum(pred(score_ref[...], key_idx).astype(I32), axis=-1, keepdims=True)

    thr, cut = _topk_threshold(
        count_ge=lambda c: count(lambda s, idx: s >= c),
        count_gt=lambda c: count(lambda s, idx: s > c),
        count_eq_before=lambda t, j: count(lambda s, idx: jnp.logical_and(s == t, idx < j)),
        any_true=lambda m: jnp.max(m.astype(I32)) > 0,
        shape=(n_q, 1), idx_bits=14)
    thr_ref[...] = jnp.broadcast_to(thr, thr_ref.shape)
    cut_ref[...] = jnp.broadcast_to(cut, cut_ref.shape)


def _sample_select(scores2d):
    n_q, n_k = scores2d.shape
    whole = lambda shape: pl.BlockSpec(shape, lambda i: (0, 0))
    return pl.pallas_call(
        _sample_select_kernel,
        out_shape=(jax.ShapeDtypeStruct((n_q, LANES), F32), jax.ShapeDtypeStruct((n_q, LANES), I32)),
        grid=(1,),
        in_specs=[whole((n_q, n_k))],
        out_specs=(whole((n_q, LANES)), whole((n_q, LANES))),
        compiler_params=_params("arbitrary"),
        name="dsa_sample_select",
    )(scores2d)


def _sample_attn_kernel(pt_ref, q_ref, sc_ref, thr_ref, cut_ref, kn_ref, vn_ref, *rest):
    k_pages = rest[:ATTN_PAGES_PER_STEP]
    v_pages = rest[ATTN_PAGES_PER_STEP:2 * ATTN_PAGES_PER_STEP]
    o_ref, m_ref, l_ref, acc_ref = rest[2 * ATTN_PAGES_PER_STEP:]
    step = pl.program_id(1)

    @pl.when(step == 0)
    def _():
        m_ref[...] = jnp.full(m_ref.shape, NEG_BIG, F32)
        l_ref[...] = jnp.zeros(l_ref.shape, F32)
        acc_ref[...] = jnp.zeros(acc_ref.shape, F32)

    thr = thr_ref[0]
    cut = cut_ref[0]
    lane = lax.broadcasted_iota(I32, (SAMPLE_ROWS, PAGE_SIZE), 1)

    def attend(pages):
        sels = []
        for page, _, _ in pages:
            sc = sc_ref[0, page]
            sels.append(jnp.logical_or(sc > thr, jnp.logical_and(sc == thr, page * PAGE_SIZE + lane <= cut)))
        cat = lambda parts, axis: parts[0] if len(parts) == 1 else jnp.concatenate(parts, axis=axis)
        sel = cat(sels, 1)
        for g in range(N_KV_HEADS):
            rows_g = pl.ds(g, PAGE_SIZE, stride=N_KV_HEADS)
            k_all = cat([k[0, rows_g, :].astype(BF16) for _, k, _ in pages], 0)
            v_all = cat([v[0, rows_g, :].astype(BF16) for _, _, v in pages], 0)
            s = jnp.where(sel, _dot_nt(q_ref[0, g], k_all), NEG_BIG)
            m_old = m_ref[g]
            m_new = jnp.maximum(m_old, jnp.max(s, axis=-1, keepdims=True))
            alpha = jnp.exp2(m_old - m_new)
            p = jnp.exp2(s - m_new)
            l_ref[g] = alpha * l_ref[g] + jnp.sum(p, axis=-1, keepdims=True)
            acc_ref[g] = alpha * acc_ref[g] + jnp.dot(p.astype(BF16), v_all, preferred_element_type=F32)
            m_ref[g] = m_new

    attend([(step * ATTN_PAGES_PER_STEP + i, k_pages[i], v_pages[i]) for i in range(ATTN_PAGES_PER_STEP)])

    @pl.when(step == pl.num_programs(1) - 1)
    def _():
        attend([(N_PAGES, kn_ref, vn_ref)])
        for g in range(N_KV_HEADS):
            o_ref[0, g] = acc_ref[g] / l_ref[g]


def _sample_attn(page_table, q16, scores, thr, cut, k_new_pad, v_new_pad, pool_k, pool_v):
    bsz = q16.shape[0]
    page_rows = PAGE_SIZE * N_KV_HEADS
    page_spec = lambda i: pl.BlockSpec((1, page_rows, HEAD_DIM),
                                       lambda b, s, pt, i=i: (pt[b, s * ATTN_PAGES_PER_STEP + i], 0, 0))
    per_b = lambda shape: pl.BlockSpec((1,) + shape, lambda b, s, pt: (b,) + (0,) * len(shape))
    pages = [page_spec(i) for i in range(ATTN_PAGES_PER_STEP)]
    grid_spec = pltpu.PrefetchScalarGridSpec(
        num_scalar_prefetch=1,
        grid=(bsz, N_PAGES // ATTN_PAGES_PER_STEP),
        in_specs=[per_b((N_KV_HEADS, SAMPLE_ROWS, HEAD_DIM)), per_b((N_KEY_PAGES, SAMPLE_ROWS, PAGE_SIZE)),
                  per_b((SAMPLE_ROWS, LANES)), per_b((SAMPLE_ROWS, LANES)),
                  per_b((page_rows, HEAD_DIM)), per_b((page_rows, HEAD_DIM))] + pages + pages,
        out_specs=per_b((N_KV_HEADS, SAMPLE_ROWS, HEAD_DIM)),
        scratch_shapes=[pltpu.VMEM((N_KV_HEADS, SAMPLE_ROWS, 1), F32),
                        pltpu.VMEM((N_KV_HEADS, SAMPLE_ROWS, 1), F32),
                        pltpu.VMEM((N_KV_HEADS, SAMPLE_ROWS, HEAD_DIM), F32)])
    return pl.pallas_call(
        _sample_attn_kernel,
        out_shape=jax.ShapeDtypeStruct((bsz, N_KV_HEADS, SAMPLE_ROWS, HEAD_DIM), F32),
        grid_spec=grid_spec,
        compiler_params=_params("parallel", "arbitrary"),
        name="dsa_sample_attn",
    )(page_table, q16, scores, thr, cut, k_new_pad, v_new_pad,
      *([pool_k] * ATTN_PAGES_PER_STEP), *([pool_v] * ATTN_PAGES_PER_STEP))


def _pad_cols(w, width):
    return jnp.pad(w, ((0, 0), (0, width - w.shape[1])))


def _dsa_weight(w):
    sizes = (TOK_WIDTH, N_KV_HEADS * HEAD_DIM, N_KV_HEADS * HEAD_DIM, IDX_HEADS * IDX_DIM, IDX_DIM, IDX_HEADS, MEM_WIDTH)
    offs = [0]
    for s in sizes:
        offs.append(offs[-1] + s)
    q, k, v, iq, ik, iw, mq = [w[:, offs[i]:offs[i + 1]] for i in range(len(sizes))]
    return jnp.concatenate([q, k, v, iq, ik, ik, _pad_cols(iw, LANES), mq], axis=1).astype(BF16)


def kernel(x_prompt, x_sample, cache_k, cache_v, cache_idx_k, cache_mem_k, cache_mem_v, state_hgrn, state_ffn_conv, page_table, mem_prompt, norm_mix, norm_ffn, w_in_hgrn, hgrn_lb_logits, hgrn_out_norm, w_in_dsa, dsa_q_norm, dsa_k_norm, idx_k_norm, w_mem_kv, mem_q_norm, mem_k_norm, w_out, w_ffn_gate, w_ffn_up, ffn_conv_w, ffn_conv_b, w_ffn_down):
    bp, seq, _ = x_prompt.shape
    bs, t_s, _ = x_sample.shape
    assert t_s == SAMPLE_T and seq % KEY_BLOCK == 0
    n_s = bs * SAMPLE_T_PAD
    lower_bounds = jnp.cumsum(jax.nn.softmax(hgrn_lb_logits.astype(F32), axis=0), axis=0)

    xp = x_prompt
    xs = jnp.pad(x_sample, ((0, 0), (0, SAMPLE_T_PAD - t_s), (0, 0)))
    mem2d = mem_prompt.reshape(bp * N_MEM, D_MODEL)
    tabs_p = _rope_tables(jnp.arange(seq))
    pos_s = PAST_LEN + jnp.arange(SAMPLE_T_PAD)
    tabs_s = tuple(jnp.tile(tb, (bs, 1)) for tb in _rope_tables(pos_s))
    row_t = jnp.arange(n_s) % SAMPLE_T_PAD

    hgrn_p, hgrn_s = [], []
    kp_l, vp_l, ikp_l, ks_l, vs_l, iks_l = [], [], [], [], [], []
    mk_l, mv_l, cvp_l, cvs_l = [], [], [], []
    for layer in range(DEPTH):
        j = layer // 2
        if layer % 2 == 0:
            w_in = w_in_hgrn[j].astype(BF16)
            cols = w_in.shape[1]
            mq_col = HGRN_MEM0 // MEM_WIDTH
            mqp = _norm_proj(xp.reshape(bp * seq, D_MODEL), norm_mix[layer], w_in, 256).reshape(bp, seq, cols)
            mqs = _norm_proj(xs.reshape(n_s, D_MODEL), norm_mix[layer], w_in, n_s).reshape(bs, SAMPLE_T_PAD, cols)
            hgrn_args = (lower_bounds[j], hgrn_out_norm[j])
            s0 = jnp.zeros((bp, HGRN_HEADS, HGRN_DK, HGRN_DK), F32)
            tp, sp = _hgrn(mqp, *hgrn_args, s0, HGRN_CHUNK, HGRN_SUB, 512, HGRN_CHUNK)
            ts, ss = _hgrn(mqs, *hgrn_args, state_hgrn[j], SAMPLE_T_PAD, SAMPLE_T_PAD,
                           SAMPLE_T_PAD, math.gcd(t_s, HGRN_CHUNK))
            hgrn_p.append(sp.astype(state_hgrn.dtype))
            hgrn_s.append(ss.astype(state_hgrn.dtype))
        else:
            mq_col = 0
            prep_args = (norm_mix[layer], _dsa_weight(w_in_dsa[j]))
            gains = (dsa_q_norm[j], dsa_k_norm[j], idx_k_norm[j])
            qp, kp, kpb, vp, vtp, iqp, ikp, ikpb, iwtp, mqp = _dsa_prep(xp, *prep_args, tabs_p, *gains, KEY_BLOCK)
            tp = _dsa_prompt(qp, iqp, iwtp, ikpb, kpb, vtp)
            kp_l.append(kp.reshape(bp, seq, N_KV_HEADS, HEAD_DIM))
            vp_l.append(vp.reshape(bp, seq, N_KV_HEADS, HEAD_DIM))
            ikp_l.append(ikp[..., :IDX_DIM])

            qs, kn, _, vn, _, iqs, ikn, _, iwts, mqs = _dsa_prep(
                xs.reshape(1, n_s, D_MODEL), *prep_args, tabs_s, *gains, n_s)
            mqs = mqs.reshape(bs, SAMPLE_T_PAD, MEM_WIDTH)
            sel_t = lambda a: a.reshape(bs, SAMPLE_T_PAD, -1)[:, :t_s]
            kn, vn, ikn = sel_t(kn), sel_t(vn), sel_t(ikn)[..., :IDX_DIM]
            ks_l.append(kn.reshape(bs, t_s, N_KV_HEADS, HEAD_DIM))
            vs_l.append(vn.reshape(bs, t_s, N_KV_HEADS, HEAD_DIM))
            iks_l.append(ikn)
            iq4 = jnp.tile(sel_t(iqs).reshape(bs, t_s * IDX_HEADS, IDX_DIM), (1, 1, 4))
            iw_col = sel_t(iwts[0].T).reshape(bs, t_s * IDX_HEADS, 1)
            ikn_t = jnp.pad(ikn.transpose(0, 2, 1), ((0, 0), (0, 0), (0, PAGE_SIZE - t_s)))
            scores = _sample_scores(page_table, iq4, iw_col, ikn_t, cache_idx_k[j].transpose(0, 2, 1))
            per_query = scores[:, :, :t_s, :].transpose(0, 2, 1, 3).reshape(bs * t_s, N_KEY_PAGES * PAGE_SIZE)
            thr, cut = _sample_select(per_query)
            rep = lambda a: jnp.tile(a.reshape(bs, t_s, LANES), (1, SAMPLE_ROWS // t_s, 1))
            thr, cut = rep(thr), rep(cut)
            q16 = sel_t(qs).reshape(bs, t_s, N_KV_HEADS, Q_PER_KV, HEAD_DIM).transpose(0, 2, 3, 1, 4)
            q16 = jnp.pad(q16.reshape(bs, N_KV_HEADS, Q_PER_KV * t_s, HEAD_DIM),
                          ((0, 0), (0, 0), (0, SAMPLE_ROWS - Q_PER_KV * t_s), (0, 0)))
            page_rows = lambda a: a.reshape(a.shape[0], -1, HEAD_DIM)
            pad_page = lambda a: jnp.pad(page_rows(a), ((0, 0), (0, (PAGE_SIZE - t_s) * N_KV_HEADS), (0, 0)))
            o16 = _sample_attn(page_table, q16, scores, thr, cut, pad_page(kn), pad_page(vn),
                               page_rows(cache_k[j]), page_rows(cache_v[j]))
            o = o16[:, :, :Q_PER_KV * t_s].reshape(bs, N_KV_HEADS, Q_PER_KV, t_s, HEAD_DIM)
            ts = jnp.pad(o.transpose(0, 3, 1, 2, 4).reshape(bs, t_s, TOK_WIDTH),
                         ((0, 0), (0, SAMPLE_T_PAD - t_s), (0, 0)))

        mkp, mvp = _mem_kv(mem2d, w_mem_kv[layer].astype(BF16), mem_k_norm[layer])
        mk_l.append(mkp.reshape(bp, N_MEM, MEM_HEADS, MEM_HD))
        mv_l.append(mvp.reshape(bp, N_MEM, MEM_HEADS, MEM_HD))
        w_o = w_out[layer].astype(BF16)
        xp = _mem_attn_out(mqp, mq_col, mem_q_norm[layer], mkp.reshape(1, bp, N_MEM, MEM_WIDTH),
                           mvp.reshape(1, bp, N_MEM, MEM_WIDTH), 0, False, xp, tp, w_o, 512)
        mem_t = lambda a: a.transpose(0, 1, 3, 4, 2).reshape(DEPTH, bs, MEM_WIDTH, N_MEM)
        xs2 = _mem_attn_out(mqs, mq_col, mem_q_norm[layer], mem_t(cache_mem_k), mem_t(cache_mem_v), layer,
                            True, xs, ts, w_o, SAMPLE_T_PAD).reshape(n_s, D_MODEL)

        wg, wu, wd = (w.astype(BF16) for w in (w_ffn_gate[layer], w_ffn_up[layer], w_ffn_down[layer]))
        ffn_w = (norm_ffn[layer], wg, wu, ffn_conv_w[layer], ffn_conv_b[layer], wd)
        xp, u_last = _ffn_long(xp, *ffn_w, jnp.zeros((bp, CONV_W - 1, D_FF), F32), 256)
        cvp_l.append(u_last[:, SUBLANES - (CONV_W - 1):, :])
        st = state_ffn_conv[layer]
        zrow = jnp.zeros((bs, SAMPLE_T_PAD - 1, D_FF), F32)
        p1 = jnp.concatenate([st[:, 1:2], zrow], axis=1).reshape(n_s, D_FF)
        p2 = jnp.concatenate([st, zrow[:, 1:]], axis=1).reshape(n_s, D_FF)
        xs2, u_s = _ffn_short(xs2, *ffn_w, p1, p2, SAMPLE_T_PAD)
        cvs_l.append(u_s.reshape(bs, SAMPLE_T_PAD, D_FF)[:, t_s - (CONV_W - 1):t_s])
        xs = jnp.where((row_t < t_s)[:, None], xs2, 0.0).reshape(bs, SAMPLE_T_PAD, D_MODEL)

    return (xp, xs[:, :t_s],
            jnp.stack(hgrn_p), jnp.stack(hgrn_s),
            jnp.stack(kp_l), jnp.stack(vp_l), jnp.stack(ikp_l),
            jnp.stack(ks_l), jnp.stack(vs_l), jnp.stack(iks_l),
            jnp.stack(mk_l), jnp.stack(mv_l),
            jnp.stack(cvp_l), jnp.stack(cvs_l))
```

```python
import functools
import math

import jax
import jax.numpy as jnp
from jax import lax
from jax.experimental import pallas as pl
from jax.experimental.pallas import tpu as pltpu

D_MODEL = 1024
DEPTH = 2
PAST_LEN = 8192
PAGE_SIZE = 128
MEM_WIDTH = 256
TOK_WIDTH = 768
MEM_HEADS = 4
MEM_HD = 64
N_MEM = 256
HGRN_DK = 128
HGRN_HEADS = 6
HGRN_CHUNK = 32
HEAD_DIM = 128
N_Q_HEADS = 6
N_KV_HEADS = 2
Q_PER_KV = N_Q_HEADS // N_KV_HEADS
IDX_HEADS = 8
IDX_DIM = 64
TOPK = 256
ROPE_THETA = 10000.0
D_FF = 2816
CONV_W = 3
EPS = 1e-6

LANES = 128
SUBLANES = 8
VMEM_LIMIT_BYTES = 56 * 1024 * 1024
FF_CHUNK = 256
HGRN_SUB = 256
PROJ_CHUNK = 256
KEY_BLOCK = 512
Q_TILE = 128
ATTN_SUB = 512
SAMPLE_T_PAD = 8
SCORE_PAGES_PER_STEP = 32
ATTN_PAGES_PER_STEP = 16
NEG_BIG = -1e30
QK_SCALE_LOG2E = HEAD_DIM ** -0.5 * math.log2(math.e)
F32 = jnp.float32
BF16 = jnp.bfloat16
I32 = jnp.int32
INT_MIN = -2 ** 31

DSA_Q0, DSA_K0, DSA_V0, DSA_IQ0, DSA_IK0, DSA_IW0, DSA_MEM0, DSA_COLS_PAD = 0, 768, 1024, 1280, 1792, 1920, 2048, 2304
HGRN_MEM0 = 4 * TOK_WIDTH


def _params(*sem):
    return pltpu.CompilerParams(dimension_semantics=sem, vmem_limit_bytes=VMEM_LIMIT_BYTES)


def _rms(x, gain):
    ms = jnp.mean(x * x, axis=-1, keepdims=True)
    return x * lax.rsqrt(ms + EPS) * gain


def _split_bf16(x):
    hi = x.astype(BF16)
    lo = (x - hi.astype(F32)).astype(BF16)
    return hi, lo


def _group_mean_sq(x, group):
    n = x.shape[-1]
    r = lax.broadcasted_iota(I32, (n, n), 0) // group
    c = lax.broadcasted_iota(I32, (n, n), 1) // group
    ones = jnp.where(r == c, 1.0, 0.0).astype(BF16)
    hi, lo = _split_bf16(x * x)
    s = jnp.dot(hi, ones, preferred_element_type=F32) + jnp.dot(lo, ones, preferred_element_type=F32)
    return s * (1.0 / group)


def _dot_nt(a, b):
    return lax.dot_general(a, b, (((1,), (1,)), ((), ())), preferred_element_type=F32)


def _dot_tn(a, b):
    return lax.dot_general(a, b, (((0,), (0,)), ((), ())), preferred_element_type=F32)


def _silu(x):
    return x * jax.nn.sigmoid(x)


def _hgrn_kernel(*refs, chunk, sub, n_sub, valid_rows, fused):
    if fused:
        x_ref, ng_ref, w_ref, lb_ref, on_ref, s0_ref, o_ref, mq_ref, so_ref, z_ref, st_ref = refs
    else:
        zin_ref, lb_ref, on_ref, s0_ref, o_ref, so_ref, st_ref = refs
        z_ref = zin_ref.at[0]
    step = pl.program_id(1)

    @pl.when(step == 0)
    def _():
        for h in range(HGRN_HEADS):
            st_ref[h] = s0_ref[0, h].T

    def project(s):
        if not fused:
            return
        rs = slice(s * sub, (s + 1) * sub)
        hn = _rms(x_ref[0, rs, :], ng_ref[...]).astype(BF16)
        for j in range(z_ref.shape[1] // PROJ_CHUNK):
            cs = slice(j * PROJ_CHUNK, (j + 1) * PROJ_CHUNK)
            z_ref[rs, cs] = jnp.dot(hn, w_ref[:, cs], preferred_element_type=F32)
        mq_ref[0, rs, :] = z_ref[rs, HGRN_MEM0:]

    zcol = lambda part, h: slice(part * TOK_WIDTH + h * HGRN_DK, part * TOK_WIDTH + (h + 1) * HGRN_DK)

    row = lax.broadcasted_iota(I32, (sub, sub), 0)
    col = lax.broadcasted_iota(I32, (sub, sub), 1)
    same_chunk = (row // chunk) == (col // chunk)
    causal = jnp.logical_and(same_chunk, row >= col)
    sum_lhs = jnp.concatenate([jnp.where(causal, 1.0, 0.0), jnp.where(same_chunk, 1.0, 0.0)], axis=0).astype(BF16)
    row_id = lax.broadcasted_iota(I32, (sub, HGRN_DK), 0)
    row_ok = row_id % chunk < valid_rows
    n_c = sub // chunk
    in_chunk = [row_id // chunk == c for c in range(n_c)]

    project(0)
    for s in range(n_sub):
        if s + 1 < n_sub:
            project(s + 1)
        rows = slice(s * sub, (s + 1) * sub)
        for h in range(HGRN_HEADS):
            sl = slice(h * HGRN_DK, (h + 1) * HGRN_DK)
            lb = lb_ref[:, sl]
            q = _silu(z_ref[rows, zcol(0, h)])
            f = lb + (1.0 - lb) * jax.nn.sigmoid(z_ref[rows, zcol(1, h)])
            if valid_rows < chunk:
                f = jnp.where(row_ok, f, 1.0)
            k = 1.0 - f
            vb = z_ref[rows, zcol(2, h)].astype(BF16)
            g_hi, g_lo = _split_bf16(jnp.log(f))
            sums = jnp.dot(sum_lhs, jnp.concatenate([g_hi, g_lo], axis=1), preferred_element_type=F32)
            b = sums[:sub, :HGRN_DK] + sums[:sub, HGRN_DK:]
            b_chunk = sums[sub:, :HGRN_DK] + sums[sub:, HGRN_DK:]
            q_dec = q * jnp.exp(b)
            k_dec = (k * jnp.exp(-b)).astype(BF16)
            k_rem = k * jnp.exp(b_chunk - b)
            chunk_decay = jnp.exp(b_chunk)
            scores = jnp.where(causal, _dot_nt(q_dec.astype(BF16), k_dec), 0.0)
            o = jnp.dot(scores.astype(BF16), vb, preferred_element_type=F32)
            spread = lambda a: jnp.concatenate([jnp.where(in_chunk[c], a, 0.0) for c in range(n_c)],
                                               axis=1).astype(BF16)
            incr = _dot_tn(vb, spread(k_rem))
            st = st_ref[h]
            states = []
            for c in range(n_c):
                states.append(st.astype(BF16))
                st = st * chunk_decay[c * chunk:c * chunk + 1] + incr[:, c * HGRN_DK:(c + 1) * HGRN_DK]
            st_ref[h] = st
            o = o + _dot_nt(spread(q_dec), jnp.concatenate(states, axis=1))
            o_ref[0, rows, sl] = _rms(o, on_ref[...]) * _silu(z_ref[rows, zcol(3, h)])

    @pl.when(step == pl.num_programs(1) - 1)
    def _():
        for h in range(HGRN_HEADS):
            so_ref[0, h] = st_ref[h].T


def _norm_proj_kernel(x_ref, g_ref, w_ref, o_ref):
    h = _rms(x_ref[...], g_ref[...]).astype(BF16)
    o_ref[...] = jnp.dot(h, w_ref[...], preferred_element_type=F32)


def _norm_proj(x2d, gain, w_bf16):
    n, d = x2d.shape
    c = w_bf16.shape[1]
    whole = lambda shape: pl.BlockSpec(shape, lambda i: (0, 0))
    return pl.pallas_call(
        _norm_proj_kernel,
        out_shape=jax.ShapeDtypeStruct((n, c), F32),
        grid=(1,),
        in_specs=[whole((n, d)), whole((1, d)), whole((d, c))],
        out_specs=whole((n, c)),
        compiler_params=_params("arbitrary"),
        name="norm_proj",
    )(x2d, gain.reshape(1, d), w_bf16)


def _hgrn_projected(z3d, lb, out_norm, s0, chunk, sub, t_step, valid_rows):
    bsz, t, _ = z3d.shape
    const = lambda shape: pl.BlockSpec(shape, lambda b, i: (0,) * len(shape))
    rows = lambda w: pl.BlockSpec((1, t_step, w), lambda b, i: (b, i, 0))
    state_spec = pl.BlockSpec((1, HGRN_HEADS, HGRN_DK, HGRN_DK), lambda b, i: (b, 0, 0, 0))
    kern = functools.partial(_hgrn_kernel, chunk=chunk, sub=sub, n_sub=t_step // sub, valid_rows=valid_rows,
                             fused=False)
    return pl.pallas_call(
        kern,
        out_shape=(jax.ShapeDtypeStruct((bsz, t, TOK_WIDTH), F32),
                   jax.ShapeDtypeStruct((bsz, HGRN_HEADS, HGRN_DK, HGRN_DK), F32)),
        grid=(bsz, t // t_step),
        in_specs=[rows(4 * TOK_WIDTH), const((1, TOK_WIDTH)), const((1, HGRN_DK)), state_spec],
        out_specs=(rows(TOK_WIDTH), state_spec),
        scratch_shapes=[pltpu.VMEM((HGRN_HEADS, HGRN_DK, HGRN_DK), F32)],
        compiler_params=_params("parallel", "arbitrary"),
        name="hgrn2_projected",
    )(z3d, lb.reshape(1, TOK_WIDTH), out_norm.reshape(1, HGRN_DK), s0)


def _hgrn(x3d, norm_gain, w_bf16, lb, out_norm, s0, chunk, sub, t_step, valid_rows):
    bsz, t, _ = x3d.shape
    cols = w_bf16.shape[1]
    const = lambda shape, **kw: pl.BlockSpec(shape, lambda b, i: (0,) * len(shape), **kw)
    rows = lambda w: pl.BlockSpec((1, t_step, w), lambda b, i: (b, i, 0))
    state_spec = pl.BlockSpec((1, HGRN_HEADS, HGRN_DK, HGRN_DK), lambda b, i: (b, 0, 0, 0))
    kern = functools.partial(_hgrn_kernel, chunk=chunk, sub=sub, n_sub=t_step // sub, valid_rows=valid_rows,
                             fused=True)
    return pl.pallas_call(
        kern,
        out_shape=(jax.ShapeDtypeStruct((bsz, t, TOK_WIDTH), F32),
                   jax.ShapeDtypeStruct((bsz, t, MEM_WIDTH), F32),
                   jax.ShapeDtypeStruct((bsz, HGRN_HEADS, HGRN_DK, HGRN_DK), F32)),
        grid=(bsz, t // t_step),
        in_specs=[rows(D_MODEL), const((1, D_MODEL)),
                  const((D_MODEL, cols), pipeline_mode=pl.Buffered(1)),
                  const((1, TOK_WIDTH)), const((1, HGRN_DK)), state_spec],
        out_specs=(rows(TOK_WIDTH), rows(MEM_WIDTH), state_spec),
        scratch_shapes=[pltpu.VMEM((t_step, cols), F32),
                        pltpu.VMEM((HGRN_HEADS, HGRN_DK, HGRN_DK), F32)],
        compiler_params=_params("parallel", "arbitrary"),
        name="hgrn2",
    )(x3d, norm_gain.reshape(1, D_MODEL), w_bf16, lb.reshape(1, TOK_WIDTH), out_norm.reshape(1, HGRN_DK), s0)


def _mem_kv_kernel(m_ref, w_ref, g_ref, k_ref, v_ref):
    kv = jnp.dot(m_ref[...].astype(BF16), w_ref[...], preferred_element_type=F32)
    k = kv[:, :MEM_WIDTH]
    k_ref[...] = k * lax.rsqrt(_group_mean_sq(k, MEM_HD) + EPS) * g_ref[...]
    v_ref[...] = kv[:, MEM_WIDTH:]


def _mem_kv(mem2d, w_bf16, k_gain):
    n = mem2d.shape[0]
    out = jax.ShapeDtypeStruct((n, MEM_WIDTH), F32)
    return pl.pallas_call(
        _mem_kv_kernel,
        out_shape=(out, out),
        grid=(n // N_MEM,),
        in_specs=[pl.BlockSpec((N_MEM, D_MODEL), lambda i: (i, 0)),
                  pl.BlockSpec((D_MODEL, 2 * MEM_WIDTH), lambda i: (0, 0)),
                  pl.BlockSpec((1, MEM_WIDTH), lambda i: (0, 0))],
        out_specs=(pl.BlockSpec((N_MEM, MEM_WIDTH), lambda i: (i, 0)),
                   pl.BlockSpec((N_MEM, MEM_WIDTH), lambda i: (i, 0))),
        compiler_params=_params("parallel"),
        name="mem_kv",
    )(mem2d, w_bf16, jnp.tile(k_gain, MEM_HEADS).reshape(1, MEM_WIDTH))


def _mem_attn_out_kernel(zq_ref, g_ref, mk_ref, mv_ref, x_ref, t_ref, wt_ref, wc_ref, o_ref, *, kv_transposed):
    q = zq_ref[0]
    qn = q * lax.rsqrt(_group_mean_sq(q, MEM_HD) + EPS) * g_ref[...]
    mk = mk_ref[0].astype(BF16)
    mv = mv_ref[0].astype(BF16)
    head_of_lane = lax.broadcasted_iota(I32, qn.shape, 1) // MEM_HD
    c = jnp.zeros(qn.shape, F32)
    for h in range(MEM_HEADS):
        mine = head_of_lane == h
        qh = jnp.where(mine, qn, 0.0).astype(BF16)
        s = (jnp.dot(qh, mk, preferred_element_type=F32) if kv_transposed else _dot_nt(qh, mk)) * (MEM_HD ** -0.5)
        p = jnp.exp(s - jnp.max(s, axis=-1, keepdims=True))
        pb = p.astype(BF16)
        oh = _dot_nt(pb, mv) if kv_transposed else jnp.dot(pb, mv, preferred_element_type=F32)
        c = jnp.where(mine, oh / jnp.sum(p, axis=-1, keepdims=True), c)
    o_ref[0] = (x_ref[0]
                + jnp.dot(t_ref[0].astype(BF16), wt_ref[...], preferred_element_type=F32)
                + jnp.dot(c.astype(BF16), wc_ref[...], preferred_element_type=F32))


def _mem_attn_out(zq3d, q_col_block, q_gain, mk, mv, layer, kv_transposed, x3d, t3d, w_out, tm):
    bsz, t, _ = x3d.shape
    rows = lambda w, col=0: pl.BlockSpec((1, tm, w), lambda b, i: (b, i, col))
    per_b = pl.BlockSpec((None, 1, N_MEM, MEM_WIDTH), lambda b, i: (layer, b, 0, 0))
    const = lambda shape: pl.BlockSpec(shape, lambda b, i: (0,) * len(shape))
    return pl.pallas_call(
        functools.partial(_mem_attn_out_kernel, kv_transposed=kv_transposed),
        out_shape=jax.ShapeDtypeStruct((bsz, t, D_MODEL), F32),
        grid=(bsz, t // tm),
        in_specs=[rows(MEM_WIDTH, q_col_block), const((1, MEM_WIDTH)), per_b, per_b,
                  rows(D_MODEL), rows(TOK_WIDTH), const((TOK_WIDTH, D_MODEL)), const((MEM_WIDTH, D_MODEL))],
        out_specs=rows(D_MODEL),
        compiler_params=_params("parallel", "parallel"),
        name="mem_attn_out",
    )(zq3d, jnp.tile(q_gain, MEM_HEADS).reshape(1, MEM_WIDTH), mk, mv, x3d, t3d,
      w_out[:TOK_WIDTH], w_out[TOK_WIDTH:])


def _ffn_kernel(x_ref, g_ref, wg_ref, wu_ref, cw_ref, cb_ref, wd_ref, p1_ref, p2_ref,
                y_ref, u_ref, carry_ref, *, seq_rows, carry):
    tm = x_ref.shape[0]
    x = x_ref[...]
    h = _rms(x, g_ref[...]).astype(BF16)
    t_in_seq = lax.broadcasted_iota(I32, (tm, FF_CHUNK), 0) % seq_rows
    first_block = pl.program_id(1) == 0 if carry else None
    n_chunks = D_FF // FF_CHUNK

    def project(c):
        cs = slice(c * FF_CHUNK, (c + 1) * FF_CHUNK)
        return (jnp.dot(h, wg_ref[:, cs], preferred_element_type=F32),
                jnp.dot(h, wu_ref[:, cs], preferred_element_type=F32))

    acc = x
    ahead = project(0)
    for c in range(n_chunks):
        cs = slice(c * FF_CHUNK, (c + 1) * FF_CHUNK)
        u, up = ahead
        if c + 1 < n_chunks:
            ahead = project(c + 1)
        u1 = pltpu.roll(u, 1, 0)
        u2 = pltpu.roll(u, 2, 0)
        if carry:
            prev = carry_ref[:, cs]
            start1 = jnp.where(first_block, p1_ref[0:1, cs], prev[7:8, :])
            start2a = jnp.where(first_block, p2_ref[0:1, cs], prev[6:7, :])
            start2b = jnp.where(first_block, p2_ref[1:2, cs], prev[7:8, :])
            u1 = jnp.where(t_in_seq == 0, start1, u1)
            u2 = jnp.where(t_in_seq == 0, start2a, jnp.where(t_in_seq == 1, start2b, u2))
            carry_ref[:, cs] = u[tm - SUBLANES:, :]
        else:
            u1 = jnp.where(t_in_seq == 0, p1_ref[:, cs], u1)
            u2 = jnp.where(t_in_seq < 2, p2_ref[:, cs], u2)
        conv = cb_ref[:, cs] + u2 * cw_ref[0:1, cs] + u1 * cw_ref[1:2, cs] + u * cw_ref[2:3, cs]
        a = (_silu(conv) * up).astype(BF16)
        acc = acc + jnp.dot(a, wd_ref[cs, :], preferred_element_type=F32)
        if carry:
            u_ref[0, :, cs] = u[tm - SUBLANES:, :]
        else:
            u_ref[:, cs] = u
    y_ref[...] = acc


def _ffn_weights_specs(idx):
    once = pl.Buffered(1)
    return [pl.BlockSpec((1, D_MODEL), idx),
            pl.BlockSpec((D_MODEL, D_FF), idx, pipeline_mode=once),
            pl.BlockSpec((D_MODEL, D_FF), idx, pipeline_mode=once),
            pl.BlockSpec((CONV_W, D_FF), idx),
            pl.BlockSpec((1, D_FF), idx),
            pl.BlockSpec((D_FF, D_MODEL), idx, pipeline_mode=once)]


def _ffn_long(x3d, gain, wg, wu, cw, cb, wd, state, tm):
    bsz, t, _ = x3d.shape
    idx = lambda b, i: (0, 0)
    kern = functools.partial(_ffn_kernel, seq_rows=t, carry=True)
    nb = t // tm
    row = lambda w: pl.BlockSpec((tm, w), lambda b, i: (b * nb + i, 0))
    p1 = state[:, 1:2, :]
    y, u_last = pl.pallas_call(
        kern,
        out_shape=(jax.ShapeDtypeStruct((bsz * t, D_MODEL), F32),
                   jax.ShapeDtypeStruct((bsz, SUBLANES, D_FF), F32)),
        grid=(bsz, nb),
        in_specs=[row(D_MODEL)] + _ffn_weights_specs(idx)
                 + [pl.BlockSpec((None, 1, D_FF), lambda b, i: (b, 0, 0)),
                    pl.BlockSpec((None, 2, D_FF), lambda b, i: (b, 0, 0))],
        out_specs=(row(D_MODEL), pl.BlockSpec((1, SUBLANES, D_FF), lambda b, i: (b, 0, 0))),
        scratch_shapes=[pltpu.VMEM((SUBLANES, D_FF), F32)],
        compiler_params=_params("parallel", "arbitrary"),
        name="conv_ffn_long",
    )(x3d.reshape(bsz * t, D_MODEL), gain.reshape(1, D_MODEL), wg, wu, cw, cb.reshape(1, D_FF), wd, p1, state)
    return y.reshape(bsz, t, D_MODEL), u_last


def _ffn_short(x2d, gain, wg, wu, cw, cb, wd, p1, p2, seq_rows):
    n = x2d.shape[0]
    idx = lambda i: (0, 0)
    kern = functools.partial(_ffn_kernel, seq_rows=seq_rows, carry=False)
    row = lambda w: pl.BlockSpec((n, w), lambda i: (0, 0))
    return pl.pallas_call(
        kern,
        out_shape=(jax.ShapeDtypeStruct((n, D_MODEL), F32), jax.ShapeDtypeStruct((n, D_FF), F32)),
        grid=(1,),
        in_specs=[row(D_MODEL)] + _ffn_weights_specs(idx) + [row(D_FF), row(D_FF)],
        out_specs=(row(D_MODEL), row(D_FF)),
        scratch_shapes=[pltpu.VMEM((SUBLANES, D_FF), F32)],
        compiler_params=_params("arbitrary"),
        name="conv_ffn_short",
    )(x2d, gain.reshape(1, D_MODEL), wg, wu, cw, cb.reshape(1, D_FF), wd, p1, p2)


def _rope128(x, cos, sin_signed):
    return x * cos + pltpu.roll(x, HEAD_DIM // 2, 1) * sin_signed


def _rope64_pairs(x, cos, sin_lo, sin_hi):
    return (x * cos + pltpu.roll(x, LANES - IDX_DIM // 2, 1) * sin_lo
            + pltpu.roll(x, IDX_DIM // 2, 1) * sin_hi)


def _dsa_prep_kernel(x_ref, ng_ref, w_ref, c128_ref, s128_ref, c64_ref, s64lo_ref, s64hi_ref, qg_ref, kg_ref, ikg_ref,
                     q_ref, k_ref, kb_ref, v_ref, vt_ref, iq_ref, ik_ref, ikb_ref, iwt_ref, mq_ref, z_ref):
    hn = _rms(x_ref[0], ng_ref[...]).astype(BF16)
    for j in range(DSA_COLS_PAD // PROJ_CHUNK):
        cs = slice(j * PROJ_CHUNK, (j + 1) * PROJ_CHUNK)
        z_ref[0, :, cs] = jnp.dot(hn, w_ref[:, cs], preferred_element_type=F32)
    mq_ref[0] = z_ref[0, :, DSA_MEM0:]
    c128, s128 = c128_ref[...], s128_ref[...]
    c64, s64lo, s64hi = c64_ref[...], s64lo_ref[...], s64hi_ref[...]
    for h in range(N_Q_HEADS):
        sl = slice(DSA_Q0 + h * HEAD_DIM, DSA_Q0 + (h + 1) * HEAD_DIM)
        q = _rope128(_rms(z_ref[0, :, sl], qg_ref[...]), c128, s128)
        q_ref[0, :, h * HEAD_DIM:(h + 1) * HEAD_DIM] = (q * QK_SCALE_LOG2E).astype(BF16)
    for h in range(N_KV_HEADS):
        sl = slice(DSA_K0 + h * HEAD_DIM, DSA_K0 + (h + 1) * HEAD_DIM)
        k = _rope128(_rms(z_ref[0, :, sl], kg_ref[...]), c128, s128)
        k_ref[0, :, h * HEAD_DIM:(h + 1) * HEAD_DIM] = k
        kb_ref[0, :, h * HEAD_DIM:(h + 1) * HEAD_DIM] = k.astype(BF16)
    v = z_ref[0, :, DSA_V0:DSA_V0 + N_KV_HEADS * HEAD_DIM]
    v_ref[0] = v
    vt_ref[0, 0] = v.T.astype(BF16)
    for p in range(IDX_HEADS // 2):
        sl = slice(DSA_IQ0 + p * LANES, DSA_IQ0 + (p + 1) * LANES)
        iq_ref[0, :, p * LANES:(p + 1) * LANES] = _rope64_pairs(z_ref[0, :, sl], c64, s64lo, s64hi)
    ik = _rope64_pairs(_rms(z_ref[0, :, DSA_IK0:DSA_IK0 + LANES], ikg_ref[...]), c64, s64lo, s64hi)
    ik_ref[0] = ik
    ik_hi, ik_lo = _split_bf16(ik)
    hi_lo = jnp.where(lax.broadcasted_iota(I32, ik.shape, 1) < IDX_DIM, ik_hi, ik_lo)
    ikb_ref[0, :, :LANES] = hi_lo
    ikb_ref[0, :, LANES:] = hi_lo
    iw = z_ref[0, :, DSA_IW0:DSA_IW0 + LANES] * (IDX_HEADS ** -0.5 * IDX_DIM ** -0.5)
    iwt_ref[0] = iw.T[:IDX_HEADS, :]


def _dsa_prep(x3d, norm_gain, w_bf16, tabs, q_gain, k_gain, ik_gain, tm):
    bsz, t, _ = x3d.shape
    nb = t // tm
    tab = pl.BlockSpec((tm, LANES), lambda b, i: (i, 0))
    gain = pl.BlockSpec((1, LANES), lambda b, i: (0, 0))
    rows = lambda w: pl.BlockSpec((1, tm, w), lambda b, i: (b, i, 0))
    const = lambda shape, **kw: pl.BlockSpec(shape, lambda b, i: (0,) * len(shape), **kw)
    kvw = N_KV_HEADS * HEAD_DIM
    sds = jax.ShapeDtypeStruct
    return pl.pallas_call(
        _dsa_prep_kernel,
        out_shape=(sds((bsz, t, TOK_WIDTH), BF16),
                   sds((bsz, t, kvw), F32), sds((bsz, t, kvw), BF16),
                   sds((bsz, t, kvw), F32), sds((bsz, nb, kvw, tm), BF16),
                   sds((bsz, t, IDX_HEADS * IDX_DIM), F32),
                   sds((bsz, t, LANES), F32),
                   sds((bsz, t, 2 * LANES), BF16),
                   sds((bsz, IDX_HEADS, t), F32),
                   sds((bsz, t, MEM_WIDTH), F32)),
        grid=(bsz, nb),
        in_specs=[rows(D_MODEL), const((1, D_MODEL)),
                  const((D_MODEL, DSA_COLS_PAD), pipeline_mode=pl.Buffered(1)),
                  tab, tab, tab, tab, tab, gain, gain, gain],
        out_specs=(rows(TOK_WIDTH), rows(kvw), rows(kvw), rows(kvw),
                   pl.BlockSpec((1, 1, kvw, tm), lambda b, i: (b, i, 0, 0)),
                   rows(IDX_HEADS * IDX_DIM), rows(LANES), rows(2 * LANES),
                   pl.BlockSpec((1, IDX_HEADS, tm), lambda b, i: (b, 0, i)),
                   rows(MEM_WIDTH)),
        scratch_shapes=[pltpu.VMEM((1, tm, DSA_COLS_PAD), F32)],
        compiler_params=_params("parallel", "parallel"),
        name="dsa_prep",
    )(x3d, norm_gain.reshape(1, D_MODEL), w_bf16, *tabs,
      q_gain.reshape(1, LANES), k_gain.reshape(1, LANES), jnp.tile(ik_gain, 2).reshape(1, LANES))


def _rope_tables(pos):
    pos = pos.astype(F32)[:, None]

    def cs(half):
        inv = ROPE_THETA ** (-jnp.arange(half, dtype=F32) / half)
        ang = pos * inv[None, :]
        return jnp.cos(ang), jnp.sin(ang)

    c, s = cs(HEAD_DIM // 2)
    c128 = jnp.concatenate([c, c], axis=1)
    s128 = jnp.concatenate([-s, s], axis=1)
    c, s = cs(IDX_DIM // 2)
    z = jnp.zeros_like(s)
    c64 = jnp.concatenate([c, c, c, c], axis=1)
    s64lo = jnp.concatenate([-s, z, -s, z], axis=1)
    s64hi = jnp.concatenate([z, s, z, s], axis=1)
    return c128, s128, c64, s64lo, s64hi


def _ordered_bits_to_float(u):
    key = u ^ INT_MIN
    bits = key ^ ((key >> 31) & 0x7FFFFFFF)
    return lax.bitcast_convert_type(bits, F32)


def _topk_threshold(count_ge, count_gt, count_eq_before, any_true, shape, idx_bits):
    def value_step(i, prefix):
        cand = prefix | jnp.left_shift(jnp.int32(1), 31 - i)
        below_neg_inf = jnp.logical_and(cand >= 0, cand < 0x00800000)
        ok = jnp.logical_or(count_ge(_ordered_bits_to_float(cand)) >= TOPK, below_neg_inf)
        return jnp.where(ok, cand, prefix)

    thr = _ordered_bits_to_float(lax.fori_loop(0, 32, value_step, jnp.zeros(shape, I32)))
    n_gt = count_gt(thr)
    need = TOPK - n_gt
    n_eq = count_ge(thr) - n_gt
    finite_thr = thr > -jnp.inf
    excess = jnp.logical_and(n_eq > need, finite_thr)

    def index_search():
        def index_step(i, cut):
            cand = cut | jnp.left_shift(jnp.int32(1), idx_bits - 1 - i)
            return jnp.where(count_eq_before(thr, cand) < need, cand, cut)
        return lax.fori_loop(0, idx_bits, index_step, jnp.zeros(shape, I32))

    everything = jnp.full(shape, 2 ** idx_bits - 1, I32)
    cut = lax.cond(any_true(excess), index_search, lambda: everything)
    return thr, jnp.where(finite_thr, cut, -1)


def _dsa_prompt_kernel(q_ref, iq_ref, iwt_ref, ik_ref, k_ref, vt_ref, prev_ref, o_ref,
                       score_ref, iqm_ref, qm_ref, acc_ref, *, n_kb):
    del prev_ref
    q0 = ((n_kb - 1) * (KEY_BLOCK // Q_TILE) + pl.program_id(1)) * Q_TILE
    lane = lax.broadcasted_iota(I32, (Q_TILE, LANES), 1)
    first_half = lane < IDX_DIM
    for p in range(IDX_HEADS // 2):
        pair = iq_ref[0, :, p * LANES:(p + 1) * LANES]
        swapped = pltpu.roll(pair, IDX_DIM, 1)
        for half in range(2):
            dup = jnp.where(first_half, pair, swapped) if half == 0 else jnp.where(first_half, swapped, pair)
            hi, lo = _split_bf16(dup)
            rows = slice(half * Q_TILE, (half + 1) * Q_TILE)
            iqm_ref[p, rows, :LANES] = hi
            iqm_ref[p, rows, LANES:] = lo
    for h in range(N_Q_HEADS):
        g = h // Q_PER_KV
        qh = q_ref[0, :, h * HEAD_DIM:(h + 1) * HEAD_DIM]
        rows = slice(h * Q_TILE, (h + 1) * Q_TILE)
        qm_ref[rows, g * HEAD_DIM:(g + 1) * HEAD_DIM] = qh
        qm_ref[rows, (1 - g) * HEAD_DIM:(2 - g) * HEAD_DIM] = jnp.zeros_like(qh)
    q_pos = q0 + lax.broadcasted_iota(I32, (1, Q_TILE), 1)
    key_row = lax.broadcasted_iota(I32, (KEY_BLOCK, 1), 0)

    blocks = [slice(kb * KEY_BLOCK, (kb + 1) * KEY_BLOCK) for kb in range(n_kb)]

    for kb in range(n_kb):
        ik = ik_ref[0, blocks[kb], :]
        acc = jnp.zeros((KEY_BLOCK, Q_TILE), F32)
        for p in range(IDX_HEADS // 2):
            d = _dot_nt(ik, iqm_ref[p])
            for half in range(2):
                h = 2 * p + half
                acc = acc + iwt_ref[0, h:h + 1, :] * jnp.maximum(d[:, half * Q_TILE:(half + 1) * Q_TILE], 0.0)
        if kb == n_kb - 1:
            acc = jnp.where(kb * KEY_BLOCK + key_row <= q_pos, acc, -jnp.inf)
        score_ref[blocks[kb], :] = acc

    def count(pred):
        part = jnp.zeros((SUBLANES, Q_TILE), I32)
        for kb in range(n_kb):
            hit = pred(score_ref[blocks[kb], :], kb * KEY_BLOCK + key_row).astype(I32)
            part = part + jnp.sum(hit.reshape(KEY_BLOCK // SUBLANES, SUBLANES, Q_TILE), axis=0)
        return jnp.sum(part, axis=0, keepdims=True)

    thr, cut = _topk_threshold(
        count_ge=lambda c: count(lambda s, idx: s >= c),
        count_gt=lambda c: count(lambda s, idx: s > c),
        count_eq_before=lambda t, j: count(lambda s, idx: jnp.logical_and(s == t, idx < j)),
        any_true=lambda m: jnp.max(m.astype(I32)) > 0,
        shape=(1, Q_TILE), idx_bits=13)

    m = [jnp.full((1, Q_TILE), NEG_BIG, F32)] * N_Q_HEADS
    l = [jnp.zeros((1, Q_TILE), F32)] * N_Q_HEADS
    sub_row = lax.broadcasted_iota(I32, (ATTN_SUB, 1), 0)
    for ab in range(n_kb * KEY_BLOCK // ATTN_SUB):
        kb, off = divmod(ab * ATTN_SUB, KEY_BLOCK)
        keys = slice(ab * ATTN_SUB, (ab + 1) * ATTN_SUB)
        sc = score_ref[keys, :]
        sel = jnp.logical_or(sc > thr, jnp.logical_and(sc == thr, ab * ATTN_SUB + sub_row <= cut))
        k_blk = k_ref[0, keys, :]
        alphas, ps = [], []
        for j in range(N_Q_HEADS // 2):
            s2 = _dot_nt(k_blk, qm_ref[2 * j * Q_TILE:2 * (j + 1) * Q_TILE, :])
            for half in range(2):
                h = 2 * j + half
                s = jnp.where(sel, s2[:, half * Q_TILE:(half + 1) * Q_TILE], NEG_BIG)
                m_new = jnp.maximum(m[h], jnp.max(s, axis=0, keepdims=True))
                alpha = jnp.exp2(m[h] - m_new)
                p = jnp.exp2(s - m_new)
                l[h] = alpha * l[h] + jnp.sum(p, axis=0, keepdims=True)
                m[h] = m_new
                alphas.append(alpha)
                ps.append(p.astype(BF16))
        for g in range(N_KV_HEADS):
            vtg = vt_ref[0, kb, g * HEAD_DIM:(g + 1) * HEAD_DIM, off:off + ATTN_SUB]
            pv = jnp.dot(vtg, jnp.concatenate(ps[g * Q_PER_KV:(g + 1) * Q_PER_KV], axis=1),
                         preferred_element_type=F32)
            for r in range(Q_PER_KV):
                h = g * Q_PER_KV + r
                part = pv[:, r * Q_TILE:(r + 1) * Q_TILE]
                acc_ref[h] = part if ab == 0 else alphas[h] * acc_ref[h] + part
    for h in range(N_Q_HEADS):
        o_ref[0, :, h * HEAD_DIM:(h + 1) * HEAD_DIM] = (acc_ref[h] / l[h]).T


def _dsa_prompt(q, iq, iwt, ikb, kb, vt):
    bsz, t, _ = q.shape
    kvw = N_KV_HEADS * HEAD_DIM
    tiles = KEY_BLOCK // Q_TILE
    out = jnp.zeros((bsz, t, TOK_WIDTH), F32)
    for n_kb in range(1, t // KEY_BLOCK + 1):
        tile = lambda b, i, n_kb=n_kb: (n_kb - 1) * tiles + i
        keys = lambda w, n_kb=n_kb: pl.BlockSpec((1, n_kb * KEY_BLOCK, w), lambda b, i: (b, 0, 0))
        out = pl.pallas_call(
            functools.partial(_dsa_prompt_kernel, n_kb=n_kb),
            out_shape=jax.ShapeDtypeStruct((bsz, t, TOK_WIDTH), F32),
            grid=(bsz, tiles),
            in_specs=[pl.BlockSpec((1, Q_TILE, TOK_WIDTH), lambda b, i, tile=tile: (b, tile(b, i), 0)),
                      pl.BlockSpec((1, Q_TILE, IDX_HEADS * IDX_DIM), lambda b, i, tile=tile: (b, tile(b, i), 0)),
                      pl.BlockSpec((1, IDX_HEADS, Q_TILE), lambda b, i, tile=tile: (b, 0, tile(b, i))),
                      keys(2 * LANES), keys(kvw),
                      pl.BlockSpec((1, n_kb, kvw, KEY_BLOCK), lambda b, i: (b, 0, 0, 0)),
                      pl.BlockSpec(memory_space=pl.ANY)],
            out_specs=pl.BlockSpec((1, Q_TILE, TOK_WIDTH), lambda b, i, tile=tile: (b, tile(b, i), 0)),
            scratch_shapes=[pltpu.VMEM((n_kb * KEY_BLOCK, Q_TILE), F32),
                            pltpu.VMEM((IDX_HEADS // 2, 2 * Q_TILE, 2 * LANES), BF16),
                            pltpu.VMEM((N_Q_HEADS * Q_TILE, N_KV_HEADS * HEAD_DIM), BF16),
                            pltpu.VMEM((N_Q_HEADS, HEAD_DIM, Q_TILE), F32)],
            input_output_aliases={6: 0},
            compiler_params=_params("parallel", "arbitrary"),
            name=f"dsa_prompt_{n_kb}",
        )(q, iq, iwt, ikb, kb, vt, out)
    return out


SAMPLE_ROWS = 16
SAMPLE_T = 4
N_PAGES = PAST_LEN // PAGE_SIZE
N_KEY_PAGES = N_PAGES + 1


def _sample_score_kernel(pt_ref, iq4_ref, iw_ref, ikn_ref, *rest):
    page_refs = rest[:SCORE_PAGES_PER_STEP]
    (score_ref,) = rest[SCORE_PAGES_PER_STEP:]
    step = pl.program_id(1)
    iq_hi, iq_lo = _split_bf16(iq4_ref[0])
    iq_parts = jnp.where(lax.broadcasted_iota(I32, iq_hi.shape, 1) < 2 * IDX_DIM, iq_hi, iq_lo)
    w = iw_ref[0]

    def page_scores(ik_t, page, allowed):
        k_hi, k_lo = _split_bf16(ik_t)
        d = jnp.dot(iq_parts, jnp.concatenate([k_hi, k_lo, k_hi, k_lo], axis=0), preferred_element_type=F32)
        sc = jnp.sum((jnp.maximum(d, 0.0) * w).reshape(SAMPLE_T, IDX_HEADS, PAGE_SIZE), axis=1)
        if allowed is not None:
            sc = jnp.where(allowed, sc, -jnp.inf)
        for r in range(SAMPLE_ROWS // SAMPLE_T):
            score_ref[0, page, r * SAMPLE_T:(r + 1) * SAMPLE_T, :] = sc

    for i in range(SCORE_PAGES_PER_STEP):
        page_scores(page_refs[i][0], step * SCORE_PAGES_PER_STEP + i, None)

    @pl.when(step == pl.num_programs(1) - 1)
    def _():
        new_idx = lax.broadcasted_iota(I32, (SAMPLE_T, PAGE_SIZE), 1)
        page_scores(ikn_ref[0], N_PAGES, new_idx <= lax.broadcasted_iota(I32, (SAMPLE_T, PAGE_SIZE), 0))


def _sample_scores(page_table, iq4, iw_col, ik_new_t, pool_ik_t):
    bsz = iq4.shape[0]
    page_spec = lambda i: pl.BlockSpec((1, IDX_DIM, PAGE_SIZE),
                                       lambda b, s, pt, i=i: (pt[b, s * SCORE_PAGES_PER_STEP + i], 0, 0))
    per_b = lambda shape: pl.BlockSpec((1,) + shape, lambda b, s, pt: (b,) + (0,) * len(shape))
    grid_spec = pltpu.PrefetchScalarGridSpec(
        num_scalar_prefetch=1,
        grid=(bsz, N_PAGES // SCORE_PAGES_PER_STEP),
        in_specs=[per_b((SAMPLE_T * IDX_HEADS, 4 * IDX_DIM)), per_b((SAMPLE_T * IDX_HEADS, 1)),
                  per_b((IDX_DIM, PAGE_SIZE))] + [page_spec(i) for i in range(SCORE_PAGES_PER_STEP)],
        out_specs=per_b((N_KEY_PAGES, SAMPLE_ROWS, PAGE_SIZE)))
    return pl.pallas_call(
        _sample_score_kernel,
        out_shape=jax.ShapeDtypeStruct((bsz, N_KEY_PAGES, SAMPLE_ROWS, PAGE_SIZE), F32),
        grid_spec=grid_spec,
        compiler_params=_params("parallel", "arbitrary"),
        name="dsa_sample_scores",
    )(page_table, iq4, iw_col, ik_new_t, *([pool_ik_t] * SCORE_PAGES_PER_STEP))


def _sample_select_kernel(score_ref, thr_ref, cut_ref):
    n_q, n_k = score_ref.shape
    key_idx = lax.broadcasted_iota(I32, (1, n_k), 1)

    def count(pred):
        return jnp.sum(pred(score_ref[...], key_idx).astype(I32), axis=-1, keepdims=True)

    thr, cut = _topk_threshold(
        count_ge=lambda c: count(lambda s, idx: s >= c),
        count_gt=lambda c: count(lambda s, idx: s > c),
        count_eq_before=lambda t, j: count(lambda s, idx: jnp.logical_and(s == t, idx < j)),
        any_true=lambda m: jnp.max(m.astype(I32)) > 0,
        shape=(n_q, 1), idx_bits=14)
    thr_ref[...] = jnp.broadcast_to(thr, thr_ref.shape)
    cut_ref[...] = jnp.broadcast_to(cut, cut_ref.shape)


def _sample_select(scores2d):
    n_q, n_k = scores2d.shape
    whole = lambda shape: pl.BlockSpec(shape, lambda i: (0, 0))
    return pl.pallas_call(
        _sample_select_kernel,
        out_shape=(jax.ShapeDtypeStruct((n_q, LANES), F32), jax.ShapeDtypeStruct((n_q, LANES), I32)),
        grid=(1,),
        in_specs=[whole((n_q, n_k))],
        out_specs=(whole((n_q, LANES)), whole((n_q, LANES))),
        compiler_params=_params("arbitrary"),
        name="dsa_sample_select",
    )(scores2d)


def _sample_attn_kernel(pt_ref, q_ref, sc_ref, thr_ref, cut_ref, kn_ref, vn_ref, *rest):
    k_pages = rest[:ATTN_PAGES_PER_STEP]
    v_pages = rest[ATTN_PAGES_PER_STEP:2 * ATTN_PAGES_PER_STEP]
    o_ref, m_ref, l_ref, acc_ref = rest[2 * ATTN_PAGES_PER_STEP:]
    step = pl.program_id(1)

    @pl.when(step == 0)
    def _():
        m_ref[...] = jnp.full(m_ref.shape, NEG_BIG, F32)
        l_ref[...] = jnp.zeros(l_ref.shape, F32)
        acc_ref[...] = jnp.zeros(acc_ref.shape, F32)

    thr = thr_ref[0]
    cut = cut_ref[0]
    lane = lax.broadcasted_iota(I32, (SAMPLE_ROWS, PAGE_SIZE), 1)

    def attend(pages):
        sels = []
        for page, _, _ in pages:
            sc = sc_ref[0, page]
            sels.append(jnp.logical_or(sc > thr, jnp.logical_and(sc == thr, page * PAGE_SIZE + lane <= cut)))
        cat = lambda parts, axis: parts[0] if len(parts) == 1 else jnp.concatenate(parts, axis=axis)
        sel = cat(sels, 1)
        for g in range(N_KV_HEADS):
            rows_g = pl.ds(g, PAGE_SIZE, stride=N_KV_HEADS)
            k_all = cat([k[0, rows_g, :].astype(BF16) for _, k, _ in pages], 0)
            v_all = cat([v[0, rows_g, :].astype(BF16) for _, _, v in pages], 0)
            s = jnp.where(sel, _dot_nt(q_ref[0, g], k_all), NEG_BIG)
            m_old = m_ref[g]
            m_new = jnp.maximum(m_old, jnp.max(s, axis=-1, keepdims=True))
            alpha = jnp.exp2(m_old - m_new)
            p = jnp.exp2(s - m_new)
            l_ref[g] = alpha * l_ref[g] + jnp.sum(p, axis=-1, keepdims=True)
            acc_ref[g] = alpha * acc_ref[g] + jnp.dot(p.astype(BF16), v_all, preferred_element_type=F32)
            m_ref[g] = m_new

    attend([(step * ATTN_PAGES_PER_STEP + i, k_pages[i], v_pages[i]) for i in range(ATTN_PAGES_PER_STEP)])

    @pl.when(step == pl.num_programs(1) - 1)
    def _():
        attend([(N_PAGES, kn_ref, vn_ref)])
        for g in range(N_KV_HEADS):
            o_ref[0, g] = acc_ref[g] / l_ref[g]


def _sample_attn(page_table, q16, scores, thr, cut, k_new_pad, v_new_pad, pool_k, pool_v):
    bsz = q16.shape[0]
    page_rows = PAGE_SIZE * N_KV_HEADS
    page_spec = lambda i: pl.BlockSpec((1, page_rows, HEAD_DIM),
                                       lambda b, s, pt, i=i: (pt[b, s * ATTN_PAGES_PER_STEP + i], 0, 0))
    per_b = lambda shape: pl.BlockSpec((1,) + shape, lambda b, s, pt: (b,) + (0,) * len(shape))
    pages = [page_spec(i) for i in range(ATTN_PAGES_PER_STEP)]
    grid_spec = pltpu.PrefetchScalarGridSpec(
        num_scalar_prefetch=1,
        grid=(bsz, N_PAGES // ATTN_PAGES_PER_STEP),
        in_specs=[per_b((N_KV_HEADS, SAMPLE_ROWS, HEAD_DIM)), per_b((N_KEY_PAGES, SAMPLE_ROWS, PAGE_SIZE)),
                  per_b((SAMPLE_ROWS, LANES)), per_b((SAMPLE_ROWS, LANES)),
                  per_b((page_rows, HEAD_DIM)), per_b((page_rows, HEAD_DIM))] + pages + pages,
        out_specs=per_b((N_KV_HEADS, SAMPLE_ROWS, HEAD_DIM)),
        scratch_shapes=[pltpu.VMEM((N_KV_HEADS, SAMPLE_ROWS, 1), F32),
                        pltpu.VMEM((N_KV_HEADS, SAMPLE_ROWS, 1), F32),
                        pltpu.VMEM((N_KV_HEADS, SAMPLE_ROWS, HEAD_DIM), F32)])
    return pl.pallas_call(
        _sample_attn_kernel,
        out_shape=jax.ShapeDtypeStruct((bsz, N_KV_HEADS, SAMPLE_ROWS, HEAD_DIM), F32),
        grid_spec=grid_spec,
        compiler_params=_params("parallel", "arbitrary"),
        name="dsa_sample_attn",
    )(page_table, q16, scores, thr, cut, k_new_pad, v_new_pad,
      *([pool_k] * ATTN_PAGES_PER_STEP), *([pool_v] * ATTN_PAGES_PER_STEP))


def _pad_cols(w, width):
    return jnp.pad(w, ((0, 0), (0, width - w.shape[1])))


def _dsa_weight(w):
    sizes = (TOK_WIDTH, N_KV_HEADS * HEAD_DIM, N_KV_HEADS * HEAD_DIM, IDX_HEADS * IDX_DIM, IDX_DIM, IDX_HEADS, MEM_WIDTH)
    offs = [0]
    for s in sizes:
        offs.append(offs[-1] + s)
    q, k, v, iq, ik, iw, mq = [w[:, offs[i]:offs[i + 1]] for i in range(len(sizes))]
    return jnp.concatenate([q, k, v, iq, ik, ik, _pad_cols(iw, LANES), mq], axis=1).astype(BF16)


def kernel(x_prompt, x_sample, cache_k, cache_v, cache_idx_k, cache_mem_k, cache_mem_v, state_hgrn, state_ffn_conv, page_table, mem_prompt, norm_mix, norm_ffn, w_in_hgrn, hgrn_lb_logits, hgrn_out_norm, w_in_dsa, dsa_q_norm, dsa_k_norm, idx_k_norm, w_mem_kv, mem_q_norm, mem_k_norm, w_out, w_ffn_gate, w_ffn_up, ffn_conv_w, ffn_conv_b, w_ffn_down):
    bp, seq, _ = x_prompt.shape
    bs, t_s, _ = x_sample.shape
    assert t_s == SAMPLE_T and seq % KEY_BLOCK == 0
    n_s = bs * SAMPLE_T_PAD
    lower_bounds = jnp.cumsum(jax.nn.softmax(hgrn_lb_logits.astype(F32), axis=0), axis=0)

    xp = x_prompt
    xs = jnp.pad(x_sample, ((0, 0), (0, SAMPLE_T_PAD - t_s), (0, 0)))
    mem2d = mem_prompt.reshape(bp * N_MEM, D_MODEL)
    tabs_p = _rope_tables(jnp.arange(seq))
    pos_s = PAST_LEN + jnp.arange(SAMPLE_T_PAD)
    tabs_s = tuple(jnp.tile(tb, (bs, 1)) for tb in _rope_tables(pos_s))
    row_t = jnp.arange(n_s) % SAMPLE_T_PAD

    hgrn_p, hgrn_s = [], []
    kp_l, vp_l, ikp_l, ks_l, vs_l, iks_l = [], [], [], [], [], []
    mk_l, mv_l, cvp_l, cvs_l = [], [], [], []
    for layer in range(DEPTH):
        j = layer // 2
        if layer % 2 == 0:
            mq_col = 0
            w_in = w_in_hgrn[j].astype(BF16)
            s0 = jnp.zeros((bp, HGRN_HEADS, HGRN_DK, HGRN_DK), F32)
            tp, mqp, sp = _hgrn(xp, norm_mix[layer], w_in, lower_bounds[j], hgrn_out_norm[j], s0,
                                HGRN_CHUNK, HGRN_SUB, 512, HGRN_CHUNK)
            zs = _norm_proj(xs.reshape(n_s, D_MODEL), norm_mix[layer], w_in).reshape(bs, SAMPLE_T_PAD, -1)
            ts, ss = _hgrn_projected(zs, lower_bounds[j], hgrn_out_norm[j], state_hgrn[j], SAMPLE_T_PAD,
                                     SAMPLE_T_PAD, SAMPLE_T_PAD, math.gcd(t_s, HGRN_CHUNK))
            mqs = zs[..., HGRN_MEM0:]
            hgrn_p.append(sp.astype(state_hgrn.dtype))
            hgrn_s.append(ss.astype(state_hgrn.dtype))
        else:
            mq_col = 0
            prep_args = (norm_mix[layer], _dsa_weight(w_in_dsa[j]))
            gains = (dsa_q_norm[j], dsa_k_norm[j], idx_k_norm[j])
            qp, kp, kpb, vp, vtp, iqp, ikp, ikpb, iwtp, mqp = _dsa_prep(xp, *prep_args, tabs_p, *gains, KEY_BLOCK)
            tp = _dsa_prompt(qp, iqp, iwtp, ikpb, kpb, vtp)
            kp_l.append(kp.reshape(bp, seq, N_KV_HEADS, HEAD_DIM))
            vp_l.append(vp.reshape(bp, seq, N_KV_HEADS, HEAD_DIM))
            ikp_l.append(ikp[..., :IDX_DIM])

            qs, kn, _, vn, _, iqs, ikn, _, iwts, mqs = _dsa_prep(
                xs.reshape(1, n_s, D_MODEL), *prep_args, tabs_s, *gains, n_s)
            mqs = mqs.reshape(bs, SAMPLE_T_PAD, MEM_WIDTH)
            sel_t = lambda a: a.reshape(bs, SAMPLE_T_PAD, -1)[:, :t_s]
            kn, vn, ikn = sel_t(kn), sel_t(vn), sel_t(ikn)[..., :IDX_DIM]
            ks_l.append(kn.reshape(bs, t_s, N_KV_HEADS, HEAD_DIM))
            vs_l.append(vn.reshape(bs, t_s, N_KV_HEADS, HEAD_DIM))
            iks_l.append(ikn)
            iq4 = jnp.tile(sel_t(iqs).reshape(bs, t_s * IDX_HEADS, IDX_DIM), (1, 1, 4))
            iw_col = sel_t(iwts[0].T).reshape(bs, t_s * IDX_HEADS, 1)
            ikn_t = jnp.pad(ikn.transpose(0, 2, 1), ((0, 0), (0, 0), (0, PAGE_SIZE - t_s)))
            scores = _sample_scores(page_table, iq4, iw_col, ikn_t, cache_idx_k[j].transpose(0, 2, 1))
            per_query = scores[:, :, :t_s, :].transpose(0, 2, 1, 3).reshape(bs * t_s, N_KEY_PAGES * PAGE_SIZE)
            thr, cut = _sample_select(per_query)
            rep = lambda a: jnp.tile(a.reshape(bs, t_s, LANES), (1, SAMPLE_ROWS // t_s, 1))
            thr, cut = rep(thr), rep(cut)
            q16 = sel_t(qs).reshape(bs, t_s, N_KV_HEADS, Q_PER_KV, HEAD_DIM).transpose(0, 2, 3, 1, 4)
            q16 = jnp.pad(q16.reshape(bs, N_KV_HEADS, Q_PER_KV * t_s, HEAD_DIM),
                          ((0, 0), (0, 0), (0, SAMPLE_ROWS - Q_PER_KV * t_s), (0, 0)))
            page_rows = lambda a: a.reshape(a.shape[0], -1, HEAD_DIM)
            pad_page = lambda a: jnp.pad(page_rows(a), ((0, 0), (0, (PAGE_SIZE - t_s) * N_KV_HEADS), (0, 0)))
            o16 = _sample_attn(page_table, q16, scores, thr, cut, pad_page(kn), pad_page(vn),
                               page_rows(cache_k[j]), page_rows(cache_v[j]))
            o = o16[:, :, :Q_PER_KV * t_s].reshape(bs, N_KV_HEADS, Q_PER_KV, t_s, HEAD_DIM)
            ts = jnp.pad(o.transpose(0, 3, 1, 2, 4).reshape(bs, t_s, TOK_WIDTH),
                         ((0, 0), (0, SAMPLE_T_PAD - t_s), (0, 0)))

        mkp, mvp = _mem_kv(mem2d, w_mem_kv[layer].astype(BF16), mem_k_norm[layer])
        mk_l.append(mkp.reshape(bp, N_MEM, MEM_HEADS, MEM_HD))
        mv_l.append(mvp.reshape(bp, N_MEM, MEM_HEADS, MEM_HD))
        w_o = w_out[layer].astype(BF16)
        xp = _mem_attn_out(mqp, mq_col, mem_q_norm[layer], mkp.reshape(1, bp, N_MEM, MEM_WIDTH),
                           mvp.reshape(1, bp, N_MEM, MEM_WIDTH), 0, False, xp, tp, w_o, 512)
        mem_t = lambda a: a.transpose(0, 1, 3, 4, 2).reshape(DEPTH, bs, MEM_WIDTH, N_MEM)
        xs2 = _mem_attn_out(mqs, mq_col, mem_q_norm[layer], mem_t(cache_mem_k), mem_t(cache_mem_v), layer,
                            True, xs, ts, w_o, SAMPLE_T_PAD).reshape(n_s, D_MODEL)

        wg, wu, wd = (w.astype(BF16) for w in (w_ffn_gate[layer], w_ffn_up[layer], w_ffn_down[layer]))
        ffn_w = (norm_ffn[layer], wg, wu, ffn_conv_w[layer], ffn_conv_b[layer], wd)
        xp, u_last = _ffn_long(xp, *ffn_w, jnp.zeros((bp, CONV_W - 1, D_FF), F32), 256)
        cvp_l.append(u_last[:, SUBLANES - (CONV_W - 1):, :])
        st = state_ffn_conv[layer]
        zrow = jnp.zeros((bs, SAMPLE_T_PAD - 1, D_FF), F32)
        p1 = jnp.concatenate([st[:, 1:2], zrow], axis=1).reshape(n_s, D_FF)
        p2 = jnp.concatenate([st, zrow[:, 1:]], axis=1).reshape(n_s, D_FF)
        xs2, u_s = _ffn_short(xs2, *ffn_w, p1, p2, SAMPLE_T_PAD)
        cvs_l.append(u_s.reshape(bs, SAMPLE_T_PAD, D_FF)[:, t_s - (CONV_W - 1):t_s])
        xs = jnp.where((row_t < t_s)[:, None], xs2, 0.0).reshape(bs, SAMPLE_T_PAD, D_MODEL)

    return (xp, xs[:, :t_s],
            jnp.stack(hgrn_p), jnp.stack(hgrn_s),
            jnp.stack(kp_l), jnp.stack(vp_l), jnp.stack(ikp_l),
            jnp.stack(ks_l), jnp.stack(vs_l), jnp.stack(iks_l),
            jnp.stack(mk_l), jnp.stack(mv_l),
            jnp.stack(cvp_l), jnp.stack(cvs_l))
```

```python
import functools
import math

import jax
import jax.numpy as jnp
from jax import lax
from jax.experimental import pallas as pl
from jax.experimental.pallas import tpu as pltpu

D_MODEL = 1024
DEPTH = 2
PAST_LEN = 8192
PAGE_SIZE = 128
MEM_WIDTH = 256
TOK_WIDTH = 768
MEM_HEADS = 4
MEM_HD = 64
N_MEM = 256
HGRN_DK = 128
HGRN_HEADS = 6
HGRN_CHUNK = 32
HEAD_DIM = 128
N_Q_HEADS = 6
N_KV_HEADS = 2
Q_PER_KV = N_Q_HEADS // N_KV_HEADS
IDX_HEADS = 8
IDX_DIM = 64
TOPK = 256
ROPE_THETA = 10000.0
D_FF = 2816
CONV_W = 3
EPS = 1e-6

LANES = 128
SUBLANES = 8
VMEM_LIMIT_BYTES = 56 * 1024 * 1024
FF_CHUNK = 256
HGRN_SUB = 256
PROJ_CHUNK = 256
PREP_PART = 256
KEY_BLOCK = 512
Q_TILE = 128
ATTN_SUB = 512
SAMPLE_T_PAD = 8
SCORE_PAGES_PER_STEP = 32
ATTN_PAGES_PER_STEP = 16
NEG_BIG = -1e30
QK_SCALE_LOG2E = HEAD_DIM ** -0.5 * math.log2(math.e)
F32 = jnp.float32
BF16 = jnp.bfloat16
I32 = jnp.int32
INT_MIN = -2 ** 31

DSA_Q0, DSA_K0, DSA_V0, DSA_IQ0, DSA_IK0, DSA_IW0, DSA_MEM0, DSA_COLS_PAD = 0, 768, 1024, 1280, 1792, 1920, 2048, 2304
HGRN_MEM0 = 4 * TOK_WIDTH


def _params(*sem):
    return pltpu.CompilerParams(dimension_semantics=sem, vmem_limit_bytes=VMEM_LIMIT_BYTES)


def _rms(x, gain):
    ms = jnp.mean(x * x, axis=-1, keepdims=True)
    return x * lax.rsqrt(ms + EPS) * gain


def _split_bf16(x):
    hi = x.astype(BF16)
    lo = (x - hi.astype(F32)).astype(BF16)
    return hi, lo


def _group_mean_sq(x, group):
    n = x.shape[-1]
    r = lax.broadcasted_iota(I32, (n, n), 0) // group
    c = lax.broadcasted_iota(I32, (n, n), 1) // group
    ones = jnp.where(r == c, 1.0, 0.0).astype(BF16)
    hi, lo = _split_bf16(x * x)
    s = jnp.dot(hi, ones, preferred_element_type=F32) + jnp.dot(lo, ones, preferred_element_type=F32)
    return s * (1.0 / group)


def _dot_nt(a, b):
    return lax.dot_general(a, b, (((1,), (1,)), ((), ())), preferred_element_type=F32)


def _dot_tn(a, b):
    return lax.dot_general(a, b, (((0,), (0,)), ((), ())), preferred_element_type=F32)


def _silu(x):
    return x * jax.nn.sigmoid(x)


def _hgrn_kernel(*refs, chunk, sub, n_sub, valid_rows, fused):
    if fused:
        x_ref, ng_ref, w_ref, lb_ref, on_ref, s0_ref, o_ref, mq_ref, so_ref, z_ref, st_ref = refs
    else:
        zin_ref, lb_ref, on_ref, s0_ref, o_ref, so_ref, st_ref = refs
        z_ref = zin_ref.at[0]
    step = pl.program_id(1)

    @pl.when(step == 0)
    def _():
        for h in range(HGRN_HEADS):
            st_ref[h] = s0_ref[0, h].T

    def project(s):
        if not fused:
            return
        rs = slice(s * sub, (s + 1) * sub)
        hn = _rms(x_ref[0, rs, :], ng_ref[...]).astype(BF16)
        for j in range(z_ref.shape[1] // PROJ_CHUNK):
            cs = slice(j * PROJ_CHUNK, (j + 1) * PROJ_CHUNK)
            z_ref[rs, cs] = jnp.dot(hn, w_ref[:, cs], preferred_element_type=F32)
        mq_ref[0, rs, :] = z_ref[rs, HGRN_MEM0:]

    zcol = lambda part, h: slice(part * TOK_WIDTH + h * HGRN_DK, part * TOK_WIDTH + (h + 1) * HGRN_DK)

    row = lax.broadcasted_iota(I32, (sub, sub), 0)
    col = lax.broadcasted_iota(I32, (sub, sub), 1)
    same_chunk = (row // chunk) == (col // chunk)
    causal = jnp.logical_and(same_chunk, row >= col)
    sum_lhs = jnp.concatenate([jnp.where(causal, 1.0, 0.0), jnp.where(same_chunk, 1.0, 0.0)], axis=0).astype(BF16)
    row_id = lax.broadcasted_iota(I32, (sub, HGRN_DK), 0)
    row_ok = row_id % chunk < valid_rows
    n_c = sub // chunk
    in_chunk = [row_id // chunk == c for c in range(n_c)]

    project(0)
    for s in range(n_sub):
        if s + 1 < n_sub:
            project(s + 1)
        rows = slice(s * sub, (s + 1) * sub)
        for h in range(HGRN_HEADS):
            sl = slice(h * HGRN_DK, (h + 1) * HGRN_DK)
            lb = lb_ref[:, sl]
            q = _silu(z_ref[rows, zcol(0, h)])
            f = lb + (1.0 - lb) * jax.nn.sigmoid(z_ref[rows, zcol(1, h)])
            if valid_rows < chunk:
                f = jnp.where(row_ok, f, 1.0)
            k = 1.0 - f
            vb = z_ref[rows, zcol(2, h)].astype(BF16)
            g_hi, g_lo = _split_bf16(jnp.log(f))
            sums = jnp.dot(sum_lhs, jnp.concatenate([g_hi, g_lo], axis=1), preferred_element_type=F32)
            b = sums[:sub, :HGRN_DK] + sums[:sub, HGRN_DK:]
            b_chunk = sums[sub:, :HGRN_DK] + sums[sub:, HGRN_DK:]
            q_dec = q * jnp.exp(b)
            k_dec = (k * jnp.exp(-b)).astype(BF16)
            k_rem = k * jnp.exp(b_chunk - b)
            chunk_decay = jnp.exp(b_chunk)
            scores = jnp.where(causal, _dot_nt(q_dec.astype(BF16), k_dec), 0.0)
            o = jnp.dot(scores.astype(BF16), vb, preferred_element_type=F32)
            spread = lambda a: jnp.concatenate([jnp.where(in_chunk[c], a, 0.0) for c in range(n_c)],
                                               axis=1).astype(BF16)
            incr = _dot_tn(vb, spread(k_rem))
            st = st_ref[h]
            states = []
            for c in range(n_c):
                states.append(st.astype(BF16))
                st = st * chunk_decay[c * chunk:c * chunk + 1] + incr[:, c * HGRN_DK:(c + 1) * HGRN_DK]
            st_ref[h] = st
            o = o + _dot_nt(spread(q_dec), jnp.concatenate(states, axis=1))
            o_ref[0, rows, sl] = _rms(o, on_ref[...]) * _silu(z_ref[rows, zcol(3, h)])

    @pl.when(step == pl.num_programs(1) - 1)
    def _():
        for h in range(HGRN_HEADS):
            so_ref[0, h] = st_ref[h].T


def _norm_proj_kernel(x_ref, g_ref, w_ref, o_ref):
    h = _rms(x_ref[...], g_ref[...]).astype(BF16)
    o_ref[...] = jnp.dot(h, w_ref[...], preferred_element_type=F32)


def _norm_proj(x2d, gain, w_bf16):
    n, d = x2d.shape
    c = w_bf16.shape[1]
    whole = lambda shape: pl.BlockSpec(shape, lambda i: (0, 0))
    return pl.pallas_call(
        _norm_proj_kernel,
        out_shape=jax.ShapeDtypeStruct((n, c), F32),
        grid=(1,),
        in_specs=[whole((n, d)), whole((1, d)), whole((d, c))],
        out_specs=whole((n, c)),
        compiler_params=_params("arbitrary"),
        name="norm_proj",
    )(x2d, gain.reshape(1, d), w_bf16)


def _hgrn_projected(z3d, lb, out_norm, s0, chunk, sub, t_step, valid_rows):
    bsz, t, _ = z3d.shape
    const = lambda shape: pl.BlockSpec(shape, lambda b, i: (0,) * len(shape))
    rows = lambda w: pl.BlockSpec((1, t_step, w), lambda b, i: (b, i, 0))
    state_spec = pl.BlockSpec((1, HGRN_HEADS, HGRN_DK, HGRN_DK), lambda b, i: (b, 0, 0, 0))
    kern = functools.partial(_hgrn_kernel, chunk=chunk, sub=sub, n_sub=t_step // sub, valid_rows=valid_rows,
                             fused=False)
    return pl.pallas_call(
        kern,
        out_shape=(jax.ShapeDtypeStruct((bsz, t, TOK_WIDTH), F32),
                   jax.ShapeDtypeStruct((bsz, HGRN_HEADS, HGRN_DK, HGRN_DK), F32)),
        grid=(bsz, t // t_step),
        in_specs=[rows(4 * TOK_WIDTH), const((1, TOK_WIDTH)), const((1, HGRN_DK)), state_spec],
        out_specs=(rows(TOK_WIDTH), state_spec),
        scratch_shapes=[pltpu.VMEM((HGRN_HEADS, HGRN_DK, HGRN_DK), F32)],
        compiler_params=_params("parallel", "arbitrary"),
        name="hgrn2_projected",
    )(z3d, lb.reshape(1, TOK_WIDTH), out_norm.reshape(1, HGRN_DK), s0)


def _hgrn(x3d, norm_gain, w_bf16, lb, out_norm, s0, chunk, sub, t_step, valid_rows):
    bsz, t, _ = x3d.shape
    cols = w_bf16.shape[1]
    const = lambda shape, **kw: pl.BlockSpec(shape, lambda b, i: (0,) * len(shape), **kw)
    rows = lambda w: pl.BlockSpec((1, t_step, w), lambda b, i: (b, i, 0))
    state_spec = pl.BlockSpec((1, HGRN_HEADS, HGRN_DK, HGRN_DK), lambda b, i: (b, 0, 0, 0))
    kern = functools.partial(_hgrn_kernel, chunk=chunk, sub=sub, n_sub=t_step // sub, valid_rows=valid_rows,
                             fused=True)
    return pl.pallas_call(
        kern,
        out_shape=(jax.ShapeDtypeStruct((bsz, t, TOK_WIDTH), F32),
                   jax.ShapeDtypeStruct((bsz, t, MEM_WIDTH), F32),
                   jax.ShapeDtypeStruct((bsz, HGRN_HEADS, HGRN_DK, HGRN_DK), F32)),
        grid=(bsz, t // t_step),
        in_specs=[rows(D_MODEL), const((1, D_MODEL)),
                  const((D_MODEL, cols), pipeline_mode=pl.Buffered(1)),
                  const((1, TOK_WIDTH)), const((1, HGRN_DK)), state_spec],
        out_specs=(rows(TOK_WIDTH), rows(MEM_WIDTH), state_spec),
        scratch_shapes=[pltpu.VMEM((t_step, cols), F32),
                        pltpu.VMEM((HGRN_HEADS, HGRN_DK, HGRN_DK), F32)],
        compiler_params=_params("parallel", "arbitrary"),
        name="hgrn2",
    )(x3d, norm_gain.reshape(1, D_MODEL), w_bf16, lb.reshape(1, TOK_WIDTH), out_norm.reshape(1, HGRN_DK), s0)


def _mem_kv_kernel(m_ref, w_ref, g_ref, k_ref, v_ref):
    kv = jnp.dot(m_ref[...].astype(BF16), w_ref[...], preferred_element_type=F32)
    k = kv[:, :MEM_WIDTH]
    k_ref[...] = k * lax.rsqrt(_group_mean_sq(k, MEM_HD) + EPS) * g_ref[...]
    v_ref[...] = kv[:, MEM_WIDTH:]


def _mem_kv(mem2d, w_bf16, k_gain):
    n = mem2d.shape[0]
    out = jax.ShapeDtypeStruct((n, MEM_WIDTH), F32)
    return pl.pallas_call(
        _mem_kv_kernel,
        out_shape=(out, out),
        grid=(n // N_MEM,),
        in_specs=[pl.BlockSpec((N_MEM, D_MODEL), lambda i: (i, 0)),
                  pl.BlockSpec((D_MODEL, 2 * MEM_WIDTH), lambda i: (0, 0)),
                  pl.BlockSpec((1, MEM_WIDTH), lambda i: (0, 0))],
        out_specs=(pl.BlockSpec((N_MEM, MEM_WIDTH), lambda i: (i, 0)),
                   pl.BlockSpec((N_MEM, MEM_WIDTH), lambda i: (i, 0))),
        compiler_params=_params("parallel"),
        name="mem_kv",
    )(mem2d, w_bf16, jnp.tile(k_gain, MEM_HEADS).reshape(1, MEM_WIDTH))


def _mem_attn_out_kernel(zq_ref, g_ref, mk_ref, mv_ref, x_ref, t_ref, wt_ref, wc_ref, o_ref, *, kv_transposed):
    q = zq_ref[0]
    qn = q * lax.rsqrt(_group_mean_sq(q, MEM_HD) + EPS) * g_ref[...]
    mk = mk_ref[0].astype(BF16)
    mv = mv_ref[0].astype(BF16)
    head_of_lane = lax.broadcasted_iota(I32, qn.shape, 1) // MEM_HD
    c = jnp.zeros(qn.shape, F32)
    for h in range(MEM_HEADS):
        mine = head_of_lane == h
        qh = jnp.where(mine, qn, 0.0).astype(BF16)
        s = (jnp.dot(qh, mk, preferred_element_type=F32) if kv_transposed else _dot_nt(qh, mk)) * (MEM_HD ** -0.5)
        p = jnp.exp(s - jnp.max(s, axis=-1, keepdims=True))
        pb = p.astype(BF16)
        oh = _dot_nt(pb, mv) if kv_transposed else jnp.dot(pb, mv, preferred_element_type=F32)
        c = jnp.where(mine, oh / jnp.sum(p, axis=-1, keepdims=True), c)
    o_ref[0] = (x_ref[0]
                + jnp.dot(t_ref[0].astype(BF16), wt_ref[...], preferred_element_type=F32)
                + jnp.dot(c.astype(BF16), wc_ref[...], preferred_element_type=F32))


def _mem_attn_out(zq3d, q_col_block, q_gain, mk, mv, layer, kv_transposed, x3d, t3d, w_out, tm):
    bsz, t, _ = x3d.shape
    rows = lambda w, col=0: pl.BlockSpec((1, tm, w), lambda b, i: (b, i, col))
    per_b = pl.BlockSpec((None, 1, N_MEM, MEM_WIDTH), lambda b, i: (layer, b, 0, 0))
    const = lambda shape: pl.BlockSpec(shape, lambda b, i: (0,) * len(shape))
    return pl.pallas_call(
        functools.partial(_mem_attn_out_kernel, kv_transposed=kv_transposed),
        out_shape=jax.ShapeDtypeStruct((bsz, t, D_MODEL), F32),
        grid=(bsz, t // tm),
        in_specs=[rows(MEM_WIDTH, q_col_block), const((1, MEM_WIDTH)), per_b, per_b,
                  rows(D_MODEL), rows(TOK_WIDTH), const((TOK_WIDTH, D_MODEL)), const((MEM_WIDTH, D_MODEL))],
        out_specs=rows(D_MODEL),
        compiler_params=_params("parallel", "parallel"),
        name="mem_attn_out",
    )(zq3d, jnp.tile(q_gain, MEM_HEADS).reshape(1, MEM_WIDTH), mk, mv, x3d, t3d,
      w_out[:TOK_WIDTH], w_out[TOK_WIDTH:])


def _ffn_kernel(x_ref, g_ref, wg_ref, wu_ref, cw_ref, cb_ref, wd_ref, p1_ref, p2_ref,
                y_ref, u_ref, carry_ref, *, seq_rows, carry):
    tm = x_ref.shape[0]
    x = x_ref[...]
    h = _rms(x, g_ref[...]).astype(BF16)
    t_in_seq = lax.broadcasted_iota(I32, (tm, FF_CHUNK), 0) % seq_rows
    first_block = pl.program_id(1) == 0 if carry else None
    n_chunks = D_FF // FF_CHUNK

    def project(c):
        cs = slice(c * FF_CHUNK, (c + 1) * FF_CHUNK)
        return (jnp.dot(h, wg_ref[:, cs], preferred_element_type=F32),
                jnp.dot(h, wu_ref[:, cs], preferred_element_type=F32))

    acc = x
    ahead = project(0)
    for c in range(n_chunks):
        cs = slice(c * FF_CHUNK, (c + 1) * FF_CHUNK)
        u, up = ahead
        if c + 1 < n_chunks:
            ahead = project(c + 1)
        u1 = pltpu.roll(u, 1, 0)
        u2 = pltpu.roll(u, 2, 0)
        if carry:
            prev = carry_ref[:, cs]
            start1 = jnp.where(first_block, p1_ref[0:1, cs], prev[7:8, :])
            start2a = jnp.where(first_block, p2_ref[0:1, cs], prev[6:7, :])
            start2b = jnp.where(first_block, p2_ref[1:2, cs], prev[7:8, :])
            u1 = jnp.where(t_in_seq == 0, start1, u1)
            u2 = jnp.where(t_in_seq == 0, start2a, jnp.where(t_in_seq == 1, start2b, u2))
            carry_ref[:, cs] = u[tm - SUBLANES:, :]
        else:
            u1 = jnp.where(t_in_seq == 0, p1_ref[:, cs], u1)
            u2 = jnp.where(t_in_seq < 2, p2_ref[:, cs], u2)
        conv = cb_ref[:, cs] + u2 * cw_ref[0:1, cs] + u1 * cw_ref[1:2, cs] + u * cw_ref[2:3, cs]
        a = (_silu(conv) * up).astype(BF16)
        acc = acc + jnp.dot(a, wd_ref[cs, :], preferred_element_type=F32)
        if carry:
            u_ref[0, :, cs] = u[tm - SUBLANES:, :]
        else:
            u_ref[:, cs] = u
    y_ref[...] = acc


def _ffn_weights_specs(idx):
    once = pl.Buffered(1)
    return [pl.BlockSpec((1, D_MODEL), idx),
            pl.BlockSpec((D_MODEL, D_FF), idx, pipeline_mode=once),
            pl.BlockSpec((D_MODEL, D_FF), idx, pipeline_mode=once),
            pl.BlockSpec((CONV_W, D_FF), idx),
            pl.BlockSpec((1, D_FF), idx),
            pl.BlockSpec((D_FF, D_MODEL), idx, pipeline_mode=once)]


def _ffn_long(x3d, gain, wg, wu, cw, cb, wd, state, tm):
    bsz, t, _ = x3d.shape
    idx = lambda b, i: (0, 0)
    kern = functools.partial(_ffn_kernel, seq_rows=t, carry=True)
    nb = t // tm
    row = lambda w: pl.BlockSpec((tm, w), lambda b, i: (b * nb + i, 0))
    p1 = state[:, 1:2, :]
    y, u_last = pl.pallas_call(
        kern,
        out_shape=(jax.ShapeDtypeStruct((bsz * t, D_MODEL), F32),
                   jax.ShapeDtypeStruct((bsz, SUBLANES, D_FF), F32)),
        grid=(bsz, nb),
        in_specs=[row(D_MODEL)] + _ffn_weights_specs(idx)
                 + [pl.BlockSpec((None, 1, D_FF), lambda b, i: (b, 0, 0)),
                    pl.BlockSpec((None, 2, D_FF), lambda b, i: (b, 0, 0))],
        out_specs=(row(D_MODEL), pl.BlockSpec((1, SUBLANES, D_FF), lambda b, i: (b, 0, 0))),
        scratch_shapes=[pltpu.VMEM((SUBLANES, D_FF), F32)],
        compiler_params=_params("parallel", "arbitrary"),
        name="conv_ffn_long",
    )(x3d.reshape(bsz * t, D_MODEL), gain.reshape(1, D_MODEL), wg, wu, cw, cb.reshape(1, D_FF), wd, p1, state)
    return y.reshape(bsz, t, D_MODEL), u_last


def _ffn_short(x2d, gain, wg, wu, cw, cb, wd, p1, p2, seq_rows):
    n = x2d.shape[0]
    idx = lambda i: (0, 0)
    kern = functools.partial(_ffn_kernel, seq_rows=seq_rows, carry=False)
    row = lambda w: pl.BlockSpec((n, w), lambda i: (0, 0))
    return pl.pallas_call(
        kern,
        out_shape=(jax.ShapeDtypeStruct((n, D_MODEL), F32), jax.ShapeDtypeStruct((n, D_FF), F32)),
        grid=(1,),
        in_specs=[row(D_MODEL)] + _ffn_weights_specs(idx) + [row(D_FF), row(D_FF)],
        out_specs=(row(D_MODEL), row(D_FF)),
        scratch_shapes=[pltpu.VMEM((SUBLANES, D_FF), F32)],
        compiler_params=_params("arbitrary"),
        name="conv_ffn_short",
    )(x2d, gain.reshape(1, D_MODEL), wg, wu, cw, cb.reshape(1, D_FF), wd, p1, p2)


def _rope128(x, cos, sin_signed):
    return x * cos + pltpu.roll(x, HEAD_DIM // 2, 1) * sin_signed


def _rope64_pairs(x, cos, sin_lo, sin_hi):
    return (x * cos + pltpu.roll(x, LANES - IDX_DIM // 2, 1) * sin_lo
            + pltpu.roll(x, IDX_DIM // 2, 1) * sin_hi)


def _dsa_prep_kernel(x_ref, ng_ref, w_ref, c128_ref, s128_ref, c64_ref, s64lo_ref, s64hi_ref, qg_ref, kg_ref, ikg_ref,
                     q_ref, k_ref, kb_ref, v_ref, vt_ref, iq_ref, ik_ref, ikb_ref, iwt_ref, mq_ref, z_ref):
    tm = x_ref.shape[1]
    part_rows = min(tm, PREP_PART)
    n_parts = tm // part_rows

    def project(p):
        rs = slice(p * part_rows, (p + 1) * part_rows)
        hn = _rms(x_ref[0, rs, :], ng_ref[...]).astype(BF16)
        for j in range(DSA_COLS_PAD // PROJ_CHUNK):
            cs = slice(j * PROJ_CHUNK, (j + 1) * PROJ_CHUNK)
            z_ref[0, rs, cs] = jnp.dot(hn, w_ref[:, cs], preferred_element_type=F32)

    def finish(p):
        rs = slice(p * part_rows, (p + 1) * part_rows)
        mq_ref[0, rs, :] = z_ref[0, rs, DSA_MEM0:]
        c128, s128 = c128_ref[rs, :], s128_ref[rs, :]
        c64, s64lo, s64hi = c64_ref[rs, :], s64lo_ref[rs, :], s64hi_ref[rs, :]
        for h in range(N_Q_HEADS):
            sl = slice(DSA_Q0 + h * HEAD_DIM, DSA_Q0 + (h + 1) * HEAD_DIM)
            q = _rope128(_rms(z_ref[0, rs, sl], qg_ref[...]), c128, s128)
            q_ref[0, rs, h * HEAD_DIM:(h + 1) * HEAD_DIM] = (q * QK_SCALE_LOG2E).astype(BF16)
        for h in range(N_KV_HEADS):
            sl = slice(DSA_K0 + h * HEAD_DIM, DSA_K0 + (h + 1) * HEAD_DIM)
            k = _rope128(_rms(z_ref[0, rs, sl], kg_ref[...]), c128, s128)
            k_ref[0, rs, h * HEAD_DIM:(h + 1) * HEAD_DIM] = k
            kb_ref[0, rs, h * HEAD_DIM:(h + 1) * HEAD_DIM] = k.astype(BF16)
        v = z_ref[0, rs, DSA_V0:DSA_V0 + N_KV_HEADS * HEAD_DIM]
        v_ref[0, rs, :] = v
        vt_ref[0, 0, :, rs] = v.T.astype(BF16)
        for pr in range(IDX_HEADS // 2):
            sl = slice(DSA_IQ0 + pr * LANES, DSA_IQ0 + (pr + 1) * LANES)
            iq_ref[0, rs, pr * LANES:(pr + 1) * LANES] = _rope64_pairs(z_ref[0, rs, sl], c64, s64lo, s64hi)
        ik = _rope64_pairs(_rms(z_ref[0, rs, DSA_IK0:DSA_IK0 + LANES], ikg_ref[...]), c64, s64lo, s64hi)
        ik_ref[0, rs, :] = ik
        ik_hi, ik_lo = _split_bf16(ik)
        hi_lo = jnp.where(lax.broadcasted_iota(I32, ik.shape, 1) < IDX_DIM, ik_hi, ik_lo)
        ikb_ref[0, rs, :LANES] = hi_lo
        ikb_ref[0, rs, LANES:] = hi_lo
        iw = z_ref[0, rs, DSA_IW0:DSA_IW0 + LANES] * (IDX_HEADS ** -0.5 * IDX_DIM ** -0.5)
        iwt_ref[0, :, rs] = iw.T[:IDX_HEADS, :]

    project(0)
    for p in range(n_parts):
        if p + 1 < n_parts:
            project(p + 1)
        finish(p)


def _dsa_prep(x3d, norm_gain, w_bf16, tabs, q_gain, k_gain, ik_gain, tm):
    bsz, t, _ = x3d.shape
    nb = t // tm
    tab = pl.BlockSpec((tm, LANES), lambda b, i: (i, 0))
    gain = pl.BlockSpec((1, LANES), lambda b, i: (0, 0))
    rows = lambda w: pl.BlockSpec((1, tm, w), lambda b, i: (b, i, 0))
    const = lambda shape, **kw: pl.BlockSpec(shape, lambda b, i: (0,) * len(shape), **kw)
    kvw = N_KV_HEADS * HEAD_DIM
    sds = jax.ShapeDtypeStruct
    return pl.pallas_call(
        _dsa_prep_kernel,
        out_shape=(sds((bsz, t, TOK_WIDTH), BF16),
                   sds((bsz, t, kvw), F32), sds((bsz, t, kvw), BF16),
                   sds((bsz, t, kvw), F32), sds((bsz, nb, kvw, tm), BF16),
                   sds((bsz, t, IDX_HEADS * IDX_DIM), F32),
                   sds((bsz, t, LANES), F32),
                   sds((bsz, t, 2 * LANES), BF16),
                   sds((bsz, IDX_HEADS, t), F32),
                   sds((bsz, t, MEM_WIDTH), F32)),
        grid=(bsz, nb),
        in_specs=[rows(D_MODEL), const((1, D_MODEL)),
                  const((D_MODEL, DSA_COLS_PAD), pipeline_mode=pl.Buffered(1)),
                  tab, tab, tab, tab, tab, gain, gain, gain],
        out_specs=(rows(TOK_WIDTH), rows(kvw), rows(kvw), rows(kvw),
                   pl.BlockSpec((1, 1, kvw, tm), lambda b, i: (b, i, 0, 0)),
                   rows(IDX_HEADS * IDX_DIM), rows(LANES), rows(2 * LANES),
                   pl.BlockSpec((1, IDX_HEADS, tm), lambda b, i: (b, 0, i)),
                   rows(MEM_WIDTH)),
        scratch_shapes=[pltpu.VMEM((1, tm, DSA_COLS_PAD), F32)],
        compiler_params=_params("parallel", "parallel"),
        name="dsa_prep",
    )(x3d, norm_gain.reshape(1, D_MODEL), w_bf16, *tabs,
      q_gain.reshape(1, LANES), k_gain.reshape(1, LANES), jnp.tile(ik_gain, 2).reshape(1, LANES))


def _rope_tables(pos):
    pos = pos.astype(F32)[:, None]

    def cs(half):
        inv = ROPE_THETA ** (-jnp.arange(half, dtype=F32) / half)
        ang = pos * inv[None, :]
        return jnp.cos(ang), jnp.sin(ang)

    c, s = cs(HEAD_DIM // 2)
    c128 = jnp.concatenate([c, c], axis=1)
    s128 = jnp.concatenate([-s, s], axis=1)
    c, s = cs(IDX_DIM // 2)
    z = jnp.zeros_like(s)
    c64 = jnp.concatenate([c, c, c, c], axis=1)
    s64lo = jnp.concatenate([-s, z, -s, z], axis=1)
    s64hi = jnp.concatenate([z, s, z, s], axis=1)
    return c128, s128, c64, s64lo, s64hi


def _ordered_bits_to_float(u):
    key = u ^ INT_MIN
    bits = key ^ ((key >> 31) & 0x7FFFFFFF)
    return lax.bitcast_convert_type(bits, F32)


def _topk_threshold(count_ge, count_gt, count_eq_before, any_true, shape, idx_bits):
    def value_step(i, prefix):
        cand = prefix | jnp.left_shift(jnp.int32(1), 31 - i)
        below_neg_inf = jnp.logical_and(cand >= 0, cand < 0x00800000)
        ok = jnp.logical_or(count_ge(_ordered_bits_to_float(cand)) >= TOPK, below_neg_inf)
        return jnp.where(ok, cand, prefix)

    thr = _ordered_bits_to_float(lax.fori_loop(0, 32, value_step, jnp.zeros(shape, I32)))
    n_gt = count_gt(thr)
    need = TOPK - n_gt
    n_eq = count_ge(thr) - n_gt
    finite_thr = thr > -jnp.inf
    excess = jnp.logical_and(n_eq > need, finite_thr)

    def index_search():
        def index_step(i, cut):
            cand = cut | jnp.left_shift(jnp.int32(1), idx_bits - 1 - i)
            return jnp.where(count_eq_before(thr, cand) < need, cand, cut)
        return lax.fori_loop(0, idx_bits, index_step, jnp.zeros(shape, I32))

    everything = jnp.full(shape, 2 ** idx_bits - 1, I32)
    cut = lax.cond(any_true(excess), index_search, lambda: everything)
    return thr, jnp.where(finite_thr, cut, -1)


def _dsa_prompt_kernel(q_ref, iq_ref, iwt_ref, ik_ref, k_ref, vt_ref, prev_ref, o_ref,
                       score_ref, iqm_ref, qm_ref, acc_ref, *, n_kb):
    del prev_ref
    q0 = ((n_kb - 1) * (KEY_BLOCK // Q_TILE) + pl.program_id(1)) * Q_TILE
    lane = lax.broadcasted_iota(I32, (Q_TILE, LANES), 1)
    first_half = lane < IDX_DIM
    for p in range(IDX_HEADS // 2):
        pair = iq_ref[0, :, p * LANES:(p + 1) * LANES]
        swapped = pltpu.roll(pair, IDX_DIM, 1)
        for half in range(2):
            dup = jnp.where(first_half, pair, swapped) if half == 0 else jnp.where(first_half, swapped, pair)
            hi, lo = _split_bf16(dup)
            rows = slice(half * Q_TILE, (half + 1) * Q_TILE)
            iqm_ref[p, rows, :LANES] = hi
            iqm_ref[p, rows, LANES:] = lo
    for h in range(N_Q_HEADS):
        g = h // Q_PER_KV
        qh = q_ref[0, :, h * HEAD_DIM:(h + 1) * HEAD_DIM]
        rows = slice(h * Q_TILE, (h + 1) * Q_TILE)
        qm_ref[rows, g * HEAD_DIM:(g + 1) * HEAD_DIM] = qh
        qm_ref[rows, (1 - g) * HEAD_DIM:(2 - g) * HEAD_DIM] = jnp.zeros_like(qh)
    q_pos = q0 + lax.broadcasted_iota(I32, (1, Q_TILE), 1)
    key_row = lax.broadcasted_iota(I32, (KEY_BLOCK, 1), 0)

    blocks = [slice(kb * KEY_BLOCK, (kb + 1) * KEY_BLOCK) for kb in range(n_kb)]

    for kb in range(n_kb):
        ik = ik_ref[0, blocks[kb], :]
        acc = jnp.zeros((KEY_BLOCK, Q_TILE), F32)
        for p in range(IDX_HEADS // 2):
            d = _dot_nt(ik, iqm_ref[p])
            for half in range(2):
                h = 2 * p + half
                acc = acc + iwt_ref[0, h:h + 1, :] * jnp.maximum(d[:, half * Q_TILE:(half + 1) * Q_TILE], 0.0)
        if kb == n_kb - 1:
            acc = jnp.where(kb * KEY_BLOCK + key_row <= q_pos, acc, -jnp.inf)
        score_ref[blocks[kb], :] = acc

    def count(pred):
        part = jnp.zeros((SUBLANES, Q_TILE), I32)
        for kb in range(n_kb):
            hit = pred(score_ref[blocks[kb], :], kb * KEY_BLOCK + key_row).astype(I32)
            part = part + jnp.sum(hit.reshape(KEY_BLOCK // SUBLANES, SUBLANES, Q_TILE), axis=0)
        return jnp.sum(part, axis=0, keepdims=True)

    thr, cut = _topk_threshold(
        count_ge=lambda c: count(lambda s, idx: s >= c),
        count_gt=lambda c: count(lambda s, idx: s > c),
        count_eq_before=lambda t, j: count(lambda s, idx: jnp.logical_and(s == t, idx < j)),
        any_true=lambda m: jnp.max(m.astype(I32)) > 0,
        shape=(1, Q_TILE), idx_bits=13)

    m = [jnp.full((1, Q_TILE), NEG_BIG, F32)] * N_Q_HEADS
    l = [jnp.zeros((1, Q_TILE), F32)] * N_Q_HEADS
    sub_row = lax.broadcasted_iota(I32, (ATTN_SUB, 1), 0)
    for ab in range(n_kb * KEY_BLOCK // ATTN_SUB):
        kb, off = divmod(ab * ATTN_SUB, KEY_BLOCK)
        keys = slice(ab * ATTN_SUB, (ab + 1) * ATTN_SUB)
        sc = score_ref[keys, :]
        sel = jnp.logical_or(sc > thr, jnp.logical_and(sc == thr, ab * ATTN_SUB + sub_row <= cut))
        k_blk = k_ref[0, keys, :]
        alphas, ps = [], []
        for j in range(N_Q_HEADS // 2):
            s2 = _dot_nt(k_blk, qm_ref[2 * j * Q_TILE:2 * (j + 1) * Q_TILE, :])
            for half in range(2):
                h = 2 * j + half
                s = jnp.where(sel, s2[:, half * Q_TILE:(half + 1) * Q_TILE], NEG_BIG)
                m_new = jnp.maximum(m[h], jnp.max(s, axis=0, keepdims=True))
                alpha = jnp.exp2(m[h] - m_new)
                p = jnp.exp2(s - m_new)
                l[h] = alpha * l[h] + jnp.sum(p, axis=0, keepdims=True)
                m[h] = m_new
                alphas.append(alpha)
                ps.append(p.astype(BF16))
        for g in range(N_KV_HEADS):
            vtg = vt_ref[0, kb, g * HEAD_DIM:(g + 1) * HEAD_DIM, off:off + ATTN_SUB]
            pv = jnp.dot(vtg, jnp.concatenate(ps[g * Q_PER_KV:(g + 1) * Q_PER_KV], axis=1),
                         preferred_element_type=F32)
            for r in range(Q_PER_KV):
                h = g * Q_PER_KV + r
                part = pv[:, r * Q_TILE:(r + 1) * Q_TILE]
                acc_ref[h] = part if ab == 0 else alphas[h] * acc_ref[h] + part
    for h in range(N_Q_HEADS):
        o_ref[0, :, h * HEAD_DIM:(h + 1) * HEAD_DIM] = (acc_ref[h] / l[h]).T


def _dsa_prompt(q, iq, iwt, ikb, kb, vt):
    bsz, t, _ = q.shape
    kvw = N_KV_HEADS * HEAD_DIM
    tiles = KEY_BLOCK // Q_TILE
    out = jnp.zeros((bsz, t, TOK_WIDTH), F32)
    for n_kb in range(1, t // KEY_BLOCK + 1):
        tile = lambda b, i, n_kb=n_kb: (n_kb - 1) * tiles + i
        keys = lambda w, n_kb=n_kb: pl.BlockSpec((1, n_kb * KEY_BLOCK, w), lambda b, i: (b, 0, 0))
        out = pl.pallas_call(
            functools.partial(_dsa_prompt_kernel, n_kb=n_kb),
            out_shape=jax.ShapeDtypeStruct((bsz, t, TOK_WIDTH), F32),
            grid=(bsz, tiles),
            in_specs=[pl.BlockSpec((1, Q_TILE, TOK_WIDTH), lambda b, i, tile=tile: (b, tile(b, i), 0)),
                      pl.BlockSpec((1, Q_TILE, IDX_HEADS * IDX_DIM), lambda b, i, tile=tile: (b, tile(b, i), 0)),
                      pl.BlockSpec((1, IDX_HEADS, Q_TILE), lambda b, i, tile=tile: (b, 0, tile(b, i))),
                      keys(2 * LANES), keys(kvw),
                      pl.BlockSpec((1, n_kb, kvw, KEY_BLOCK), lambda b, i: (b, 0, 0, 0)),
                      pl.BlockSpec(memory_space=pl.ANY)],
            out_specs=pl.BlockSpec((1, Q_TILE, TOK_WIDTH), lambda b, i, tile=tile: (b, tile(b, i), 0)),
            scratch_shapes=[pltpu.VMEM((n_kb * KEY_BLOCK, Q_TILE), F32),
                            pltpu.VMEM((IDX_HEADS // 2, 2 * Q_TILE, 2 * LANES), BF16),
                            pltpu.VMEM((N_Q_HEADS * Q_TILE, N_KV_HEADS * HEAD_DIM), BF16),
                            pltpu.VMEM((N_Q_HEADS, HEAD_DIM, Q_TILE), F32)],
            input_output_aliases={6: 0},
            compiler_params=_params("parallel", "arbitrary"),
            name=f"dsa_prompt_{n_kb}",
        )(q, iq, iwt, ikb, kb, vt, out)
    return out


SAMPLE_ROWS = 16
SAMPLE_T = 4
N_PAGES = PAST_LEN // PAGE_SIZE
N_KEY_PAGES = N_PAGES + 1


def _sample_score_kernel(pt_ref, iq4_ref, iw_ref, ikn_ref, *rest):
    page_refs = rest[:SCORE_PAGES_PER_STEP]
    (score_ref,) = rest[SCORE_PAGES_PER_STEP:]
    step = pl.program_id(1)
    iq_hi, iq_lo = _split_bf16(iq4_ref[0])
    iq_parts = jnp.where(lax.broadcasted_iota(I32, iq_hi.shape, 1) < 2 * IDX_DIM, iq_hi, iq_lo)
    w = iw_ref[0]

    def page_scores(ik_t, page, allowed):
        k_hi, k_lo = _split_bf16(ik_t)
        d = jnp.dot(iq_parts, jnp.concatenate([k_hi, k_lo, k_hi, k_lo], axis=0), preferred_element_type=F32)
        sc = jnp.sum((jnp.maximum(d, 0.0) * w).reshape(SAMPLE_T, IDX_HEADS, PAGE_SIZE), axis=1)
        if allowed is not None:
            sc = jnp.where(allowed, sc, -jnp.inf)
        for r in range(SAMPLE_ROWS // SAMPLE_T):
            score_ref[0, page, r * SAMPLE_T:(r + 1) * SAMPLE_T, :] = sc

    for i in range(SCORE_PAGES_PER_STEP):
        page_scores(page_refs[i][0], step * SCORE_PAGES_PER_STEP + i, None)

    @pl.when(step == pl.num_programs(1) - 1)
    def _():
        new_idx = lax.broadcasted_iota(I32, (SAMPLE_T, PAGE_SIZE), 1)
        page_scores(ikn_ref[0], N_PAGES, new_idx <= lax.broadcasted_iota(I32, (SAMPLE_T, PAGE_SIZE), 0))


def _sample_scores(page_table, iq4, iw_col, ik_new_t, pool_ik_t):
    bsz = iq4.shape[0]
    page_spec = lambda i: pl.BlockSpec((1, IDX_DIM, PAGE_SIZE),
                                       lambda b, s, pt, i=i: (pt[b, s * SCORE_PAGES_PER_STEP + i], 0, 0))
    per_b = lambda shape: pl.BlockSpec((1,) + shape, lambda b, s, pt: (b,) + (0,) * len(shape))
    grid_spec = pltpu.PrefetchScalarGridSpec(
        num_scalar_prefetch=1,
        grid=(bsz, N_PAGES // SCORE_PAGES_PER_STEP),
        in_specs=[per_b((SAMPLE_T * IDX_HEADS, 4 * IDX_DIM)), per_b((SAMPLE_T * IDX_HEADS, 1)),
                  per_b((IDX_DIM, PAGE_SIZE))] + [page_spec(i) for i in range(SCORE_PAGES_PER_STEP)],
        out_specs=per_b((N_KEY_PAGES, SAMPLE_ROWS, PAGE_SIZE)))
    return pl.pallas_call(
        _sample_score_kernel,
        out_shape=jax.ShapeDtypeStruct((bsz, N_KEY_PAGES, SAMPLE_ROWS, PAGE_SIZE), F32),
        grid_spec=grid_spec,
        compiler_params=_params("parallel", "arbitrary"),
        name="dsa_sample_scores",
    )(page_table, iq4, iw_col, ik_new_t, *([pool_ik_t] * SCORE_PAGES_PER_STEP))


def _sample_select_kernel(score_ref, thr_ref, cut_ref):
    n_q, n_k = score_ref.shape
    key_idx = lax.broadcasted_iota(I32, (1, n_k), 1)

    def count(pred):
        return jnp.sum(pred(score_ref[...], key_idx).astype(I32), axis=-1, keepdims=True)

    thr, cut = _topk_threshold(
        count_ge=lambda c: count(lambda s, idx: s >= c),
        count_gt=lambda c: count(lambda s, idx: s > c),
        count_eq_before=lambda t, j: count(lambda s, idx: jnp.logical_and(s == t, idx < j)),
        any_true=lambda m: jnp.max(m.astype(I32)) > 0,
        shape=(n_q, 1), idx_bits=14)
    thr_ref[...] = jnp.broadcast_to(thr, thr_ref.shape)
    cut_ref[...] = jnp.broadcast_to(cut, cut_ref.shape)


def _sample_select(scores2d):
    n_q, n_k = scores2d.shape
    whole = lambda shape: pl.BlockSpec(shape, lambda i: (0, 0))
    return pl.pallas_call(
        _sample_select_kernel,
        out_shape=(jax.ShapeDtypeStruct((n_q, LANES), F32), jax.ShapeDtypeStruct((n_q, LANES), I32)),
        grid=(1,),
        in_specs=[whole((n_q, n_k))],
        out_specs=(whole((n_q, LANES)), whole((n_q, LANES))),
        compiler_params=_params("arbitrary"),
        name="dsa_sample_select",
    )(scores2d)


def _sample_attn_kernel(pt_ref, q_ref, sc_ref, thr_ref, cut_ref, kn_ref, vn_ref, *rest):
    k_pages = rest[:ATTN_PAGES_PER_STEP]
    v_pages = rest[ATTN_PAGES_PER_STEP:2 * ATTN_PAGES_PER_STEP]
    o_ref, m_ref, l_ref, acc_ref = rest[2 * ATTN_PAGES_PER_STEP:]
    step = pl.program_id(1)

    @pl.when(step == 0)
    def _():
        m_ref[...] = jnp.full(m_ref.shape, NEG_BIG, F32)
        l_ref[...] = jnp.zeros(l_ref.shape, F32)
        acc_ref[...] = jnp.zeros(acc_ref.shape, F32)

    thr = thr_ref[0]
    cut = cut_ref[0]
    lane = lax.broadcasted_iota(I32, (SAMPLE_ROWS, PAGE_SIZE), 1)

    def attend(pages):
        sels = []
        for page, _, _ in pages:
            sc = sc_ref[0, page]
            sels.append(jnp.logical_or(sc > thr, jnp.logical_and(sc == thr, page * PAGE_SIZE + lane <= cut)))
        cat = lambda parts, axis: parts[0] if len(parts) == 1 else jnp.concatenate(parts, axis=axis)
        sel = cat(sels, 1)
        for g in range(N_KV_HEADS):
            rows_g = pl.ds(g, PAGE_SIZE, stride=N_KV_HEADS)
            k_all = cat([k[0, rows_g, :].astype(BF16) for _, k, _ in pages], 0)
            v_all = cat([v[0, rows_g, :].astype(BF16) for _, _, v in pages], 0)
            s = jnp.where(sel, _dot_nt(q_ref[0, g], k_all), NEG_BIG)
            m_old = m_ref[g]
            m_new = jnp.maximum(m_old, jnp.max(s, axis=-1, keepdims=True))
            alpha = jnp.exp2(m_old - m_new)
            p = jnp.exp2(s - m_new)
            l_ref[g] = alpha * l_ref[g] + jnp.sum(p, axis=-1, keepdims=True)
            acc_ref[g] = alpha * acc_ref[g] + jnp.dot(p.astype(BF16), v_all, preferred_element_type=F32)
            m_ref[g] = m_new

    attend([(step * ATTN_PAGES_PER_STEP + i, k_pages[i], v_pages[i]) for i in range(ATTN_PAGES_PER_STEP)])

    @pl.when(step == pl.num_programs(1) - 1)
    def _():
        attend([(N_PAGES, kn_ref, vn_ref)])
        for g in range(N_KV_HEADS):
            o_ref[0, g] = acc_ref[g] / l_ref[g]


def _sample_attn(page_table, q16, scores, thr, cut, k_new_pad, v_new_pad, pool_k, pool_v):
    bsz = q16.shape[0]
    page_rows = PAGE_SIZE * N_KV_HEADS
    page_spec = lambda i: pl.BlockSpec((1, page_rows, HEAD_DIM),
                                       lambda b, s, pt, i=i: (pt[b, s * ATTN_PAGES_PER_STEP + i], 0, 0))
    per_b = lambda shape: pl.BlockSpec((1,) + shape, lambda b, s, pt: (b,) + (0,) * len(shape))
    pages = [page_spec(i) for i in range(ATTN_PAGES_PER_STEP)]
    grid_spec = pltpu.PrefetchScalarGridSpec(
        num_scalar_prefetch=1,
        grid=(bsz, N_PAGES // ATTN_PAGES_PER_STEP),
        in_specs=[per_b((N_KV_HEADS, SAMPLE_ROWS, HEAD_DIM)), per_b((N_KEY_PAGES, SAMPLE_ROWS, PAGE_SIZE)),
                  per_b((SAMPLE_ROWS, LANES)), per_b((SAMPLE_ROWS, LANES)),
                  per_b((page_rows, HEAD_DIM)), per_b((page_rows, HEAD_DIM))] + pages + pages,
        out_specs=per_b((N_KV_HEADS, SAMPLE_ROWS, HEAD_DIM)),
        scratch_shapes=[pltpu.VMEM((N_KV_HEADS, SAMPLE_ROWS, 1), F32),
                        pltpu.VMEM((N_KV_HEADS, SAMPLE_ROWS, 1), F32),
                        pltpu.VMEM((N_KV_HEADS, SAMPLE_ROWS, HEAD_DIM), F32)])
    return pl.pallas_call(
        _sample_attn_kernel,
        out_shape=jax.ShapeDtypeStruct((bsz, N_KV_HEADS, SAMPLE_ROWS, HEAD_DIM), F32),
        grid_spec=grid_spec,
        compiler_params=_params("parallel", "arbitrary"),
        name="dsa_sample_attn",
    )(page_table, q16, scores, thr, cut, k_new_pad, v_new_pad,
      *([pool_k] * ATTN_PAGES_PER_STEP), *([pool_v] * ATTN_PAGES_PER_STEP))


def _pad_cols(w, width):
    return jnp.pad(w, ((0, 0), (0, width - w.shape[1])))


def _dsa_weight(w):
    sizes = (TOK_WIDTH, N_KV_HEADS * HEAD_DIM, N_KV_HEADS * HEAD_DIM, IDX_HEADS * IDX_DIM, IDX_DIM, IDX_HEADS, MEM_WIDTH)
    offs = [0]
    for s in sizes:
        offs.append(offs[-1] + s)
    q, k, v, iq, ik, iw, mq = [w[:, offs[i]:offs[i + 1]] for i in range(len(sizes))]
    return jnp.concatenate([q, k, v, iq, ik, ik, _pad_cols(iw, LANES), mq], axis=1).astype(BF16)


def kernel(x_prompt, x_sample, cache_k, cache_v, cache_idx_k, cache_mem_k, cache_mem_v, state_hgrn, state_ffn_conv, page_table, mem_prompt, norm_mix, norm_ffn, w_in_hgrn, hgrn_lb_logits, hgrn_out_norm, w_in_dsa, dsa_q_norm, dsa_k_norm, idx_k_norm, w_mem_kv, mem_q_norm, mem_k_norm, w_out, w_ffn_gate, w_ffn_up, ffn_conv_w, ffn_conv_b, w_ffn_down):
    bp, seq, _ = x_prompt.shape
    bs, t_s, _ = x_sample.shape
    assert t_s == SAMPLE_T and seq % KEY_BLOCK == 0
    n_s = bs * SAMPLE_T_PAD
    lower_bounds = jnp.cumsum(jax.nn.softmax(hgrn_lb_logits.astype(F32), axis=0), axis=0)

    xp = x_prompt
    xs = jnp.pad(x_sample, ((0, 0), (0, SAMPLE_T_PAD - t_s), (0, 0)))
    mem2d = mem_prompt.reshape(bp * N_MEM, D_MODEL)
    tabs_p = _rope_tables(jnp.arange(seq))
    pos_s = PAST_LEN + jnp.arange(SAMPLE_T_PAD)
    tabs_s = tuple(jnp.tile(tb, (bs, 1)) for tb in _rope_tables(pos_s))
    row_t = jnp.arange(n_s) % SAMPLE_T_PAD

    hgrn_p, hgrn_s = [], []
    kp_l, vp_l, ikp_l, ks_l, vs_l, iks_l = [], [], [], [], [], []
    mk_l, mv_l, cvp_l, cvs_l = [], [], [], []
    for layer in range(DEPTH):
        j = layer // 2
        if layer % 2 == 0:
            mq_col = 0
            w_in = w_in_hgrn[j].astype(BF16)
            s0 = jnp.zeros((bp, HGRN_HEADS, HGRN_DK, HGRN_DK), F32)
            tp, mqp, sp = _hgrn(xp, norm_mix[layer], w_in, lower_bounds[j], hgrn_out_norm[j], s0,
                                HGRN_CHUNK, HGRN_SUB, 512, HGRN_CHUNK)
            zs = _norm_proj(xs.reshape(n_s, D_MODEL), norm_mix[layer], w_in).reshape(bs, SAMPLE_T_PAD, -1)
            ts, ss = _hgrn_projected(zs, lower_bounds[j], hgrn_out_norm[j], state_hgrn[j], SAMPLE_T_PAD,
                                     SAMPLE_T_PAD, SAMPLE_T_PAD, math.gcd(t_s, HGRN_CHUNK))
            mqs = zs[..., HGRN_MEM0:]
            hgrn_p.append(sp.astype(state_hgrn.dtype))
            hgrn_s.append(ss.astype(state_hgrn.dtype))
        else:
            mq_col = 0
            prep_args = (norm_mix[layer], _dsa_weight(w_in_dsa[j]))
            gains = (dsa_q_norm[j], dsa_k_norm[j], idx_k_norm[j])
            qp, kp, kpb, vp, vtp, iqp, ikp, ikpb, iwtp, mqp = _dsa_prep(xp, *prep_args, tabs_p, *gains, KEY_BLOCK)
            tp = _dsa_prompt(qp, iqp, iwtp, ikpb, kpb, vtp)
            kp_l.append(kp.reshape(bp, seq, N_KV_HEADS, HEAD_DIM))
            vp_l.append(vp.reshape(bp, seq, N_KV_HEADS, HEAD_DIM))
            ikp_l.append(ikp[..., :IDX_DIM])

            qs, kn, _, vn, _, iqs, ikn, _, iwts, mqs = _dsa_prep(
                xs.reshape(1, n_s, D_MODEL), *prep_args, tabs_s, *gains, n_s)
            mqs = mqs.reshape(bs, SAMPLE_T_PAD, MEM_WIDTH)
            sel_t = lambda a: a.reshape(bs, SAMPLE_T_PAD, -1)[:, :t_s]
            kn, vn, ikn = sel_t(kn), sel_t(vn), sel_t(ikn)[..., :IDX_DIM]
            ks_l.append(kn.reshape(bs, t_s, N_KV_HEADS, HEAD_DIM))
            vs_l.append(vn.reshape(bs, t_s, N_KV_HEADS, HEAD_DIM))
            iks_l.append(ikn)
            iq4 = jnp.tile(sel_t(iqs).reshape(bs, t_s * IDX_HEADS, IDX_DIM), (1, 1, 4))
            iw_col = sel_t(iwts[0].T).reshape(bs, t_s * IDX_HEADS, 1)
            ikn_t = jnp.pad(ikn.transpose(0, 2, 1), ((0, 0), (0, 0), (0, PAGE_SIZE - t_s)))
            scores = _sample_scores(page_table, iq4, iw_col, ikn_t, cache_idx_k[j].transpose(0, 2, 1))
            per_query = scores[:, :, :t_s, :].transpose(0, 2, 1, 3).reshape(bs * t_s, N_KEY_PAGES * PAGE_SIZE)
            thr, cut = _sample_select(per_query)
            rep = lambda a: jnp.tile(a.reshape(bs, t_s, LANES), (1, SAMPLE_ROWS // t_s, 1))
            thr, cut = rep(thr), rep(cut)
            q16 = sel_t(qs).reshape(bs, t_s, N_KV_HEADS, Q_PER_KV, HEAD_DIM).transpose(0, 2, 3, 1, 4)
            q16 = jnp.pad(q16.reshape(bs, N_KV_HEADS, Q_PER_KV * t_s, HEAD_DIM),
                          ((0, 0), (0, 0), (0, SAMPLE_ROWS - Q_PER_KV * t_s), (0, 0)))
            page_rows = lambda a: a.reshape(a.shape[0], -1, HEAD_DIM)
            pad_page = lambda a: jnp.pad(page_rows(a), ((0, 0), (0, (PAGE_SIZE - t_s) * N_KV_HEADS), (0, 0)))
            o16 = _sample_attn(page_table, q16, scores, thr, cut, pad_page(kn), pad_page(vn),
                               page_rows(cache_k[j]), page_rows(cache_v[j]))
            o = o16[:, :, :Q_PER_KV * t_s].reshape(bs, N_KV_HEADS, Q_PER_KV, t_s, HEAD_DIM)
            ts = jnp.pad(o.transpose(0, 3, 1, 2, 4).reshape(bs, t_s, TOK_WIDTH),
                         ((0, 0), (0, SAMPLE_T_PAD - t_s), (0, 0)))

        mkp, mvp = _mem_kv(mem2d, w_mem_kv[layer].astype(BF16), mem_k_norm[layer])
        mk_l.append(mkp.reshape(bp, N_MEM, MEM_HEADS, MEM_HD))
        mv_l.append(mvp.reshape(bp, N_MEM, MEM_HEADS, MEM_HD))
        w_o = w_out[layer].astype(BF16)
        xp = _mem_attn_out(mqp, mq_col, mem_q_norm[layer], mkp.reshape(1, bp, N_MEM, MEM_WIDTH),
                           mvp.reshape(1, bp, N_MEM, MEM_WIDTH), 0, False, xp, tp, w_o, 512)
        mem_t = lambda a: a.transpose(0, 1, 3, 4, 2).reshape(DEPTH, bs, MEM_WIDTH, N_MEM)
        xs2 = _mem_attn_out(mqs, mq_col, mem_q_norm[layer], mem_t(cache_mem_k), mem_t(cache_mem_v), layer,
                            True, xs, ts, w_o, SAMPLE_T_PAD).reshape(n_s, D_MODEL)

        wg, wu, wd = (w.astype(BF16) for w in (w_ffn_gate[layer], w_ffn_up[layer], w_ffn_down[layer]))
        ffn_w = (norm_ffn[layer], wg, wu, ffn_conv_w[layer], ffn_conv_b[layer], wd)
        xp, u_last = _ffn_long(xp, *ffn_w, jnp.zeros((bp, CONV_W - 1, D_FF), F32), 256)
        cvp_l.append(u_last[:, SUBLANES - (CONV_W - 1):, :])
        st = state_ffn_conv[layer]
        zrow = jnp.zeros((bs, SAMPLE_T_PAD - 1, D_FF), F32)
        p1 = jnp.concatenate([st[:, 1:2], zrow], axis=1).reshape(n_s, D_FF)
        p2 = jnp.concatenate([st, zrow[:, 1:]], axis=1).reshape(n_s, D_FF)
        xs2, u_s = _ffn_short(xs2, *ffn_w, p1, p2, SAMPLE_T_PAD)
        cvs_l.append(u_s.reshape(bs, SAMPLE_T_PAD, D_FF)[:, t_s - (CONV_W - 1):t_s])
        xs = jnp.where((row_t < t_s)[:, None], xs2, 0.0).reshape(bs, SAMPLE_T_PAD, D_MODEL)

    return (xp, xs[:, :t_s],
            jnp.stack(hgrn_p), jnp.stack(hgrn_s),
            jnp.stack(kp_l), jnp.stack(vp_l), jnp.stack(ikp_l),
            jnp.stack(ks_l), jnp.stack(vs_l), jnp.stack(iks_l),
            jnp.stack(mk_l), jnp.stack(mv_l),
            jnp.stack(cvp_l), jnp.stack(cvs_l))
```

```python
import functools
import math

import jax
import jax.numpy as jnp
from jax import lax
from jax.experimental import pallas as pl
from jax.experimental.pallas import tpu as pltpu

D_MODEL = 1024
DEPTH = 2
PAST_LEN = 8192
PAGE_SIZE = 128
MEM_WIDTH = 256
TOK_WIDTH = 768
MEM_HEADS = 4
MEM_HD = 64
N_MEM = 256
HGRN_DK = 128
HGRN_HEADS = 6
HGRN_CHUNK = 32
HEAD_DIM = 128
N_Q_HEADS = 6
N_KV_HEADS = 2
Q_PER_KV = N_Q_HEADS // N_KV_HEADS
IDX_HEADS = 8
IDX_DIM = 64
TOPK = 256
ROPE_THETA = 10000.0
D_FF = 2816
CONV_W = 3
EPS = 1e-6

LANES = 128
SUBLANES = 8
VMEM_LIMIT_BYTES = 56 * 1024 * 1024
FF_CHUNK = 256
HGRN_SUB = 256
PROJ_CHUNK = 256
PREP_PART = 256
KEY_BLOCK = 512
Q_TILE = 128
ATTN_SUB = 512
SAMPLE_T_PAD = 8
SCORE_PAGES_PER_STEP = 64
ATTN_PAGES_PER_STEP = 16
NEG_BIG = -1e30
QK_SCALE_LOG2E = HEAD_DIM ** -0.5 * math.log2(math.e)
F32 = jnp.float32
BF16 = jnp.bfloat16
I32 = jnp.int32
INT_MIN = -2 ** 31

DSA_Q0, DSA_K0, DSA_V0, DSA_IQ0, DSA_IK0, DSA_IW0, DSA_MEM0, DSA_COLS_PAD = 0, 768, 1024, 1280, 1792, 1920, 2048, 2304
HGRN_MEM0 = 4 * TOK_WIDTH


def _params(*sem):
    return pltpu.CompilerParams(dimension_semantics=sem, vmem_limit_bytes=VMEM_LIMIT_BYTES)


def _rms(x, gain):
    ms = jnp.mean(x * x, axis=-1, keepdims=True)
    return x * lax.rsqrt(ms + EPS) * gain


def _split_bf16(x):
    hi = x.astype(BF16)
    lo = (x - hi.astype(F32)).astype(BF16)
    return hi, lo


def _group_mean_sq(x, group):
    n = x.shape[-1]
    r = lax.broadcasted_iota(I32, (n, n), 0) // group
    c = lax.broadcasted_iota(I32, (n, n), 1) // group
    ones = jnp.where(r == c, 1.0, 0.0).astype(BF16)
    hi, lo = _split_bf16(x * x)
    s = jnp.dot(hi, ones, preferred_element_type=F32) + jnp.dot(lo, ones, preferred_element_type=F32)
    return s * (1.0 / group)


def _dot_nt(a, b):
    return lax.dot_general(a, b, (((1,), (1,)), ((), ())), preferred_element_type=F32)


def _dot_tn(a, b):
    return lax.dot_general(a, b, (((0,), (0,)), ((), ())), preferred_element_type=F32)


def _silu(x):
    return x * jax.nn.sigmoid(x)


def _hgrn_kernel(*refs, chunk, sub, n_sub, valid_rows, fused):
    if fused:
        x_ref, ng_ref, w_ref, lb_ref, on_ref, s0_ref, o_ref, mq_ref, so_ref, z_ref, st_ref = refs
    else:
        zin_ref, lb_ref, on_ref, s0_ref, o_ref, so_ref, st_ref = refs
        z_ref = zin_ref.at[0]
    step = pl.program_id(1)

    @pl.when(step == 0)
    def _():
        for h in range(HGRN_HEADS):
            st_ref[h] = s0_ref[0, h].T

    def project(s):
        if not fused:
            return
        rs = slice(s * sub, (s + 1) * sub)
        hn = _rms(x_ref[0, rs, :], ng_ref[...]).astype(BF16)
        for j in range(z_ref.shape[1] // PROJ_CHUNK):
            cs = slice(j * PROJ_CHUNK, (j + 1) * PROJ_CHUNK)
            z_ref[rs, cs] = jnp.dot(hn, w_ref[:, cs], preferred_element_type=F32)
        mq_ref[0, rs, :] = z_ref[rs, HGRN_MEM0:]

    zcol = lambda part, h: slice(part * TOK_WIDTH + h * HGRN_DK, part * TOK_WIDTH + (h + 1) * HGRN_DK)

    row = lax.broadcasted_iota(I32, (sub, sub), 0)
    col = lax.broadcasted_iota(I32, (sub, sub), 1)
    same_chunk = (row // chunk) == (col // chunk)
    causal = jnp.logical_and(same_chunk, row >= col)
    sum_lhs = jnp.concatenate([jnp.where(causal, 1.0, 0.0), jnp.where(same_chunk, 1.0, 0.0)], axis=0).astype(BF16)
    row_id = lax.broadcasted_iota(I32, (sub, HGRN_DK), 0)
    row_ok = row_id % chunk < valid_rows
    n_c = sub // chunk
    in_chunk = [row_id // chunk == c for c in range(n_c)]

    project(0)
    for s in range(n_sub):
        if s + 1 < n_sub:
            project(s + 1)
        rows = slice(s * sub, (s + 1) * sub)
        for h in range(HGRN_HEADS):
            sl = slice(h * HGRN_DK, (h + 1) * HGRN_DK)
            lb = lb_ref[:, sl]
            q = _silu(z_ref[rows, zcol(0, h)])
            f = lb + (1.0 - lb) * jax.nn.sigmoid(z_ref[rows, zcol(1, h)])
            if valid_rows < chunk:
                f = jnp.where(row_ok, f, 1.0)
            k = 1.0 - f
            vb = z_ref[rows, zcol(2, h)].astype(BF16)
            g_hi, g_lo = _split_bf16(jnp.log(f))
            sums = jnp.dot(sum_lhs, jnp.concatenate([g_hi, g_lo], axis=1), preferred_element_type=F32)
            b = sums[:sub, :HGRN_DK] + sums[:sub, HGRN_DK:]
            b_chunk = sums[sub:, :HGRN_DK] + sums[sub:, HGRN_DK:]
            q_dec = q * jnp.exp(b)
            k_dec = (k * jnp.exp(-b)).astype(BF16)
            k_rem = k * jnp.exp(b_chunk - b)
            chunk_decay = jnp.exp(b_chunk)
            scores = jnp.where(causal, _dot_nt(q_dec.astype(BF16), k_dec), 0.0)
            o = jnp.dot(scores.astype(BF16), vb, preferred_element_type=F32)
            spread = lambda a: jnp.concatenate([jnp.where(in_chunk[c], a, 0.0) for c in range(n_c)],
                                               axis=1).astype(BF16)
            incr = _dot_tn(vb, spread(k_rem))
            st = st_ref[h]
            states = []
            for c in range(n_c):
                states.append(st.astype(BF16))
                st = st * chunk_decay[c * chunk:c * chunk + 1] + incr[:, c * HGRN_DK:(c + 1) * HGRN_DK]
            st_ref[h] = st
            o = o + _dot_nt(spread(q_dec), jnp.concatenate(states, axis=1))
            o_ref[0, rows, sl] = _rms(o, on_ref[...]) * _silu(z_ref[rows, zcol(3, h)])

    @pl.when(step == pl.num_programs(1) - 1)
    def _():
        for h in range(HGRN_HEADS):
            so_ref[0, h] = st_ref[h].T


def _norm_proj_kernel(x_ref, g_ref, w_ref, o_ref):
    h = _rms(x_ref[...], g_ref[...]).astype(BF16)
    o_ref[...] = jnp.dot(h, w_ref[...], preferred_element_type=F32)


def _norm_proj(x2d, gain, w_bf16):
    n, d = x2d.shape
    c = w_bf16.shape[1]
    whole = lambda shape: pl.BlockSpec(shape, lambda i: (0, 0))
    return pl.pallas_call(
        _norm_proj_kernel,
        out_shape=jax.ShapeDtypeStruct((n, c), F32),
        grid=(1,),
        in_specs=[whole((n, d)), whole((1, d)), whole((d, c))],
        out_specs=whole((n, c)),
        compiler_params=_params("arbitrary"),
        name="norm_proj",
    )(x2d, gain.reshape(1, d), w_bf16)


def _hgrn_projected(z3d, lb, out_norm, s0, chunk, sub, t_step, valid_rows):
    bsz, t, _ = z3d.shape
    const = lambda shape: pl.BlockSpec(shape, lambda b, i: (0,) * len(shape))
    rows = lambda w: pl.BlockSpec((1, t_step, w), lambda b, i: (b, i, 0))
    state_spec = pl.BlockSpec((1, HGRN_HEADS, HGRN_DK, HGRN_DK), lambda b, i: (b, 0, 0, 0))
    kern = functools.partial(_hgrn_kernel, chunk=chunk, sub=sub, n_sub=t_step // sub, valid_rows=valid_rows,
                             fused=False)
    return pl.pallas_call(
        kern,
        out_shape=(jax.ShapeDtypeStruct((bsz, t, TOK_WIDTH), F32),
                   jax.ShapeDtypeStruct((bsz, HGRN_HEADS, HGRN_DK, HGRN_DK), F32)),
        grid=(bsz, t // t_step),
        in_specs=[rows(4 * TOK_WIDTH), const((1, TOK_WIDTH)), const((1, HGRN_DK)), state_spec],
        out_specs=(rows(TOK_WIDTH), state_spec),
        scratch_shapes=[pltpu.VMEM((HGRN_HEADS, HGRN_DK, HGRN_DK), F32)],
        compiler_params=_params("parallel", "arbitrary"),
        name="hgrn2_projected",
    )(z3d, lb.reshape(1, TOK_WIDTH), out_norm.reshape(1, HGRN_DK), s0)


def _hgrn(x3d, norm_gain, w_bf16, lb, out_norm, s0, chunk, sub, t_step, valid_rows):
    bsz, t, _ = x3d.shape
    cols = w_bf16.shape[1]
    const = lambda shape, **kw: pl.BlockSpec(shape, lambda b, i: (0,) * len(shape), **kw)
    rows = lambda w: pl.BlockSpec((1, t_step, w), lambda b, i: (b, i, 0))
    state_spec = pl.BlockSpec((1, HGRN_HEADS, HGRN_DK, HGRN_DK), lambda b, i: (b, 0, 0, 0))
    kern = functools.partial(_hgrn_kernel, chunk=chunk, sub=sub, n_sub=t_step // sub, valid_rows=valid_rows,
                             fused=True)
    return pl.pallas_call(
        kern,
        out_shape=(jax.ShapeDtypeStruct((bsz, t, TOK_WIDTH), F32),
                   jax.ShapeDtypeStruct((bsz, t, MEM_WIDTH), F32),
                   jax.ShapeDtypeStruct((bsz, HGRN_HEADS, HGRN_DK, HGRN_DK), F32)),
        grid=(bsz, t // t_step),
        in_specs=[rows(D_MODEL), const((1, D_MODEL)),
                  const((D_MODEL, cols), pipeline_mode=pl.Buffered(1)),
                  const((1, TOK_WIDTH)), const((1, HGRN_DK)), state_spec],
        out_specs=(rows(TOK_WIDTH), rows(MEM_WIDTH), state_spec),
        scratch_shapes=[pltpu.VMEM((t_step, cols), F32),
                        pltpu.VMEM((HGRN_HEADS, HGRN_DK, HGRN_DK), F32)],
        compiler_params=_params("parallel", "arbitrary"),
        name="hgrn2",
    )(x3d, norm_gain.reshape(1, D_MODEL), w_bf16, lb.reshape(1, TOK_WIDTH), out_norm.reshape(1, HGRN_DK), s0)


def _mem_kv_kernel(m_ref, w_ref, g_ref, k_ref, v_ref):
    kv = jnp.dot(m_ref[...].astype(BF16), w_ref[...], preferred_element_type=F32)
    k = kv[:, :MEM_WIDTH]
    k_ref[...] = k * lax.rsqrt(_group_mean_sq(k, MEM_HD) + EPS) * g_ref[...]
    v_ref[...] = kv[:, MEM_WIDTH:]


def _mem_kv(mem2d, w_bf16, k_gain):
    n = mem2d.shape[0]
    out = jax.ShapeDtypeStruct((n, MEM_WIDTH), F32)
    return pl.pallas_call(
        _mem_kv_kernel,
        out_shape=(out, out),
        grid=(n // N_MEM,),
        in_specs=[pl.BlockSpec((N_MEM, D_MODEL), lambda i: (i, 0)),
                  pl.BlockSpec((D_MODEL, 2 * MEM_WIDTH), lambda i: (0, 0)),
                  pl.BlockSpec((1, MEM_WIDTH), lambda i: (0, 0))],
        out_specs=(pl.BlockSpec((N_MEM, MEM_WIDTH), lambda i: (i, 0)),
                   pl.BlockSpec((N_MEM, MEM_WIDTH), lambda i: (i, 0))),
        compiler_params=_params("parallel"),
        name="mem_kv",
    )(mem2d, w_bf16, jnp.tile(k_gain, MEM_HEADS).reshape(1, MEM_WIDTH))


def _mem_attn_out_kernel(zq_ref, g_ref, mk_ref, mv_ref, x_ref, t_ref, wt_ref, wc_ref, o_ref, *, kv_transposed):
    q = zq_ref[0]
    qn = q * lax.rsqrt(_group_mean_sq(q, MEM_HD) + EPS) * g_ref[...]
    mk = mk_ref[0].astype(BF16)
    mv = mv_ref[0].astype(BF16)
    head_of_lane = lax.broadcasted_iota(I32, qn.shape, 1) // MEM_HD
    c = jnp.zeros(qn.shape, F32)
    for h in range(MEM_HEADS):
        mine = head_of_lane == h
        qh = jnp.where(mine, qn, 0.0).astype(BF16)
        s = (jnp.dot(qh, mk, preferred_element_type=F32) if kv_transposed else _dot_nt(qh, mk)) * (MEM_HD ** -0.5)
        p = jnp.exp(s - jnp.max(s, axis=-1, keepdims=True))
        pb = p.astype(BF16)
        oh = _dot_nt(pb, mv) if kv_transposed else jnp.dot(pb, mv, preferred_element_type=F32)
        c = jnp.where(mine, oh / jnp.sum(p, axis=-1, keepdims=True), c)
    o_ref[0] = (x_ref[0]
                + jnp.dot(t_ref[0].astype(BF16), wt_ref[...], preferred_element_type=F32)
                + jnp.dot(c.astype(BF16), wc_ref[...], preferred_element_type=F32))


def _mem_attn_out(zq3d, q_col_block, q_gain, mk, mv, layer, kv_transposed, x3d, t3d, w_out, tm):
    bsz, t, _ = x3d.shape
    rows = lambda w, col=0: pl.BlockSpec((1, tm, w), lambda b, i: (b, i, col))
    per_b = pl.BlockSpec((None, 1, N_MEM, MEM_WIDTH), lambda b, i: (layer, b, 0, 0))
    const = lambda shape: pl.BlockSpec(shape, lambda b, i: (0,) * len(shape))
    return pl.pallas_call(
        functools.partial(_mem_attn_out_kernel, kv_transposed=kv_transposed),
        out_shape=jax.ShapeDtypeStruct((bsz, t, D_MODEL), F32),
        grid=(bsz, t // tm),
        in_specs=[rows(MEM_WIDTH, q_col_block), const((1, MEM_WIDTH)), per_b, per_b,
                  rows(D_MODEL), rows(TOK_WIDTH), const((TOK_WIDTH, D_MODEL)), const((MEM_WIDTH, D_MODEL))],
        out_specs=rows(D_MODEL),
        compiler_params=_params("parallel", "parallel"),
        name="mem_attn_out",
    )(zq3d, jnp.tile(q_gain, MEM_HEADS).reshape(1, MEM_WIDTH), mk, mv, x3d, t3d,
      w_out[:TOK_WIDTH], w_out[TOK_WIDTH:])


def _ffn_kernel(x_ref, g_ref, wg_ref, wu_ref, cw_ref, cb_ref, wd_ref, p1_ref, p2_ref,
                y_ref, u_ref, carry_ref, *, seq_rows, carry):
    tm = x_ref.shape[0]
    x = x_ref[...]
    h = _rms(x, g_ref[...]).astype(BF16)
    t_in_seq = lax.broadcasted_iota(I32, (tm, FF_CHUNK), 0) % seq_rows
    first_block = pl.program_id(1) == 0 if carry else None
    n_chunks = D_FF // FF_CHUNK

    def project(c):
        cs = slice(c * FF_CHUNK, (c + 1) * FF_CHUNK)
        return (jnp.dot(h, wg_ref[:, cs], preferred_element_type=F32),
                jnp.dot(h, wu_ref[:, cs], preferred_element_type=F32))

    acc = x
    ahead = project(0)
    for c in range(n_chunks):
        cs = slice(c * FF_CHUNK, (c + 1) * FF_CHUNK)
        u, up = ahead
        if c + 1 < n_chunks:
            ahead = project(c + 1)
        u1 = pltpu.roll(u, 1, 0)
        u2 = pltpu.roll(u, 2, 0)
        if carry:
            prev = carry_ref[:, cs]
            start1 = jnp.where(first_block, p1_ref[0:1, cs], prev[7:8, :])
            start2a = jnp.where(first_block, p2_ref[0:1, cs], prev[6:7, :])
            start2b = jnp.where(first_block, p2_ref[1:2, cs], prev[7:8, :])
            u1 = jnp.where(t_in_seq == 0, start1, u1)
            u2 = jnp.where(t_in_seq == 0, start2a, jnp.where(t_in_seq == 1, start2b, u2))
            carry_ref[:, cs] = u[tm - SUBLANES:, :]
        else:
            u1 = jnp.where(t_in_seq == 0, p1_ref[:, cs], u1)
            u2 = jnp.where(t_in_seq < 2, p2_ref[:, cs], u2)
        conv = cb_ref[:, cs] + u2 * cw_ref[0:1, cs] + u1 * cw_ref[1:2, cs] + u * cw_ref[2:3, cs]
        a = (_silu(conv) * up).astype(BF16)
        acc = acc + jnp.dot(a, wd_ref[cs, :], preferred_element_type=F32)
        if carry:
            u_ref[0, :, cs] = u[tm - SUBLANES:, :]
        else:
            u_ref[:, cs] = u
    y_ref[...] = acc


def _ffn_weights_specs(idx):
    once = pl.Buffered(1)
    return [pl.BlockSpec((1, D_MODEL), idx),
            pl.BlockSpec((D_MODEL, D_FF), idx, pipeline_mode=once),
            pl.BlockSpec((D_MODEL, D_FF), idx, pipeline_mode=once),
            pl.BlockSpec((CONV_W, D_FF), idx),
            pl.BlockSpec((1, D_FF), idx),
            pl.BlockSpec((D_FF, D_MODEL), idx, pipeline_mode=once)]


def _ffn_long(x3d, gain, wg, wu, cw, cb, wd, state, tm):
    bsz, t, _ = x3d.shape
    idx = lambda b, i: (0, 0)
    kern = functools.partial(_ffn_kernel, seq_rows=t, carry=True)
    nb = t // tm
    row = lambda w: pl.BlockSpec((tm, w), lambda b, i: (b * nb + i, 0))
    p1 = state[:, 1:2, :]
    y, u_last = pl.pallas_call(
        kern,
        out_shape=(jax.ShapeDtypeStruct((bsz * t, D_MODEL), F32),
                   jax.ShapeDtypeStruct((bsz, SUBLANES, D_FF), F32)),
        grid=(bsz, nb),
        in_specs=[row(D_MODEL)] + _ffn_weights_specs(idx)
                 + [pl.BlockSpec((None, 1, D_FF), lambda b, i: (b, 0, 0)),
                    pl.BlockSpec((None, 2, D_FF), lambda b, i: (b, 0, 0))],
        out_specs=(row(D_MODEL), pl.BlockSpec((1, SUBLANES, D_FF), lambda b, i: (b, 0, 0))),
        scratch_shapes=[pltpu.VMEM((SUBLANES, D_FF), F32)],
        compiler_params=_params("parallel", "arbitrary"),
        name="conv_ffn_long",
    )(x3d.reshape(bsz * t, D_MODEL), gain.reshape(1, D_MODEL), wg, wu, cw, cb.reshape(1, D_FF), wd, p1, state)
    return y.reshape(bsz, t, D_MODEL), u_last


def _ffn_short(x2d, gain, wg, wu, cw, cb, wd, p1, p2, seq_rows):
    n = x2d.shape[0]
    idx = lambda i: (0, 0)
    kern = functools.partial(_ffn_kernel, seq_rows=seq_rows, carry=False)
    row = lambda w: pl.BlockSpec((n, w), lambda i: (0, 0))
    return pl.pallas_call(
        kern,
        out_shape=(jax.ShapeDtypeStruct((n, D_MODEL), F32), jax.ShapeDtypeStruct((n, D_FF), F32)),
        grid=(1,),
        in_specs=[row(D_MODEL)] + _ffn_weights_specs(idx) + [row(D_FF), row(D_FF)],
        out_specs=(row(D_MODEL), row(D_FF)),
        scratch_shapes=[pltpu.VMEM((SUBLANES, D_FF), F32)],
        compiler_params=_params("arbitrary"),
        name="conv_ffn_short",
    )(x2d, gain.reshape(1, D_MODEL), wg, wu, cw, cb.reshape(1, D_FF), wd, p1, p2)


def _rope128(x, cos, sin_signed):
    return x * cos + pltpu.roll(x, HEAD_DIM // 2, 1) * sin_signed


def _rope64_pairs(x, cos, sin_lo, sin_hi):
    return (x * cos + pltpu.roll(x, LANES - IDX_DIM // 2, 1) * sin_lo
            + pltpu.roll(x, IDX_DIM // 2, 1) * sin_hi)


def _dsa_prep_kernel(x_ref, ng_ref, w_ref, c128_ref, s128_ref, c64_ref, s64lo_ref, s64hi_ref, qg_ref, kg_ref, ikg_ref,
                     q_ref, k_ref, kb_ref, v_ref, vt_ref, iq_ref, ik_ref, ikb_ref, iwt_ref, mq_ref, z_ref):
    tm = x_ref.shape[1]
    part_rows = min(tm, PREP_PART)
    n_parts = tm // part_rows

    def project(p):
        rs = slice(p * part_rows, (p + 1) * part_rows)
        hn = _rms(x_ref[0, rs, :], ng_ref[...]).astype(BF16)
        for j in range(DSA_COLS_PAD // PROJ_CHUNK):
            cs = slice(j * PROJ_CHUNK, (j + 1) * PROJ_CHUNK)
            z_ref[0, rs, cs] = jnp.dot(hn, w_ref[:, cs], preferred_element_type=F32)

    def finish(p):
        rs = slice(p * part_rows, (p + 1) * part_rows)
        mq_ref[0, rs, :] = z_ref[0, rs, DSA_MEM0:]
        c128, s128 = c128_ref[rs, :], s128_ref[rs, :]
        c64, s64lo, s64hi = c64_ref[rs, :], s64lo_ref[rs, :], s64hi_ref[rs, :]
        for h in range(N_Q_HEADS):
            sl = slice(DSA_Q0 + h * HEAD_DIM, DSA_Q0 + (h + 1) * HEAD_DIM)
            q = _rope128(_rms(z_ref[0, rs, sl], qg_ref[...]), c128, s128)
            q_ref[0, rs, h * HEAD_DIM:(h + 1) * HEAD_DIM] = (q * QK_SCALE_LOG2E).astype(BF16)
        for h in range(N_KV_HEADS):
            sl = slice(DSA_K0 + h * HEAD_DIM, DSA_K0 + (h + 1) * HEAD_DIM)
            k = _rope128(_rms(z_ref[0, rs, sl], kg_ref[...]), c128, s128)
            k_ref[0, rs, h * HEAD_DIM:(h + 1) * HEAD_DIM] = k
            kb_ref[0, rs, h * HEAD_DIM:(h + 1) * HEAD_DIM] = k.astype(BF16)
        v = z_ref[0, rs, DSA_V0:DSA_V0 + N_KV_HEADS * HEAD_DIM]
        v_ref[0, rs, :] = v
        vt_ref[0, 0, :, rs] = v.T.astype(BF16)
        for pr in range(IDX_HEADS // 2):
            sl = slice(DSA_IQ0 + pr * LANES, DSA_IQ0 + (pr + 1) * LANES)
            iq_ref[0, rs, pr * LANES:(pr + 1) * LANES] = _rope64_pairs(z_ref[0, rs, sl], c64, s64lo, s64hi)
        ik = _rope64_pairs(_rms(z_ref[0, rs, DSA_IK0:DSA_IK0 + LANES], ikg_ref[...]), c64, s64lo, s64hi)
        ik_ref[0, rs, :] = ik
        ik_hi, ik_lo = _split_bf16(ik)
        hi_lo = jnp.where(lax.broadcasted_iota(I32, ik.shape, 1) < IDX_DIM, ik_hi, ik_lo)
        ikb_ref[0, rs, :LANES] = hi_lo
        ikb_ref[0, rs, LANES:] = hi_lo
        iw = z_ref[0, rs, DSA_IW0:DSA_IW0 + LANES] * (IDX_HEADS ** -0.5 * IDX_DIM ** -0.5)
        iwt_ref[0, :, rs] = iw.T[:IDX_HEADS, :]

    project(0)
    for p in range(n_parts):
        if p + 1 < n_parts:
            project(p + 1)
        finish(p)


def _dsa_prep(x3d, norm_gain, w_bf16, tabs, q_gain, k_gain, ik_gain, tm):
    bsz, t, _ = x3d.shape
    nb = t // tm
    tab = pl.BlockSpec((tm, LANES), lambda b, i: (i, 0))
    gain = pl.BlockSpec((1, LANES), lambda b, i: (0, 0))
    rows = lambda w: pl.BlockSpec((1, tm, w), lambda b, i: (b, i, 0))
    const = lambda shape, **kw: pl.BlockSpec(shape, lambda b, i: (0,) * len(shape), **kw)
    kvw = N_KV_HEADS * HEAD_DIM
    sds = jax.ShapeDtypeStruct
    return pl.pallas_call(
        _dsa_prep_kernel,
        out_shape=(sds((bsz, t, TOK_WIDTH), BF16),
                   sds((bsz, t, kvw), F32), sds((bsz, t, kvw), BF16),
                   sds((bsz, t, kvw), F32), sds((bsz, nb, kvw, tm), BF16),
                   sds((bsz, t, IDX_HEADS * IDX_DIM), F32),
                   sds((bsz, t, LANES), F32),
                   sds((bsz, t, 2 * LANES), BF16),
                   sds((bsz, IDX_HEADS, t), F32),
                   sds((bsz, t, MEM_WIDTH), F32)),
        grid=(bsz, nb),
        in_specs=[rows(D_MODEL), const((1, D_MODEL)),
                  const((D_MODEL, DSA_COLS_PAD), pipeline_mode=pl.Buffered(1)),
                  tab, tab, tab, tab, tab, gain, gain, gain],
        out_specs=(rows(TOK_WIDTH), rows(kvw), rows(kvw), rows(kvw),
                   pl.BlockSpec((1, 1, kvw, tm), lambda b, i: (b, i, 0, 0)),
                   rows(IDX_HEADS * IDX_DIM), rows(LANES), rows(2 * LANES),
                   pl.BlockSpec((1, IDX_HEADS, tm), lambda b, i: (b, 0, i)),
                   rows(MEM_WIDTH)),
        scratch_shapes=[pltpu.VMEM((1, tm, DSA_COLS_PAD), F32)],
        compiler_params=_params("parallel", "parallel"),
        name="dsa_prep",
    )(x3d, norm_gain.reshape(1, D_MODEL), w_bf16, *tabs,
      q_gain.reshape(1, LANES), k_gain.reshape(1, LANES), jnp.tile(ik_gain, 2).reshape(1, LANES))


def _rope_tables(pos):
    pos = pos.astype(F32)[:, None]

    def cs(half):
        inv = ROPE_THETA ** (-jnp.arange(half, dtype=F32) / half)
        ang = pos * inv[None, :]
        return jnp.cos(ang), jnp.sin(ang)

    c, s = cs(HEAD_DIM // 2)
    c128 = jnp.concatenate([c, c], axis=1)
    s128 = jnp.concatenate([-s, s], axis=1)
    c, s = cs(IDX_DIM // 2)
    z = jnp.zeros_like(s)
    c64 = jnp.concatenate([c, c, c, c], axis=1)
    s64lo = jnp.concatenate([-s, z, -s, z], axis=1)
    s64hi = jnp.concatenate([z, s, z, s], axis=1)
    return c128, s128, c64, s64lo, s64hi


def _ordered_bits_to_float(u):
    key = u ^ INT_MIN
    bits = key ^ ((key >> 31) & 0x7FFFFFFF)
    return lax.bitcast_convert_type(bits, F32)


def _topk_threshold(count_ge, count_gt, count_eq_before, any_true, shape, idx_bits):
    def value_step(i, prefix):
        cand = prefix | jnp.left_shift(jnp.int32(1), 31 - i)
        below_neg_inf = jnp.logical_and(cand >= 0, cand < 0x00800000)
        ok = jnp.logical_or(count_ge(_ordered_bits_to_float(cand)) >= TOPK, below_neg_inf)
        return jnp.where(ok, cand, prefix)

    thr = _ordered_bits_to_float(lax.fori_loop(0, 32, value_step, jnp.zeros(shape, I32)))
    n_gt = count_gt(thr)
    need = TOPK - n_gt
    n_eq = count_ge(thr) - n_gt
    finite_thr = thr > -jnp.inf
    excess = jnp.logical_and(n_eq > need, finite_thr)

    def index_search():
        def index_step(i, cut):
            cand = cut | jnp.left_shift(jnp.int32(1), idx_bits - 1 - i)
            return jnp.where(count_eq_before(thr, cand) < need, cand, cut)
        return lax.fori_loop(0, idx_bits, index_step, jnp.zeros(shape, I32))

    everything = jnp.full(shape, 2 ** idx_bits - 1, I32)
    cut = lax.cond(any_true(excess), index_search, lambda: everything)
    return thr, jnp.where(finite_thr, cut, -1)


def _dsa_prompt_kernel(q_ref, iq_ref, iwt_ref, ik_ref, k_ref, vt_ref, prev_ref, o_ref,
                       score_ref, iqm_ref, qm_ref, acc_ref, *, n_kb):
    del prev_ref
    q0 = ((n_kb - 1) * (KEY_BLOCK // Q_TILE) + pl.program_id(1)) * Q_TILE
    lane = lax.broadcasted_iota(I32, (Q_TILE, LANES), 1)
    first_half = lane < IDX_DIM
    for p in range(IDX_HEADS // 2):
        pair = iq_ref[0, :, p * LANES:(p + 1) * LANES]
        swapped = pltpu.roll(pair, IDX_DIM, 1)
        for half in range(2):
            dup = jnp.where(first_half, pair, swapped) if half == 0 else jnp.where(first_half, swapped, pair)
            hi, lo = _split_bf16(dup)
            rows = slice(half * Q_TILE, (half + 1) * Q_TILE)
            iqm_ref[p, rows, :LANES] = hi
            iqm_ref[p, rows, LANES:] = lo
    for h in range(N_Q_HEADS):
        g = h // Q_PER_KV
        qh = q_ref[0, :, h * HEAD_DIM:(h + 1) * HEAD_DIM]
        rows = slice(h * Q_TILE, (h + 1) * Q_TILE)
        qm_ref[rows, g * HEAD_DIM:(g + 1) * HEAD_DIM] = qh
        qm_ref[rows, (1 - g) * HEAD_DIM:(2 - g) * HEAD_DIM] = jnp.zeros_like(qh)
    q_pos = q0 + lax.broadcasted_iota(I32, (1, Q_TILE), 1)
    key_row = lax.broadcasted_iota(I32, (KEY_BLOCK, 1), 0)

    blocks = [slice(kb * KEY_BLOCK, (kb + 1) * KEY_BLOCK) for kb in range(n_kb)]

    for kb in range(n_kb):
        ik = ik_ref[0, blocks[kb], :]
        acc = jnp.zeros((KEY_BLOCK, Q_TILE), F32)
        for p in range(IDX_HEADS // 2):
            d = _dot_nt(ik, iqm_ref[p])
            for half in range(2):
                h = 2 * p + half
                acc = acc + iwt_ref[0, h:h + 1, :] * jnp.maximum(d[:, half * Q_TILE:(half + 1) * Q_TILE], 0.0)
        if kb == n_kb - 1:
            acc = jnp.where(kb * KEY_BLOCK + key_row <= q_pos, acc, -jnp.inf)
        score_ref[blocks[kb], :] = acc

    def count(pred):
        part = jnp.zeros((SUBLANES, Q_TILE), I32)
        for kb in range(n_kb):
            hit = pred(score_ref[blocks[kb], :], kb * KEY_BLOCK + key_row).astype(I32)
            part = part + jnp.sum(hit.reshape(KEY_BLOCK // SUBLANES, SUBLANES, Q_TILE), axis=0)
        return jnp.sum(part, axis=0, keepdims=True)

    thr, cut = _topk_threshold(
        count_ge=lambda c: count(lambda s, idx: s >= c),
        count_gt=lambda c: count(lambda s, idx: s > c),
        count_eq_before=lambda t, j: count(lambda s, idx: jnp.logical_and(s == t, idx < j)),
        any_true=lambda m: jnp.max(m.astype(I32)) > 0,
        shape=(1, Q_TILE), idx_bits=13)

    m = [jnp.full((1, Q_TILE), NEG_BIG, F32)] * N_Q_HEADS
    l = [jnp.zeros((1, Q_TILE), F32)] * N_Q_HEADS
    sub_row = lax.broadcasted_iota(I32, (ATTN_SUB, 1), 0)
    for ab in range(n_kb * KEY_BLOCK // ATTN_SUB):
        kb, off = divmod(ab * ATTN_SUB, KEY_BLOCK)
        keys = slice(ab * ATTN_SUB, (ab + 1) * ATTN_SUB)
        sc = score_ref[keys, :]
        sel = jnp.logical_or(sc > thr, jnp.logical_and(sc == thr, ab * ATTN_SUB + sub_row <= cut))
        k_blk = k_ref[0, keys, :]
        alphas, ps = [], []
        for j in range(N_Q_HEADS // 2):
            s2 = _dot_nt(k_blk, qm_ref[2 * j * Q_TILE:2 * (j + 1) * Q_TILE, :])
            for half in range(2):
                h = 2 * j + half
                s = jnp.where(sel, s2[:, half * Q_TILE:(half + 1) * Q_TILE], NEG_BIG)
                m_new = jnp.maximum(m[h], jnp.max(s, axis=0, keepdims=True))
                alpha = jnp.exp2(m[h] - m_new)
                p = jnp.exp2(s - m_new)
                l[h] = alpha * l[h] + jnp.sum(p, axis=0, keepdims=True)
                m[h] = m_new
                alphas.append(alpha)
                ps.append(p.astype(BF16))
        for g in range(N_KV_HEADS):
            vtg = vt_ref[0, kb, g * HEAD_DIM:(g + 1) * HEAD_DIM, off:off + ATTN_SUB]
            pv = jnp.dot(vtg, jnp.concatenate(ps[g * Q_PER_KV:(g + 1) * Q_PER_KV], axis=1),
                         preferred_element_type=F32)
            for r in range(Q_PER_KV):
                h = g * Q_PER_KV + r
                part = pv[:, r * Q_TILE:(r + 1) * Q_TILE]
                acc_ref[h] = part if ab == 0 else alphas[h] * acc_ref[h] + part
    for h in range(N_Q_HEADS):
        o_ref[0, :, h * HEAD_DIM:(h + 1) * HEAD_DIM] = (acc_ref[h] / l[h]).T


def _dsa_prompt(q, iq, iwt, ikb, kb, vt):
    bsz, t, _ = q.shape
    kvw = N_KV_HEADS * HEAD_DIM
    tiles = KEY_BLOCK // Q_TILE
    out = jnp.zeros((bsz, t, TOK_WIDTH), F32)
    for n_kb in range(1, t // KEY_BLOCK + 1):
        tile = lambda b, i, n_kb=n_kb: (n_kb - 1) * tiles + i
        keys = lambda w, n_kb=n_kb: pl.BlockSpec((1, n_kb * KEY_BLOCK, w), lambda b, i: (b, 0, 0))
        out = pl.pallas_call(
            functools.partial(_dsa_prompt_kernel, n_kb=n_kb),
            out_shape=jax.ShapeDtypeStruct((bsz, t, TOK_WIDTH), F32),
            grid=(bsz, tiles),
            in_specs=[pl.BlockSpec((1, Q_TILE, TOK_WIDTH), lambda b, i, tile=tile: (b, tile(b, i), 0)),
                      pl.BlockSpec((1, Q_TILE, IDX_HEADS * IDX_DIM), lambda b, i, tile=tile: (b, tile(b, i), 0)),
                      pl.BlockSpec((1, IDX_HEADS, Q_TILE), lambda b, i, tile=tile: (b, 0, tile(b, i))),
                      keys(2 * LANES), keys(kvw),
                      pl.BlockSpec((1, n_kb, kvw, KEY_BLOCK), lambda b, i: (b, 0, 0, 0)),
                      pl.BlockSpec(memory_space=pl.ANY)],
            out_specs=pl.BlockSpec((1, Q_TILE, TOK_WIDTH), lambda b, i, tile=tile: (b, tile(b, i), 0)),
            scratch_shapes=[pltpu.VMEM((n_kb * KEY_BLOCK, Q_TILE), F32),
                            pltpu.VMEM((IDX_HEADS // 2, 2 * Q_TILE, 2 * LANES), BF16),
                            pltpu.VMEM((N_Q_HEADS * Q_TILE, N_KV_HEADS * HEAD_DIM), BF16),
                            pltpu.VMEM((N_Q_HEADS, HEAD_DIM, Q_TILE), F32)],
            input_output_aliases={6: 0},
            compiler_params=_params("parallel", "arbitrary"),
            name=f"dsa_prompt_{n_kb}",
        )(q, iq, iwt, ikb, kb, vt, out)
    return out


SAMPLE_ROWS = 16
SAMPLE_T = 4
N_PAGES = PAST_LEN // PAGE_SIZE
N_KEY_PAGES = N_PAGES + 1


def _sample_score_kernel(pt_ref, iq4_ref, iw_ref, ikn_ref, *rest):
    page_refs = rest[:SCORE_PAGES_PER_STEP]
    (score_ref,) = rest[SCORE_PAGES_PER_STEP:]
    step = pl.program_id(1)
    iq_hi, iq_lo = _split_bf16(iq4_ref[0])
    iq_parts = jnp.where(lax.broadcasted_iota(I32, iq_hi.shape, 1) < 2 * IDX_DIM, iq_hi, iq_lo)
    w = iw_ref[0]

    def page_scores(ik_t, page, allowed):
        k_hi, k_lo = _split_bf16(ik_t)
        d = jnp.dot(iq_parts, jnp.concatenate([k_hi, k_lo, k_hi, k_lo], axis=0), preferred_element_type=F32)
        sc = jnp.sum((jnp.maximum(d, 0.0) * w).reshape(SAMPLE_T, IDX_HEADS, PAGE_SIZE), axis=1)
        if allowed is not None:
            sc = jnp.where(allowed, sc, -jnp.inf)
        for r in range(SAMPLE_ROWS // SAMPLE_T):
            score_ref[0, page, r * SAMPLE_T:(r + 1) * SAMPLE_T, :] = sc

    for i in range(SCORE_PAGES_PER_STEP):
        page_scores(page_refs[i][0], step * SCORE_PAGES_PER_STEP + i, None)

    @pl.when(step == pl.num_programs(1) - 1)
    def _():
        new_idx = lax.broadcasted_iota(I32, (SAMPLE_T, PAGE_SIZE), 1)
        page_scores(ikn_ref[0], N_PAGES, new_idx <= lax.broadcasted_iota(I32, (SAMPLE_T, PAGE_SIZE), 0))


def _sample_scores(page_table, iq4, iw_col, ik_new_t, pool_ik_t):
    bsz = iq4.shape[0]
    page_spec = lambda i: pl.BlockSpec((1, IDX_DIM, PAGE_SIZE),
                                       lambda b, s, pt, i=i: (pt[b, s * SCORE_PAGES_PER_STEP + i], 0, 0))
    per_b = lambda shape: pl.BlockSpec((1,) + shape, lambda b, s, pt: (b,) + (0,) * len(shape))
    grid_spec = pltpu.PrefetchScalarGridSpec(
        num_scalar_prefetch=1,
        grid=(bsz, N_PAGES // SCORE_PAGES_PER_STEP),
        in_specs=[per_b((SAMPLE_T * IDX_HEADS, 4 * IDX_DIM)), per_b((SAMPLE_T * IDX_HEADS, 1)),
                  per_b((IDX_DIM, PAGE_SIZE))] + [page_spec(i) for i in range(SCORE_PAGES_PER_STEP)],
        out_specs=per_b((N_KEY_PAGES, SAMPLE_ROWS, PAGE_SIZE)))
    return pl.pallas_call(
        _sample_score_kernel,
        out_shape=jax.ShapeDtypeStruct((bsz, N_KEY_PAGES, SAMPLE_ROWS, PAGE_SIZE), F32),
        grid_spec=grid_spec,
        compiler_params=_params("parallel", "arbitrary"),
        name="dsa_sample_scores",
    )(page_table, iq4, iw_col, ik_new_t, *([pool_ik_t] * SCORE_PAGES_PER_STEP))


def _sample_select_kernel(score_ref, thr_ref, cut_ref):
    n_q, n_k = score_ref.shape
    key_idx = lax.broadcasted_iota(I32, (1, n_k), 1)

    def count(pred):
        return jnp.sum(pred(score_ref[...], key_idx).astype(I32), axis=-1, keepdims=True)

    thr, cut = _topk_threshold(
        count_ge=lambda c: count(lambda s, idx: s >= c),
        count_gt=lambda c: count(lambda s, idx: s > c),
        count_eq_before=lambda t, j: count(lambda s, idx: jnp.logical_and(s == t, idx < j)),
        any_true=lambda m: jnp.max(m.astype(I32)) > 0,
        shape=(n_q, 1), idx_bits=14)
    thr_ref[...] = jnp.broadcast_to(thr, thr_ref.shape)
    cut_ref[...] = jnp.broadcast_to(cut, cut_ref.shape)


def _sample_select(scores2d):
    n_q, n_k = scores2d.shape
    whole = lambda shape: pl.BlockSpec(shape, lambda i: (0, 0))
    return pl.pallas_call(
        _sample_select_kernel,
        out_shape=(jax.ShapeDtypeStruct((n_q, LANES), F32), jax.ShapeDtypeStruct((n_q, LANES), I32)),
        grid=(1,),
        in_specs=[whole((n_q, n_k))],
        out_specs=(whole((n_q, LANES)), whole((n_q, LANES))),
        compiler_params=_params("arbitrary"),
        name="dsa_sample_select",
    )(scores2d)


def _sample_attn_kernel(pt_ref, q_ref, sc_ref, thr_ref, cut_ref, kn_ref, vn_ref, *rest):
    k_pages = rest[:ATTN_PAGES_PER_STEP]
    v_pages = rest[ATTN_PAGES_PER_STEP:2 * ATTN_PAGES_PER_STEP]
    o_ref, m_ref, l_ref, acc_ref = rest[2 * ATTN_PAGES_PER_STEP:]
    step = pl.program_id(1)

    @pl.when(step == 0)
    def _():
        m_ref[...] = jnp.full(m_ref.shape, NEG_BIG, F32)
        l_ref[...] = jnp.zeros(l_ref.shape, F32)
        acc_ref[...] = jnp.zeros(acc_ref.shape, F32)

    thr = thr_ref[0]
    cut = cut_ref[0]
    lane = lax.broadcasted_iota(I32, (SAMPLE_ROWS, PAGE_SIZE), 1)

    def attend(pages):
        sels = []
        for page, _, _ in pages:
            sc = sc_ref[0, page]
            sels.append(jnp.logical_or(sc > thr, jnp.logical_and(sc == thr, page * PAGE_SIZE + lane <= cut)))
        cat = lambda parts, axis: parts[0] if len(parts) == 1 else jnp.concatenate(parts, axis=axis)
        sel = cat(sels, 1)
        for g in range(N_KV_HEADS):
            rows_g = pl.ds(g, PAGE_SIZE, stride=N_KV_HEADS)
            k_all = cat([k[0, rows_g, :].astype(BF16) for _, k, _ in pages], 0)
            v_all = cat([v[0, rows_g, :].astype(BF16) for _, _, v in pages], 0)
            s = jnp.where(sel, _dot_nt(q_ref[0, g], k_all), NEG_BIG)
            m_old = m_ref[g]
            m_new = jnp.maximum(m_old, jnp.max(s, axis=-1, keepdims=True))
            alpha = jnp.exp2(m_old - m_new)
            p = jnp.exp2(s - m_new)
            l_ref[g] = alpha * l_ref[g] + jnp.sum(p, axis=-1, keepdims=True)
            acc_ref[g] = alpha * acc_ref[g] + jnp.dot(p.astype(BF16), v_all, preferred_element_type=F32)
            m_ref[g] = m_new

    attend([(step * ATTN_PAGES_PER_STEP + i, k_pages[i], v_pages[i]) for i in range(ATTN_PAGES_PER_STEP)])

    @pl.when(step == pl.num_programs(1) - 1)
    def _():
        attend([(N_PAGES, kn_ref, vn_ref)])
        for g in range(N_KV_HEADS):
            o_ref[0, g] = acc_ref[g] / l_ref[g]


def _sample_attn(page_table, q16, scores, thr, cut, k_new_pad, v_new_pad, pool_k, pool_v):
    bsz = q16.shape[0]
    page_rows = PAGE_SIZE * N_KV_HEADS
    page_spec = lambda i: pl.BlockSpec((1, page_rows, HEAD_DIM),
                                       lambda b, s, pt, i=i: (pt[b, s * ATTN_PAGES_PER_STEP + i], 0, 0))
    per_b = lambda shape: pl.BlockSpec((1,) + shape, lambda b, s, pt: (b,) + (0,) * len(shape))
    pages = [page_spec(i) for i in range(ATTN_PAGES_PER_STEP)]
    grid_spec = pltpu.PrefetchScalarGridSpec(
        num_scalar_prefetch=1,
        grid=(bsz, N_PAGES // ATTN_PAGES_PER_STEP),
        in_specs=[per_b((N_KV_HEADS, SAMPLE_ROWS, HEAD_DIM)), per_b((N_KEY_PAGES, SAMPLE_ROWS, PAGE_SIZE)),
                  per_b((SAMPLE_ROWS, LANES)), per_b((SAMPLE_ROWS, LANES)),
                  per_b((page_rows, HEAD_DIM)), per_b((page_rows, HEAD_DIM))] + pages + pages,
        out_specs=per_b((N_KV_HEADS, SAMPLE_ROWS, HEAD_DIM)),
        scratch_shapes=[pltpu.VMEM((N_KV_HEADS, SAMPLE_ROWS, 1), F32),
                        pltpu.VMEM((N_KV_HEADS, SAMPLE_ROWS, 1), F32),
                        pltpu.VMEM((N_KV_HEADS, SAMPLE_ROWS, HEAD_DIM), F32)])
    return pl.pallas_call(
        _sample_attn_kernel,
        out_shape=jax.ShapeDtypeStruct((bsz, N_KV_HEADS, SAMPLE_ROWS, HEAD_DIM), F32),
        grid_spec=grid_spec,
        compiler_params=_params("parallel", "arbitrary"),
        name="dsa_sample_attn",
    )(page_table, q16, scores, thr, cut, k_new_pad, v_new_pad,
      *([pool_k] * ATTN_PAGES_PER_STEP), *([pool_v] * ATTN_PAGES_PER_STEP))


def _pad_cols(w, width):
    return jnp.pad(w, ((0, 0), (0, width - w.shape[1])))


def _dsa_weight(w):
    sizes = (TOK_WIDTH, N_KV_HEADS * HEAD_DIM, N_KV_HEADS * HEAD_DIM, IDX_HEADS * IDX_DIM, IDX_DIM, IDX_HEADS, MEM_WIDTH)
    offs = [0]
    for s in sizes:
        offs.append(offs[-1] + s)
    q, k, v, iq, ik, iw, mq = [w[:, offs[i]:offs[i + 1]] for i in range(len(sizes))]
    return jnp.concatenate([q, k, v, iq, ik, ik, _pad_cols(iw, LANES), mq], axis=1).astype(BF16)


def kernel(x_prompt, x_sample, cache_k, cache_v, cache_idx_k, cache_mem_k, cache_mem_v, state_hgrn, state_ffn_conv, page_table, mem_prompt, norm_mix, norm_ffn, w_in_hgrn, hgrn_lb_logits, hgrn_out_norm, w_in_dsa, dsa_q_norm, dsa_k_norm, idx_k_norm, w_mem_kv, mem_q_norm, mem_k_norm, w_out, w_ffn_gate, w_ffn_up, ffn_conv_w, ffn_conv_b, w_ffn_down):
    bp, seq, _ = x_prompt.shape
    bs, t_s, _ = x_sample.shape
    assert t_s == SAMPLE_T and seq % KEY_BLOCK == 0
    n_s = bs * SAMPLE_T_PAD
    lower_bounds = jnp.cumsum(jax.nn.softmax(hgrn_lb_logits.astype(F32), axis=0), axis=0)

    xp = x_prompt
    xs = jnp.pad(x_sample, ((0, 0), (0, SAMPLE_T_PAD - t_s), (0, 0)))
    mem2d = mem_prompt.reshape(bp * N_MEM, D_MODEL)
    tabs_p = _rope_tables(jnp.arange(seq))
    pos_s = PAST_LEN + jnp.arange(SAMPLE_T_PAD)
    tabs_s = tuple(jnp.tile(tb, (bs, 1)) for tb in _rope_tables(pos_s))
    row_t = jnp.arange(n_s) % SAMPLE_T_PAD

    hgrn_p, hgrn_s = [], []
    kp_l, vp_l, ikp_l, ks_l, vs_l, iks_l = [], [], [], [], [], []
    mk_l, mv_l, cvp_l, cvs_l = [], [], [], []
    for layer in range(DEPTH):
        j = layer // 2
        if layer % 2 == 0:
            mq_col = 0
            w_in = w_in_hgrn[j].astype(BF16)
            s0 = jnp.zeros((bp, HGRN_HEADS, HGRN_DK, HGRN_DK), F32)
            tp, mqp, sp = _hgrn(xp, norm_mix[layer], w_in, lower_bounds[j], hgrn_out_norm[j], s0,
                                HGRN_CHUNK, HGRN_SUB, 1024, HGRN_CHUNK)
            zs = _norm_proj(xs.reshape(n_s, D_MODEL), norm_mix[layer], w_in).reshape(bs, SAMPLE_T_PAD, -1)
            ts, ss = _hgrn_projected(zs, lower_bounds[j], hgrn_out_norm[j], state_hgrn[j], SAMPLE_T_PAD,
                                     SAMPLE_T_PAD, SAMPLE_T_PAD, math.gcd(t_s, HGRN_CHUNK))
            mqs = zs[..., HGRN_MEM0:]
            hgrn_p.append(sp.astype(state_hgrn.dtype))
            hgrn_s.append(ss.astype(state_hgrn.dtype))
        else:
            mq_col = 0
            prep_args = (norm_mix[layer], _dsa_weight(w_in_dsa[j]))
            gains = (dsa_q_norm[j], dsa_k_norm[j], idx_k_norm[j])
            qp, kp, kpb, vp, vtp, iqp, ikp, ikpb, iwtp, mqp = _dsa_prep(xp, *prep_args, tabs_p, *gains, KEY_BLOCK)
            tp = _dsa_prompt(qp, iqp, iwtp, ikpb, kpb, vtp)
            kp_l.append(kp.reshape(bp, seq, N_KV_HEADS, HEAD_DIM))
            vp_l.append(vp.reshape(bp, seq, N_KV_HEADS, HEAD_DIM))
            ikp_l.append(ikp[..., :IDX_DIM])

            qs, kn, _, vn, _, iqs, ikn, _, iwts, mqs = _dsa_prep(
                xs.reshape(1, n_s, D_MODEL), *prep_args, tabs_s, *gains, n_s)
            mqs = mqs.reshape(bs, SAMPLE_T_PAD, MEM_WIDTH)
            sel_t = lambda a: a.reshape(bs, SAMPLE_T_PAD, -1)[:, :t_s]
            kn, vn, ikn = sel_t(kn), sel_t(vn), sel_t(ikn)[..., :IDX_DIM]
            ks_l.append(kn.reshape(bs, t_s, N_KV_HEADS, HEAD_DIM))
            vs_l.append(vn.reshape(bs, t_s, N_KV_HEADS, HEAD_DIM))
            iks_l.append(ikn)
            iq4 = jnp.tile(sel_t(iqs).reshape(bs, t_s * IDX_HEADS, IDX_DIM), (1, 1, 4))
            iw_col = sel_t(iwts[0].T).reshape(bs, t_s * IDX_HEADS, 1)
            ikn_t = jnp.pad(ikn.transpose(0, 2, 1), ((0, 0), (0, 0), (0, PAGE_SIZE - t_s)))
            scores = _sample_scores(page_table, iq4, iw_col, ikn_t, cache_idx_k[j].transpose(0, 2, 1))
            per_query = scores[:, :, :t_s, :].transpose(0, 2, 1, 3).reshape(bs * t_s, N_KEY_PAGES * PAGE_SIZE)
            thr, cut = _sample_select(per_query)
            rep = lambda a: jnp.tile(a.reshape(bs, t_s, LANES), (1, SAMPLE_ROWS // t_s, 1))
            thr, cut = rep(thr), rep(cut)
            q16 = sel_t(qs).reshape(bs, t_s, N_KV_HEADS, Q_PER_KV, HEAD_DIM).transpose(0, 2, 3, 1, 4)
            q16 = jnp.pad(q16.reshape(bs, N_KV_HEADS, Q_PER_KV * t_s, HEAD_DIM),
                          ((0, 0), (0, 0), (0, SAMPLE_ROWS - Q_PER_KV * t_s), (0, 0)))
            page_rows = lambda a: a.reshape(a.shape[0], -1, HEAD_DIM)
            pad_page = lambda a: jnp.pad(page_rows(a), ((0, 0), (0, (PAGE_SIZE - t_s) * N_KV_HEADS), (0, 0)))
            o16 = _sample_attn(page_table, q16, scores, thr, cut, pad_page(kn), pad_page(vn),
                               page_rows(cache_k[j]), page_rows(cache_v[j]))
            o = o16[:, :, :Q_PER_KV * t_s].reshape(bs, N_KV_HEADS, Q_PER_KV, t_s, HEAD_DIM)
            ts = jnp.pad(o.transpose(0, 3, 1, 2, 4).reshape(bs, t_s, TOK_WIDTH),
                         ((0, 0), (0, SAMPLE_T_PAD - t_s), (0, 0)))

        mkp, mvp = _mem_kv(mem2d, w_mem_kv[layer].astype(BF16), mem_k_norm[layer])
        mk_l.append(mkp.reshape(bp, N_MEM, MEM_HEADS, MEM_HD))
        mv_l.append(mvp.reshape(bp, N_MEM, MEM_HEADS, MEM_HD))
        w_o = w_out[layer].astype(BF16)
        xp = _mem_attn_out(mqp, mq_col, mem_q_norm[layer], mkp.reshape(1, bp, N_MEM, MEM_WIDTH),
                           mvp.reshape(1, bp, N_MEM, MEM_WIDTH), 0, False, xp, tp, w_o, 512)
        mem_t = lambda a: a.transpose(0, 1, 3, 4, 2).reshape(DEPTH, bs, MEM_WIDTH, N_MEM)
        xs2 = _mem_attn_out(mqs, mq_col, mem_q_norm[layer], mem_t(cache_mem_k), mem_t(cache_mem_v), layer,
                            True, xs, ts, w_o, SAMPLE_T_PAD).reshape(n_s, D_MODEL)

        wg, wu, wd = (w.astype(BF16) for w in (w_ffn_gate[layer], w_ffn_up[layer], w_ffn_down[layer]))
        ffn_w = (norm_ffn[layer], wg, wu, ffn_conv_w[layer], ffn_conv_b[layer], wd)
        xp, u_last = _ffn_long(xp, *ffn_w, jnp.zeros((bp, CONV_W - 1, D_FF), F32), 256)
        cvp_l.append(u_last[:, SUBLANES - (CONV_W - 1):, :])
        st = state_ffn_conv[layer]
        zrow = jnp.zeros((bs, SAMPLE_T_PAD - 1, D_FF), F32)
        p1 = jnp.concatenate([st[:, 1:2], zrow], axis=1).reshape(n_s, D_FF)
        p2 = jnp.concatenate([st, zrow[:, 1:]], axis=1).reshape(n_s, D_FF)
        xs2, u_s = _ffn_short(xs2, *ffn_w, p1, p2, SAMPLE_T_PAD)
        cvs_l.append(u_s.reshape(bs, SAMPLE_T_PAD, D_FF)[:, t_s - (CONV_W - 1):t_s])
        xs = jnp.where((row_t < t_s)[:, None], xs2, 0.0).reshape(bs, SAMPLE_T_PAD, D_MODEL)

    return (xp, xs[:, :t_s],
            jnp.stack(hgrn_p), jnp.stack(hgrn_s),
            jnp.stack(kp_l), jnp.stack(vp_l), jnp.stack(ikp_l),
            jnp.stack(ks_l), jnp.stack(vs_l), jnp.stack(iks_l),
            jnp.stack(mk_l), jnp.stack(mv_l),
            jnp.stack(cvp_l), jnp.stack(cvs_l))
```
